```python
import jax, jax.numpy as jnp
from jax import lax
import numpy as np

D_MODEL = 1024
BATCH = 2
SEQ = 16384
DEPTH = 2

SB_HEADS = 8
SB_HEAD_DIM = 64
SB_BLOCK = 128
SW_Q_HEADS = 8
SW_KV_HEADS = 2
SW_HEAD_DIM = 64
SW_WINDOW = 128
SW_BLOCK = 128
GDN_HEADS = 4
GDN_HEAD_DIM = 128
GDN_CONV = 4
GDN_CHUNK = 64
D_FF = 2816
FFN_CONV = 3
NORM_EPS = 1e-6

SB_WIDTH = SB_HEADS * SB_HEAD_DIM
SW_Q_WIDTH = SW_Q_HEADS * SW_HEAD_DIM
SW_KV_WIDTH = SW_KV_HEADS * SW_HEAD_DIM
GDN_WIDTH = GDN_HEADS * GDN_HEAD_DIM
IN_SPLITS = (SB_WIDTH, SB_WIDTH, SB_WIDTH,
             SW_Q_WIDTH, SW_KV_WIDTH, SW_KV_WIDTH,
             3 * GDN_WIDTH, GDN_WIDTH, GDN_HEADS, GDN_HEADS,
             D_MODEL, D_MODEL, D_MODEL)
IN_COLS = sum(IN_SPLITS)

kernel_name = 'hybrid_sb_swa_gdn_convffn'


def rms_norm(t, gain):
    t32 = t.astype(jnp.float32)
    y = t32 * lax.rsqrt(jnp.mean(t32 * t32, axis=-1, keepdims=True) + NORM_EPS)
    return (y * gain.astype(jnp.float32)).astype(t.dtype)


def l2_norm(t):
    return t * lax.rsqrt(jnp.sum(t * t, axis=-1, keepdims=True) + NORM_EPS)


def split_columns(t, sizes):
    out = []
    start = 0
    for s in sizes:
        out.append(t[..., start:start + s])
        start += s
    return out


def causal_dwconv(t, w):
    k_width = w.shape[0]
    seq = t.shape[1]
    tp = jnp.pad(t, ((0, 0), (k_width - 1, 0), (0, 0)))
    y = w[0] * tp[:, 0:seq]
    for k in range(1, k_width):
        y = y + w[k] * tp[:, k:k + seq]
    return y


def stick_breaking_attention(q, k, v):
    b, s, h, d = q.shape
    n = s // SB_BLOCK
    scale = d ** -0.5

    def blocks(t):
        return t.astype(jnp.float32).reshape(b, n, SB_BLOCK, h, d).transpose(1, 0, 3, 2, 4)

    qb, kb, vb = blocks(q), blocks(k), blocks(v)
    local = jnp.arange(SB_BLOCK)

    def query_block(i):
        qi = qb[i]
        q_pos = i * SB_BLOCK + local

        def key_step(m, carry):
            acc, log_rem = carry
            j = i - m
            kj = lax.dynamic_index_in_dim(kb, j, 0, keepdims=False)
            vj = lax.dynamic_index_in_dim(vb, j, 0, keepdims=False)
            z = jnp.einsum('bhqd,bhkd->bhqk', qi, kj) * scale
            valid = (j * SB_BLOCK + local)[None, :] < q_pos[:, None]
            log_stay = jnp.where(valid, jax.nn.log_sigmoid(-z), 0.0)
            later = lax.cumsum(log_stay, axis=3, reverse=True) - log_stay
            w = jnp.where(valid, jnp.exp(jax.nn.log_sigmoid(z) + later + log_rem[..., None]), 0.0)
            acc = acc + jnp.einsum('bhqk,bhkd->bhqd', w, vj)
            log_rem = log_rem + jnp.sum(log_stay, axis=-1)
            return acc, log_rem

        init = (jnp.zeros((b, h, SB_BLOCK, d), jnp.float32), jnp.zeros((b, h, SB_BLOCK), jnp.float32))
        acc, _ = lax.fori_loop(0, i + 1, key_step, init)
        return acc

    out = lax.map(query_block, jnp.arange(n))
    return out.transpose(1, 0, 3, 2, 4).reshape(b, s, h * d)


def sliding_window_attention(q, k, v, sinks):
    b, s, _, d = q.shape
    n = s // SW_BLOCK
    g = SW_Q_HEADS // SW_KV_HEADS
    qb = q.reshape(b, n, SW_BLOCK, SW_KV_HEADS, g, d)

    def band(t):
        tb = t.reshape(b, n, SW_BLOCK, SW_KV_HEADS, d)
        prev = jnp.pad(tb, ((0, 0), (1, 0), (0, 0), (0, 0), (0, 0)))[:, :-1]
        return jnp.concatenate([prev, tb], axis=2)

    kb, vb = band(k), band(v)
    scores = jnp.einsum('bnqhgd,bnshd->bnhgqs', qb, kb,
                        preferred_element_type=jnp.float32) * (d ** -0.5)
    qi = jnp.arange(SW_BLOCK)[:, None] + SW_BLOCK
    si = jnp.arange(2 * SW_BLOCK)[None, :]
    dist = qi - si
    key_pos = jnp.arange(n)[:, None] * SW_BLOCK - SW_BLOCK + si
    valid = ((dist >= 0) & (dist < SW_WINDOW))[None] & (key_pos >= 0)[:, None, :]
    slopes = jnp.exp2(-8.0 * jnp.arange(1, SW_Q_HEADS + 1, dtype=jnp.float32) / SW_Q_HEADS)
    slopes = slopes.reshape(SW_KV_HEADS, g, 1, 1)
    scores = scores - slopes * dist.astype(jnp.float32)
    scores = jnp.where(valid[None, :, None, None], scores, -jnp.inf)
    sink = jnp.broadcast_to(sinks.astype(jnp.float32).reshape(SW_KV_HEADS, g, 1, 1), scores.shape[:-1] + (1,))
    p = jax.nn.softmax(jnp.concatenate([scores, sink], axis=-1), axis=-1)[..., :-1]
    o = jnp.einsum('bnhgqs,bnshd->bnqhgd', p.astype(v.dtype), vb)
    return o.reshape(b, s, SW_Q_HEADS * d)


def gated_delta_rule(q, k, v, g, beta):
    b, s, h, dk = q.shape
    dv = v.shape[-1]
    c = GDN_CHUNK
    n = s // c

    def chunks(t):
        return t.reshape(b, n, c, h, t.shape[-1]).transpose(1, 0, 3, 2, 4)

    q = chunks(q * (dk ** -0.5))
    k = chunks(k)
    v = chunks(v)
    g = g.reshape(b, n, c, h).transpose(1, 0, 3, 2)
    beta = beta.reshape(b, n, c, h).transpose(1, 0, 3, 2)
    gc = jnp.cumsum(g, axis=-1)
    idx = jnp.arange(c)
    causal = idx[:, None] >= idx[None, :]
    strict = idx[:, None] > idx[None, :]
    decay = jnp.exp(jnp.where(causal, gc[..., :, None] - gc[..., None, :], -jnp.inf))
    k_beta = k * beta[..., None]
    lower = jnp.where(strict, jnp.einsum('nbhid,nbhjd->nbhij', k_beta, k) * decay, 0.0)
    eye = jnp.eye(c, dtype=jnp.float32)
    t_inv = lax.linalg.triangular_solve(lower + eye, jnp.broadcast_to(eye, lower.shape),
                                        left_side=True, lower=True, unit_diagonal=True)
    u = t_inv @ (v * beta[..., None])
    w = t_inv @ (k_beta * jnp.exp(gc)[..., None])
    a_intra = jnp.where(causal, jnp.einsum('nbhid,nbhjd->nbhij', q, k) * decay, 0.0)
    q_dec = q * jnp.exp(gc)[..., None]
    k_dec = k * jnp.exp(gc[..., -1:] - gc)[..., None]
    g_last = jnp.exp(gc[..., -1])

    def step(state, xs):
        u_i, w_i, q_i, k_i, a_i, gl = xs
        v_new = u_i - w_i @ state
        o = q_i @ state + a_i @ v_new
        state = state * gl[..., None, None] + jnp.swapaxes(k_i, -1, -2) @ v_new
        return state, o

    state0 = jnp.zeros((b, h, dk, dv), jnp.float32)
    _, o = lax.scan(step, state0, (u, w, q_dec, k_dec, a_intra, g_last))
    return o.transpose(1, 0, 3, 2, 4).reshape(b, s, h, dv)


def setup_inputs(seed: int = 0) -> dict:
    key = jax.random.key(seed)
    ks = jax.random.split(key, 20)
    f32 = jnp.float32

    def gain(k, width):
        return 1.0 + 0.05 * jax.random.normal(k, (DEPTH, width), f32)

    dt = jnp.exp(jax.random.uniform(ks[6], (DEPTH, GDN_HEADS), f32, np.log(1e-3), np.log(1e-1)))
    return {
        'x': jax.random.normal(ks[0], (BATCH, SEQ, D_MODEL), f32),
        'ln_mix_pre': gain(ks[1], D_MODEL),
        'w_in': jax.random.normal(ks[2], (DEPTH, D_MODEL, IN_COLS), f32) * D_MODEL ** -0.5,
        'sw_sinks': 0.5 * jax.random.normal(ks[3], (DEPTH, SW_Q_HEADS), f32),
        'gdn_conv': jax.random.normal(ks[4], (DEPTH, GDN_CONV, 3 * GDN_WIDTH), f32) * GDN_CONV ** -0.5,
        'gdn_a_log': jnp.log(jax.random.uniform(ks[5], (DEPTH, GDN_HEADS), f32, 1.0, 16.0)),
        'gdn_dt_bias': jnp.log(jnp.expm1(dt)),
        'gdn_norm': gain(ks[7], GDN_HEAD_DIM),
        'w_branch_a': jax.random.normal(ks[8], (DEPTH, SB_WIDTH, D_MODEL), f32) * SB_WIDTH ** -0.5,
        'w_branch_b': jax.random.normal(ks[9], (DEPTH, SW_Q_WIDTH, D_MODEL), f32) * SW_Q_WIDTH ** -0.5,
        'w_branch_c': jax.random.normal(ks[10], (DEPTH, GDN_WIDTH, D_MODEL), f32) * GDN_WIDTH ** -0.5,
        'w_out': jax.random.normal(ks[11], (DEPTH, D_MODEL, D_MODEL), f32) * D_MODEL ** -0.5,
        'ln_mix_post': gain(ks[12], D_MODEL),
        'ln_ffn_pre': gain(ks[13], D_MODEL),
        'w_up': jax.random.normal(ks[14], (DEPTH, D_MODEL, 2 * D_FF), f32) * D_MODEL ** -0.5,
        'ffn_conv': jax.random.normal(ks[15], (DEPTH, FFN_CONV, 2 * D_FF), f32) * FFN_CONV ** -0.5,
        'w_down': jax.random.normal(ks[16], (DEPTH, D_FF, D_MODEL), f32) * D_FF ** -0.5,
        'ln_ffn_post': gain(ks[17], D_MODEL),
    }


def reference(x, ln_mix_pre, w_in, sw_sinks, gdn_conv, gdn_a_log, gdn_dt_bias, gdn_norm,
              w_branch_a, w_branch_b, w_branch_c, w_out, ln_mix_post, ln_ffn_pre, w_up,
              ffn_conv, w_down, ln_ffn_post):
    b, s, _ = x.shape
    for layer in range(DEPTH):
        h = rms_norm(x, ln_mix_pre[layer])
        proj = h @ w_in[layer]
        (a_q, a_k, a_v, b_q, b_k, b_v, c_qkv, c_z, c_a, c_b,
         gate_a, gate_b, gate_c) = split_columns(proj, IN_SPLITS)

        y_a = stick_breaking_attention(a_q.reshape(b, s, SB_HEADS, SB_HEAD_DIM),
                                       a_k.reshape(b, s, SB_HEADS, SB_HEAD_DIM),
                                       a_v.reshape(b, s, SB_HEADS, SB_HEAD_DIM)).astype(x.dtype)

        y_b = sliding_window_attention(b_q.reshape(b, s, SW_Q_HEADS, SW_HEAD_DIM),
                                       b_k.reshape(b, s, SW_KV_HEADS, SW_HEAD_DIM),
                                       b_v.reshape(b, s, SW_KV_HEADS, SW_HEAD_DIM),
                                       sw_sinks[layer]).astype(x.dtype)

        qkv = jax.nn.silu(causal_dwconv(c_qkv, gdn_conv[layer]))
        c_q, c_k, c_v = split_columns(qkv.astype(jnp.float32), (GDN_WIDTH, GDN_WIDTH, GDN_WIDTH))
        c_q = l2_norm(c_q.reshape(b, s, GDN_HEADS, GDN_HEAD_DIM))
        c_k = l2_norm(c_k.reshape(b, s, GDN_HEADS, GDN_HEAD_DIM))
        c_v = c_v.reshape(b, s, GDN_HEADS, GDN_HEAD_DIM)
        log_decay = -jnp.exp(gdn_a_log[layer].astype(jnp.float32)) * jax.nn.softplus(
            c_a.astype(jnp.float32) + gdn_dt_bias[layer].astype(jnp.float32))
        beta = jax.nn.sigmoid(c_b.astype(jnp.float32))
        o_c = gated_delta_rule(c_q, c_k, c_v, log_decay, beta)
        y_c = (rms_norm(o_c, gdn_norm[layer]) *
               jax.nn.silu(c_z.astype(jnp.float32).reshape(b, s, GDN_HEADS, GDN_HEAD_DIM)))
        y_c = y_c.reshape(b, s, GDN_WIDTH).astype(x.dtype)

        merged = (jax.nn.sigmoid(gate_a) * (y_a @ w_branch_a[layer])
                  + jax.nn.sigmoid(gate_b) * (y_b @ w_branch_b[layer])
                  + jax.nn.sigmoid(gate_c) * (y_c @ w_branch_c[layer]))
        x = x + rms_norm(merged @ w_out[layer], ln_mix_post[layer])

        h = rms_norm(x, ln_ffn_pre[layer])
        hid = causal_dwconv(h @ w_up[layer], ffn_conv[layer])
        f_gate, f_up = hid[..., :D_FF], hid[..., D_FF:]
        f = (jax.nn.gelu(f_gate, approximate=True) * f_up) @ w_down[layer]
        x = x + rms_norm(f, ln_ffn_post[layer])
    return x
```

```python
import functools

import jax
import jax.numpy as jnp
from jax import lax
from jax.experimental import pallas as pl
from jax.experimental.pallas import tpu as pltpu

F32 = jnp.float32
BF16 = jnp.bfloat16
NORM_EPS = 1e-6

D_MODEL = 1024
SB_HEADS, SB_HEAD_DIM = 8, 64
SW_Q_HEADS, SW_KV_HEADS, SW_HEAD_DIM = 8, 2, 64
GDN_HEADS, GDN_HEAD_DIM, GDN_CONV = 4, 128, 4
D_FF, FFN_CONV = 2816, 3
SB_W = SB_HEADS * SB_HEAD_DIM
SW_QW = SW_Q_HEADS * SW_HEAD_DIM
SW_KVW = SW_KV_HEADS * SW_HEAD_DIM
GDN_W = GDN_HEADS * GDN_HEAD_DIM

BLK = 128
LANES = 128
HALF = 64
VMEM_LIMIT = 56 * 1024 * 1024

BF_SBQ, BF_SBK, BF_SBV, BF_SWQ, BF_SWK, BF_SWV, BF_COLS = 0, 512, 1024, 1536, 2048, 2304, 2560
FP_GA, FP_GB, FP_GC, FP_CQKV, FP_CZ, FP_AB, FP_COLS = 0, 1024, 2048, 3072, 4608, 5120, 5632
PROJ_TN = 512
F32_EXP_ZERO = -104.0


def _dot(a, b, precision=None):
    return jnp.dot(a, b, preferred_element_type=F32, precision=precision)


def _dot_nt(a, b):
    return lax.dot_general(a, b, (((1,), (1,)), ((), ())), preferred_element_type=F32)


def _dot_tn(a, b):
    return lax.dot_general(a, b, (((0,), (0,)), ((), ())), preferred_element_type=F32)


def _iota(shape, dim):
    return lax.broadcasted_iota(jnp.int32, shape, dim)


def _rms(t, gain):
    return t * lax.rsqrt(jnp.mean(t * t, axis=-1, keepdims=True) + NORM_EPS) * gain


def _shift_rows(h, prev8, s):
    r = pltpu.roll(h, s, axis=0)
    row = _iota(h.shape, 0)
    for t in range(s):
        r = jnp.where(row == t, prev8[8 - s + t:8 - s + t + 1, :], r)
    return r


def _inproj_kernel(x_ref, g_ref, w_ref, obf_ref, of_ref, xn_ref, *, nbf):
    j = pl.program_id(1)

    @pl.when(j == 0)
    def _():
        xn_ref[...] = _rms(x_ref[...], g_ref[...]).astype(BF16)

    r = _dot(xn_ref[...], w_ref[...])

    @pl.when(j < nbf)
    def _():
        obf_ref[...] = r.astype(BF16)

    @pl.when(j >= nbf)
    def _():
        of_ref[...] = r


def _inproj(x, gain, w, tm):
    t, d = x.shape
    tn = PROJ_TN
    nbf, nfp = BF_COLS // tn, FP_COLS // tn
    return pl.pallas_call(
        functools.partial(_inproj_kernel, nbf=nbf),
        grid=(t // tm, nbf + nfp),
        in_specs=[
            pl.BlockSpec((tm, d), lambda i, j: (i, 0)),
            pl.BlockSpec((1, d), lambda i, j: (0, 0)),
            pl.BlockSpec((d, tn), lambda i, j: (0, j)),
        ],
        out_specs=[
            pl.BlockSpec((tm, tn), lambda i, j: (i, jnp.minimum(j, nbf - 1))),
            pl.BlockSpec((tm, tn), lambda i, j: (i, jnp.maximum(j - nbf, 0))),
        ],
        out_shape=[jax.ShapeDtypeStruct((t, BF_COLS), BF16), jax.ShapeDtypeStruct((t, FP_COLS), F32)],
        scratch_shapes=[pltpu.VMEM((tm, d), BF16)],
        compiler_params=pltpu.CompilerParams(
            dimension_semantics=("arbitrary", "arbitrary"), vmem_limit_bytes=VMEM_LIMIT),
        name="inproj",
    )(x, gain, w)


def _sb_kernel(q_ref, k_ref, v_ref, o_ref, acc_ref, lr_ref):
    i = pl.program_id(2)
    lane = _iota((BLK, LANES), 1)
    row = _iota((BLK, LANES), 0)
    first = lane < HALF
    q = q_ref[...]
    zero = jnp.zeros_like(q)
    qh = (jnp.where(first, q, zero), jnp.where(first, zero, q))
    r2 = _iota((BLK, 2 * LANES), 0)
    c2 = _iota((BLK, 2 * LANES), 1)
    suffix_ones = jnp.where((r2 > c2) | (c2 >= LANES), 1.0, 0.0).astype(BF16)

    def block(j, valid):
        k = k_ref[pl.ds(pl.multiple_of(j * BLK, BLK), BLK), :]
        v = v_ref[pl.ds(pl.multiple_of(j * BLK, BLK), BLK), :]
        pv = []
        for h in range(2):
            z = _dot_nt(qh[h], k)
            log_stay = -(jnp.maximum(z, 0.0) + jnp.log(1.0 + jnp.exp(-jnp.abs(z))))
            log_take = log_stay + z
            if valid is not None:
                log_stay = jnp.where(valid, log_stay, 0.0)
            hi = log_stay.astype(BF16)
            lo = (log_stay - hi.astype(F32)).astype(BF16)
            sums = _dot(hi, suffix_ones) + _dot(lo, suffix_ones)
            later, total = sums[:, :LANES], sums[:, LANES:]
            w = jnp.exp(log_take + later + lr_ref[h])
            if valid is not None:
                w = jnp.where(valid, w, 0.0)
            pv.append(_dot(w.astype(BF16), v))
            lr_ref[h] = lr_ref[h] + total
        return jnp.where(first, pv[0], pv[1])

    lr_ref[...] = jnp.zeros_like(lr_ref)
    acc_ref[...] = block(i, lane < row)

    def cond(c):
        j, live = c
        return jnp.logical_and(j >= 0, live)

    def body(c):
        j, _ = c
        acc_ref[...] += block(j, None)
        return j - 1, jnp.max(lr_ref[...]) > F32_EXP_ZERO

    lax.while_loop(cond, body, (i - 1, jnp.max(lr_ref[...]) > F32_EXP_ZERO))
    o_ref[...] = acc_ref[...].astype(o_ref.dtype)


def _sb_attention(pbf, batch, seq):
    nq = seq // BLK
    cq, ck, cv = BF_SBQ // LANES, BF_SBK // LANES, BF_SBV // LANES
    return pl.pallas_call(
        _sb_kernel,
        grid=(batch, SB_W // LANES, nq),
        in_specs=[
            pl.BlockSpec((BLK, LANES), lambda b, p, i: (b * nq + i, cq + p)),
            pl.BlockSpec((seq, LANES), lambda b, p, i: (b, ck + p)),
            pl.BlockSpec((seq, LANES), lambda b, p, i: (b, cv + p)),
        ],
        out_specs=pl.BlockSpec((BLK, LANES), lambda b, p, i: (b * nq + i, p)),
        out_shape=jax.ShapeDtypeStruct((batch * seq, SB_W), BF16),
        scratch_shapes=[pltpu.VMEM((BLK, LANES), F32), pltpu.VMEM((2, BLK, LANES), F32)],
        compiler_params=pltpu.CompilerParams(
            dimension_semantics=("arbitrary", "arbitrary", "arbitrary"), vmem_limit_bytes=VMEM_LIMIT),
        name="sb_attention",
    )(pbf, pbf, pbf)


def _sw_kernel(sinks_ref, slopes_ref, q_ref, kp_ref, kc_ref, vp_ref, vc_ref, o_ref):
    p = pl.program_id(1)
    i = pl.program_id(2)
    lane = _iota((BLK, LANES), 1)
    row = _iota((BLK, LANES), 0)
    first = lane < HALF
    cur = lane <= row
    dist = jnp.where(cur, row - lane, row - lane + BLK).astype(F32)
    has_prev = i > 0
    q = q_ref[...]
    zero = jnp.zeros_like(q)
    qh = (jnp.where(first, q, zero), jnp.where(first, zero, q))
    kp, kc, vp, vc = kp_ref[...], kc_ref[...], vp_ref[...], vc_ref[...]
    out = []
    for h in range(2):
        sink = sinks_ref[2 * p + h]
        slope = slopes_ref[2 * p + h]
        s = jnp.where(cur, _dot_nt(qh[h], kc), _dot_nt(qh[h], kp)) - slope * dist
        s = jnp.where(jnp.logical_or(cur, has_prev), s, -jnp.inf)
        m = jnp.maximum(jnp.max(s, axis=-1, keepdims=True), sink)
        e = jnp.exp(s - m)
        denom = jnp.sum(e, axis=-1, keepdims=True) + jnp.exp(sink - m)
        prob = (e / denom).astype(BF16)
        pz = jnp.zeros_like(prob)
        out.append(_dot(jnp.where(cur, prob, pz), vc) + _dot(jnp.where(cur, pz, prob), vp))
    o_ref[...] = jnp.where(first, out[0], out[1]).astype(o_ref.dtype)


def _sw_attention(pbf, sinks, slopes, batch, seq):
    nq = seq // BLK
    cq, ck, cv = BF_SWQ // LANES, BF_SWK // LANES, BF_SWV // LANES
    pairs_per_kv = (SW_Q_HEADS // SW_KV_HEADS) // 2

    def prev(b, p, i):
        return b * nq + jnp.maximum(i - 1, 0)

    smem = pl.BlockSpec(memory_space=pltpu.SMEM)
    return pl.pallas_call(
        _sw_kernel,
        grid=(batch, SW_QW // LANES, nq),
        in_specs=[
            smem, smem,
            pl.BlockSpec((BLK, LANES), lambda b, p, i: (b * nq + i, cq + p)),
            pl.BlockSpec((BLK, LANES), lambda b, p, i: (prev(b, p, i), ck + p // pairs_per_kv)),
            pl.BlockSpec((BLK, LANES), lambda b, p, i: (b * nq + i, ck + p // pairs_per_kv)),
            pl.BlockSpec((BLK, LANES), lambda b, p, i: (prev(b, p, i), cv + p // pairs_per_kv)),
            pl.BlockSpec((BLK, LANES), lambda b, p, i: (b * nq + i, cv + p // pairs_per_kv)),
        ],
        out_specs=pl.BlockSpec((BLK, LANES), lambda b, p, i: (b * nq + i, p)),
        out_shape=jax.ShapeDtypeStruct((batch * seq, SW_QW), BF16),
        compiler_params=pltpu.CompilerParams(
            dimension_semantics=("arbitrary", "arbitrary", "arbitrary"), vmem_limit_bytes=VMEM_LIMIT),
        name="sw_attention",
    )(sinks, slopes, pbf, pbf, pbf, pbf, pbf)


GDN_CHUNK = 128
HIGHEST = lax.Precision.HIGHEST


def _unit_lower_inverse(lower):
    n = lower.shape[0]
    r = _iota((n, n), 0)
    c = _iota((n, n), 1)
    eye = jnp.where(r == c, 1.0, 0.0).astype(F32)
    x = eye - jnp.where((r >> 1) == (c >> 1), lower, 0.0)
    s = 2
    while s < n:
        ls = s.bit_length() - 1
        m = jnp.where(((r >> (ls + 1)) == (c >> (ls + 1))) & ((r >> ls) != (c >> ls)), lower, 0.0)
        x = x - _dot(_dot(x, m, HIGHEST), x, HIGHEST)
        s *= 2
    return x


def _gdn_kernel(q_ref, k_ref, v_ref, z_ref, ab_ref, conv_ref, small_ref, norm_ref, o_ref,
                carry_ref, state_ref, yq_ref, yk_ref, yv_ref, gb_ref, *, rows):
    step = pl.program_id(1)

    @pl.when(step == 0)
    def _():
        carry_ref[...] = jnp.zeros_like(carry_ref)
        state_ref[...] = jnp.zeros_like(state_ref)

    for idx, (src, dst) in enumerate(((q_ref, yq_ref), (k_ref, yk_ref), (v_ref, yv_ref))):
        x = src[...]
        cols = slice(idx * GDN_W, (idx + 1) * GDN_W)
        prev8 = carry_ref[:, cols]
        w = conv_ref[:, cols]
        y = w[GDN_CONV - 1:GDN_CONV, :] * x
        for s in range(1, GDN_CONV):
            y = y + w[GDN_CONV - 1 - s:GDN_CONV - s, :] * _shift_rows(x, prev8, s)
        carry_ref[:, cols] = x[rows - 8:rows, :]
        y = y * jax.nn.sigmoid(y)
        for h in range(GDN_HEADS):
            yh = y[:, h * GDN_HEAD_DIM:(h + 1) * GDN_HEAD_DIM]
            if idx < 2:
                yh = yh * lax.rsqrt(jnp.sum(yh * yh, axis=-1, keepdims=True) + NORM_EPS)
            if idx == 0:
                yh = yh * (GDN_HEAD_DIM ** -0.5)
            dst[:, h * GDN_HEAD_DIM:(h + 1) * GDN_HEAD_DIM] = yh

    ab = ab_ref[...]
    a_log = small_ref[0:1, :]
    dt_bias = small_ref[1:2, :]
    zab = ab + dt_bias
    softplus = jnp.maximum(zab, 0.0) + jnp.log(1.0 + jnp.exp(-jnp.abs(zab)))
    lanes = _iota(ab.shape, 1)
    gb_ref[...] = jnp.where(lanes < GDN_HEADS, -jnp.exp(a_log) * softplus, jax.nn.sigmoid(ab))

    n = GDN_CHUNK
    r = _iota((n, n), 0)
    c = _iota((n, n), 1)
    causal = r >= c
    strict = r > c
    lower_ones = jnp.where(causal, 1.0, 0.0).astype(F32)
    ones = jnp.ones((n, n), F32)
    gain = norm_ref[...]

    def chunk(ci, carry):
        r0 = pl.multiple_of(ci * n, n)
        gbeta = gb_ref[pl.ds(r0, n), :]
        for h in range(GDN_HEADS):
            hs = slice(h * GDN_HEAD_DIM, (h + 1) * GDN_HEAD_DIM)
            q = yq_ref[pl.ds(r0, n), hs]
            k = yk_ref[pl.ds(r0, n), hs]
            v = yv_ref[pl.ds(r0, n), hs]
            g_b = jnp.broadcast_to(gbeta[:, h:h + 1], (n, n))
            beta_b = jnp.broadcast_to(gbeta[:, GDN_HEADS + h:GDN_HEADS + h + 1], (n, n))
            gc = _dot(lower_ones, g_b, HIGHEST)
            gc_row = _dot(ones, jnp.where(r <= c, g_b, 0.0), HIGHEST)
            g_tot = _dot(ones, g_b, HIGHEST)
            decay = jnp.exp(jnp.where(causal, gc - gc_row, -jnp.inf))
            k_beta = k * beta_b
            lower = jnp.where(strict, _dot_nt(k_beta, k) * decay, 0.0)
            a_intra = jnp.where(causal, _dot_nt(q, k) * decay, 0.0)
            t_inv = _unit_lower_inverse(lower)
            e_gc = jnp.exp(gc)
            u = _dot(t_inv, v * beta_b)
            w = _dot(t_inv, k_beta * e_gc)
            q_dec = q * e_gc
            k_dec = k * jnp.exp(g_tot - gc)
            state = state_ref[h]
            v_new = u - _dot(w, state)
            o = _dot(q_dec, state) + _dot(a_intra, v_new)
            state_ref[h] = state * jnp.exp(g_tot) + _dot_tn(k_dec, v_new)
            zg = z_ref[pl.ds(r0, n), hs]
            o_ref[pl.ds(r0, n), hs] = (_rms(o, gain) * (zg * jax.nn.sigmoid(zg))).astype(o_ref.dtype)
        return carry

    lax.fori_loop(0, rows // n, chunk, 0)


def _gdn(pfp, conv_w, small, norm, batch, seq, rows):
    steps = seq // rows
    w = GDN_W
    cq = FP_CQKV // w

    def blk(col):
        return pl.BlockSpec((rows, w), lambda b, i: (b * steps + i, col))

    return pl.pallas_call(
        functools.partial(_gdn_kernel, rows=rows),
        grid=(batch, steps),
        in_specs=[
            blk(cq), blk(cq + 1), blk(cq + 2), blk(FP_CZ // w),
            pl.BlockSpec((rows, LANES), lambda b, i: (b * steps + i, FP_AB // LANES)),
            pl.BlockSpec((GDN_CONV, 3 * w), lambda b, i: (0, 0)),
            pl.BlockSpec((8, LANES), lambda b, i: (0, 0)),
            pl.BlockSpec((1, GDN_HEAD_DIM), lambda b, i: (0, 0)),
        ],
        out_specs=pl.BlockSpec((rows, w), lambda b, i: (b * steps + i, 0)),
        out_shape=jax.ShapeDtypeStruct((batch * seq, w), BF16),
        scratch_shapes=[
            pltpu.VMEM((8, 3 * w), F32),
            pltpu.VMEM((GDN_HEADS, GDN_HEAD_DIM, GDN_HEAD_DIM), F32),
            pltpu.VMEM((rows, w), F32), pltpu.VMEM((rows, w), F32), pltpu.VMEM((rows, w), F32),
            pltpu.VMEM((rows, LANES), F32),
        ],
        compiler_params=pltpu.CompilerParams(
            dimension_semantics=("arbitrary", "arbitrary"), vmem_limit_bytes=VMEM_LIMIT),
        name="gdn",
    )(pfp, pfp, pfp, pfp, pfp, conv_w, small, norm)


def _merge_kernel(ya_ref, yb_ref, yc_ref, ga_ref, gb_ref, gc_ref, x_ref,
                  wa_ref, wb_ref, wc_ref, wo_ref, gain_ref, o_ref):
    merged = (jax.nn.sigmoid(ga_ref[...]) * _dot(ya_ref[...], wa_ref[...])
              + jax.nn.sigmoid(gb_ref[...]) * _dot(yb_ref[...], wb_ref[...])
              + jax.nn.sigmoid(gc_ref[...]) * _dot(yc_ref[...], wc_ref[...]))
    r = _dot(merged.astype(BF16), wo_ref[...])
    o_ref[...] = x_ref[...] + _rms(r, gain_ref[...])


def _merge(ya, yb, yc, pfp, x, wa, wb, wc, wo, gain, tm):
    t, d = x.shape

    def rows(width, col=0):
        return pl.BlockSpec((tm, width), lambda i: (i, col))

    def whole(a):
        return pl.BlockSpec(a.shape, lambda i: (0, 0))

    return pl.pallas_call(
        _merge_kernel,
        grid=(t // tm,),
        in_specs=[rows(SB_W), rows(SW_QW), rows(GDN_W),
                  rows(d, FP_GA // d), rows(d, FP_GB // d), rows(d, FP_GC // d), rows(d),
                  whole(wa), whole(wb), whole(wc), whole(wo), whole(gain)],
        out_specs=rows(d),
        out_shape=jax.ShapeDtypeStruct((t, d), F32),
        compiler_params=pltpu.CompilerParams(
            dimension_semantics=("arbitrary",), vmem_limit_bytes=VMEM_LIMIT),
        name="merge",
    )(ya, yb, yc, pfp, pfp, pfp, x, wa, wb, wc, wo, gain)


FFN_CK = 256


def _ffn_kernel(x_ref, gpre_ref, wup_ref, conv_ref, wdn_ref, gpost_ref, o_ref, carry_ref, *, tm):
    @pl.when(pl.program_id(1) == 0)
    def _():
        carry_ref[...] = jnp.zeros_like(carry_ref)

    x = x_ref[...]
    hn = _rms(x, gpre_ref[...]).astype(BF16)

    def conv(col):
        cs = slice(col, col + FFN_CK)
        hid = _dot(hn, wup_ref[:, cs])
        prev8 = carry_ref[:, cs]
        w = conv_ref[:, cs]
        y = w[FFN_CONV - 1:FFN_CONV, :] * hid
        for s in range(1, FFN_CONV):
            y = y + w[FFN_CONV - 1 - s:FFN_CONV - s, :] * _shift_rows(hid, prev8, s)
        carry_ref[:, cs] = hid[tm - 8:tm, :]
        return y

    acc = jnp.zeros((tm, D_MODEL), F32)
    for ci in range(D_FF // FFN_CK):
        f_gate = conv(ci * FFN_CK)
        f_up = conv(D_FF + ci * FFN_CK)
        f = jax.nn.gelu(f_gate, approximate=True) * f_up
        acc = acc + _dot(f.astype(BF16), wdn_ref[ci * FFN_CK:(ci + 1) * FFN_CK, :])
    o_ref[...] = x + _rms(acc, gpost_ref[...])


def _ffn(x, gpre, wup, conv_w, wdn, gpost, batch, seq, tm):
    steps = seq // tm
    d = x.shape[1]

    def whole(a):
        return pl.BlockSpec(a.shape, lambda b, i: (0, 0))

    return pl.pallas_call(
        functools.partial(_ffn_kernel, tm=tm),
        grid=(batch, steps),
        in_specs=[pl.BlockSpec((tm, d), lambda b, i: (b * steps + i, 0)),
                  whole(gpre), whole(wup), whole(conv_w), whole(wdn), whole(gpost)],
        out_specs=pl.BlockSpec((tm, d), lambda b, i: (b * steps + i, 0)),
        out_shape=jax.ShapeDtypeStruct(x.shape, F32),
        scratch_shapes=[pltpu.VMEM((8, 2 * D_FF), F32)],
        compiler_params=pltpu.CompilerParams(
            dimension_semantics=("arbitrary", "arbitrary"), vmem_limit_bytes=VMEM_LIMIT),
        name="ffn",
    )(x, gpre, wup, conv_w, wdn, gpost)


def _pack_w_in(w):
    o = 0
    parts = {}
    for name, width in (("aq", SB_W), ("ak", SB_W), ("av", SB_W), ("bq", SW_QW), ("bk", SW_KVW),
                        ("bv", SW_KVW), ("cqkv", 3 * GDN_W), ("cz", GDN_W), ("ca", GDN_HEADS),
                        ("cb", GDN_HEADS), ("ga", D_MODEL), ("gb", D_MODEL), ("gc", D_MODEL)):
        parts[name] = w[:, o:o + width]
        o += width

    def dup(t):
        return jnp.concatenate([t[:, h * HALF:(h + 1) * HALF] for h in range(SW_KV_HEADS) for _ in range(2)], axis=1)

    pad = jnp.zeros((w.shape[0], FP_COLS - FP_AB - 2 * GDN_HEADS), w.dtype)
    cols = [parts["aq"] * (SB_HEAD_DIM ** -0.5), parts["ak"], parts["av"],
            parts["bq"] * (SW_HEAD_DIM ** -0.5), dup(parts["bk"]), dup(parts["bv"]),
            parts["ga"], parts["gb"], parts["gc"], parts["cqkv"], parts["cz"],
            parts["ca"], parts["cb"], pad]
    return jnp.concatenate(cols, axis=1).astype(BF16)


def _layer(x, batch, seq, p):
    pbf, pfp = _inproj(x, p["ln_mix_pre"], p["w_in"], tm=1024)
    ya = _sb_attention(pbf, batch, seq)
    yb = _sw_attention(pbf, p["sinks"], p["slopes"], batch, seq)
    yc = _gdn(pfp, p["gdn_conv"], p["gdn_small"], p["gdn_norm"], batch, seq, rows=256)
    x = _merge(ya, yb, yc, pfp, x, p["wa"], p["wb"], p["wc"], p["wo"], p["ln_mix_post"], tm=512)
    return _ffn(x, p["ln_ffn_pre"], p["w_up"], p["ffn_conv"], p["w_down"], p["ln_ffn_post"],
                batch, seq, tm=512)


def kernel(x, ln_mix_pre, w_in, sw_sinks, gdn_conv, gdn_a_log, gdn_dt_bias, gdn_norm, w_branch_a,
           w_branch_b, w_branch_c, w_out, ln_mix_post, ln_ffn_pre, w_up, ffn_conv, w_down, ln_ffn_post):
    batch, seq, d = x.shape
    depth = w_in.shape[0]
    slopes = jnp.exp2(-8.0 * jnp.arange(1, SW_Q_HEADS + 1, dtype=F32) / SW_Q_HEADS)
    h = x.reshape(batch * seq, d)
    for l in range(depth):
        small = jnp.zeros((8, LANES), F32)
        small = small.at[0, :GDN_HEADS].set(gdn_a_log[l]).at[1, :GDN_HEADS].set(gdn_dt_bias[l])
        p = {
            "ln_mix_pre": ln_mix_pre[l][None, :], "w_in": _pack_w_in(w_in[l]),
            "sinks": sw_sinks[l], "slopes": slopes,
            "gdn_conv": gdn_conv[l], "gdn_small": small, "gdn_norm": gdn_norm[l][None, :],
            "wa": w_branch_a[l].astype(BF16), "wb": w_branch_b[l].astype(BF16),
            "wc": w_branch_c[l].astype(BF16), "wo": w_out[l].astype(BF16),
            "ln_mix_post": ln_mix_post[l][None, :], "ln_ffn_pre": ln_ffn_pre[l][None, :],
            "w_up": w_up[l].astype(BF16), "ffn_conv": ffn_conv[l], "w_down": w_down[l].astype(BF16),
            "ln_ffn_post": ln_ffn_post[l][None, :],
        }
        h = _layer(h, batch, seq, p)
    return h.reshape(batch, seq, d)
```

```python
import functools

import jax
import jax.numpy as jnp
from jax import lax
from jax.experimental import pallas as pl
from jax.experimental.pallas import tpu as pltpu

F32 = jnp.float32
BF16 = jnp.bfloat16
NORM_EPS = 1e-6

D_MODEL = 1024
SB_HEADS, SB_HEAD_DIM = 8, 64
SW_Q_HEADS, SW_KV_HEADS, SW_HEAD_DIM = 8, 2, 64
GDN_HEADS, GDN_HEAD_DIM, GDN_CONV = 4, 128, 4
D_FF, FFN_CONV = 2816, 3
SB_W = SB_HEADS * SB_HEAD_DIM
SW_QW = SW_Q_HEADS * SW_HEAD_DIM
SW_KVW = SW_KV_HEADS * SW_HEAD_DIM
GDN_W = GDN_HEADS * GDN_HEAD_DIM

BLK = 128
LANES = 128
HALF = 64
VMEM_LIMIT = 56 * 1024 * 1024

BF_SBQ, BF_SBK, BF_SBV, BF_SWQ, BF_SWK, BF_SWV, BF_COLS = 0, 512, 1024, 1536, 2048, 2304, 2560
FP_GA, FP_GB, FP_GC, FP_CQKV, FP_CZ, FP_AB, FP_COLS = 0, 1024, 2048, 3072, 4608, 5120, 5632
PROJ_TN = 512
F32_EXP_ZERO = -104.0


def _dot(a, b):
    return jnp.dot(a, b, preferred_element_type=F32)


def _dot_nt(a, b):
    return lax.dot_general(a, b, (((1,), (1,)), ((), ())), preferred_element_type=F32)


def _iota(shape, dim):
    return lax.broadcasted_iota(jnp.int32, shape, dim)


def _rms(t, gain):
    return t * lax.rsqrt(jnp.mean(t * t, axis=-1, keepdims=True) + NORM_EPS) * gain


def _shift_rows(h, prev8, s):
    r = pltpu.roll(h, s, axis=0)
    row = _iota(h.shape, 0)
    for t in range(s):
        r = jnp.where(row == t, prev8[8 - s + t:8 - s + t + 1, :], r)
    return r


def _inproj_kernel(x_ref, g_ref, w_ref, obf_ref, of_ref, xn_ref, *, nbf):
    j = pl.program_id(1)

    @pl.when(j == 0)
    def _():
        xn_ref[...] = _rms(x_ref[...], g_ref[...]).astype(BF16)

    r = _dot(xn_ref[...], w_ref[...])

    @pl.when(j < nbf)
    def _():
        obf_ref[...] = r.astype(BF16)

    @pl.when(j >= nbf)
    def _():
        of_ref[...] = r


def _inproj(x, gain, w, tm):
    t, d = x.shape
    tn = PROJ_TN
    nbf, nfp = BF_COLS // tn, FP_COLS // tn
    return pl.pallas_call(
        functools.partial(_inproj_kernel, nbf=nbf),
        grid=(t // tm, nbf + nfp),
        in_specs=[
            pl.BlockSpec((tm, d), lambda i, j: (i, 0)),
            pl.BlockSpec((1, d), lambda i, j: (0, 0)),
            pl.BlockSpec((d, tn), lambda i, j: (0, j)),
        ],
        out_specs=[
            pl.BlockSpec((tm, tn), lambda i, j: (i, jnp.minimum(j, nbf - 1))),
            pl.BlockSpec((tm, tn), lambda i, j: (i, jnp.maximum(j - nbf, 0))),
        ],
        out_shape=[jax.ShapeDtypeStruct((t, BF_COLS), BF16), jax.ShapeDtypeStruct((t, FP_COLS), F32)],
        scratch_shapes=[pltpu.VMEM((tm, d), BF16)],
        compiler_params=pltpu.CompilerParams(
            dimension_semantics=("arbitrary", "arbitrary"), vmem_limit_bytes=VMEM_LIMIT),
        name="inproj",
    )(x, gain, w)


SB_TQ = 2 * BLK


def _sb_kernel(q_ref, k_ref, v_ref, o_ref, acc_ref, lr_ref):
    ti = pl.program_id(2)
    lane = _iota((BLK, LANES), 1)
    row = _iota((BLK, LANES), 0)
    first = lane < HALF
    tri = lane < row
    full = lane >= 0
    zero_tile = jnp.zeros((BLK, LANES), F32)
    r2 = _iota((BLK, 2 * LANES), 0)
    c2 = _iota((BLK, 2 * LANES), 1)
    suffix_ones = jnp.where((r2 > c2) | (c2 >= LANES), 1.0, 0.0).astype(BF16)

    def split_heads(q):
        sel = jnp.concatenate([first] * (q.shape[0] // BLK), axis=0)
        zero = jnp.zeros_like(q)
        return jnp.where(sel, q, zero), jnp.where(sel, zero, q)

    def log_parts(z):
        log_stay = -(jnp.maximum(z, 0.0) + jnp.log(1.0 + jnp.exp(-jnp.abs(z))))
        return log_stay, log_stay + z

    def suffix_sums(parts):
        hi = [x.astype(BF16) for x in parts]
        lo = [(x - h.astype(F32)).astype(BF16) for x, h in zip(parts, hi)]
        sums = _dot(jnp.concatenate(hi + lo, axis=0), suffix_ones)
        m_all = sum(x.shape[0] for x in parts)
        out, o = [], 0
        for x in parts:
            s = sums[o:o + x.shape[0]] + sums[m_all + o:m_all + o + x.shape[0]]
            out.append((s[:, :LANES], s[:, LANES:]))
            o += x.shape[0]
        return out

    def kv(j):
        off = pl.multiple_of(j * BLK, BLK)
        return k_ref[pl.ds(off, BLK), :], v_ref[pl.ds(off, BLK), :]

    def generic_block(s, j, valid):
        q0, q1 = split_heads(q_ref[s * BLK:(s + 1) * BLK, :])
        k, v = kv(j)
        log_stay, log_take = log_parts(_dot_nt(jnp.concatenate([q0, q1], axis=0), k))
        if valid is not None:
            vm = jnp.concatenate([valid, valid], axis=0)
            log_stay = jnp.where(vm, log_stay, 0.0)
        (later, total), = suffix_sums([log_stay])
        w = jnp.exp(log_take + later + lr_ref[s])
        if valid is not None:
            w = jnp.where(vm, w, 0.0)
        pv = _dot(w.astype(BF16), v)
        acc_ref[s] += jnp.where(first, pv[:BLK], pv[BLK:])
        lr_ref[s] += total

    def tail(s, j0):
        def cond(c):
            j, live = c
            return jnp.logical_and(j >= 0, live)

        def body(c):
            j, _ = c
            generic_block(s, j, None)
            return j - 1, jnp.max(lr_ref[s]) > F32_EXP_ZERO

        lax.while_loop(cond, body, (j0, jnp.max(lr_ref[s]) > F32_EXP_ZERO))

    @pl.when(ti == 0)
    def _():
        for s in range(2):
            acc_ref[s] = jnp.zeros((BLK, LANES), F32)
            lr_ref[s] = jnp.zeros((2 * BLK, LANES), F32)
            generic_block(s, 2 * ti + s, tri)
            tail(s, 2 * ti + s - 1)

    @pl.when(ti > 0)
    def _():
        q0, q1 = split_heads(q_ref[...])
        base = pl.multiple_of((2 * ti - 2) * BLK, BLK)
        kb = [k_ref[pl.ds(base + t * BLK, BLK), :] for t in range(4)]
        vb = [v_ref[pl.ds(base + t * BLK, BLK), :] for t in range(4)]
        l_ab = jnp.concatenate([q0, q1], axis=0)
        l_a = jnp.concatenate([q0[:BLK], q1[:BLK]], axis=0)
        l_b = jnp.concatenate([q0[BLK:], q1[BLK:]], axis=0)
        ls0, lt0 = log_parts(_dot_nt(l_a, kb[0]))
        ls1, lt1 = log_parts(_dot_nt(l_ab, kb[1]))
        ls2, lt2 = log_parts(_dot_nt(l_ab, kb[2]))
        ls3, lt3 = log_parts(_dot_nt(l_b, kb[3]))
        m2 = jnp.concatenate([tri, full, tri, full], axis=0)
        m3 = jnp.concatenate([tri, tri], axis=0)
        ls2 = jnp.where(m2, ls2, 0.0)
        ls3 = jnp.where(m3, ls3, 0.0)
        (lat0, tot0), (lat1, tot1), (lat2, tot2), (lat3, tot3) = suffix_sums([ls0, ls1, ls2, ls3])
        lrb2 = jnp.concatenate([zero_tile, tot3[:BLK], zero_tile, tot3[BLK:]], axis=0)
        lrb1 = lrb2 + tot2
        lre1 = lrb1 + tot1
        lrb0 = jnp.concatenate([lre1[:BLK], lre1[2 * BLK:3 * BLK]], axis=0)
        w3 = jnp.where(m3, jnp.exp(lt3 + lat3), 0.0)
        w2 = jnp.where(m2, jnp.exp(lt2 + lat2 + lrb2), 0.0)
        w1 = jnp.exp(lt1 + lat1 + lrb1)
        w0 = jnp.exp(lt0 + lat0 + lrb0)
        pv12 = _dot(w1.astype(BF16), vb[1]) + _dot(w2.astype(BF16), vb[2])
        pv0 = _dot(w0.astype(BF16), vb[0])
        pv3 = _dot(w3.astype(BF16), vb[3])
        acc_ref[0] = jnp.where(first, pv12[:BLK] + pv0[:BLK], pv12[2 * BLK:3 * BLK] + pv0[BLK:])
        acc_ref[1] = jnp.where(first, pv12[BLK:2 * BLK] + pv3[:BLK], pv12[3 * BLK:] + pv3[BLK:])
        lr_ref[0] = lrb0 + tot0
        lr_ref[1] = jnp.concatenate([lre1[BLK:2 * BLK], lre1[3 * BLK:]], axis=0)

        @pl.when(jnp.max(lr_ref[...]) > F32_EXP_ZERO)
        def _():
            tail(0, 2 * ti - 3)
            tail(1, 2 * ti - 2)

    o_ref[:BLK, :] = acc_ref[0].astype(o_ref.dtype)
    o_ref[BLK:, :] = acc_ref[1].astype(o_ref.dtype)


def _sb_attention(pbf, batch, seq):
    nt = seq // SB_TQ
    cq, ck, cv = BF_SBQ // LANES, BF_SBK // LANES, BF_SBV // LANES
    return pl.pallas_call(
        _sb_kernel,
        grid=(batch, SB_W // LANES, nt),
        in_specs=[
            pl.BlockSpec((SB_TQ, LANES), lambda b, p, i: (b * nt + i, cq + p)),
            pl.BlockSpec((seq, LANES), lambda b, p, i: (b, ck + p)),
            pl.BlockSpec((seq, LANES), lambda b, p, i: (b, cv + p)),
        ],
        out_specs=pl.BlockSpec((SB_TQ, LANES), lambda b, p, i: (b * nt + i, p)),
        out_shape=jax.ShapeDtypeStruct((batch * seq, SB_W), BF16),
        scratch_shapes=[pltpu.VMEM((2, BLK, LANES), F32), pltpu.VMEM((2, 2 * BLK, LANES), F32)],
        compiler_params=pltpu.CompilerParams(
            dimension_semantics=("arbitrary", "arbitrary", "arbitrary"), vmem_limit_bytes=VMEM_LIMIT),
        name="sb_attention",
    )(pbf, pbf, pbf)


SW_TQ = 2 * BLK


def _sw_kernel(sinks_ref, slopes_ref, q_ref, kp_ref, kc_ref, vp_ref, vc_ref, o_ref):
    ti = pl.program_id(1)
    lane = _iota((BLK, LANES), 1)
    row = _iota((BLK, LANES), 0)
    first = lane < HALF
    cur = lane <= row
    dist = jnp.where(cur, row - lane, row - lane + BLK).astype(F32)
    group = SW_Q_HEADS // SW_KV_HEADS
    for s in range(SW_TQ // BLK):
        rs = slice(s * BLK, (s + 1) * BLK)
        for hk in range(SW_KV_HEADS):
            ks = slice(hk * LANES, (hk + 1) * LANES)
            kc, vc = kc_ref[rs, ks], vc_ref[rs, ks]
            if s == 0:
                kp, vp = kp_ref[:, ks], vp_ref[:, ks]
            else:
                kp, vp = kc_ref[(s - 1) * BLK:s * BLK, ks], vc_ref[(s - 1) * BLK:s * BLK, ks]
            pairs = range(hk * group // 2, (hk + 1) * group // 2)
            lhs = []
            for p in pairs:
                qp = q_ref[rs, p * LANES:(p + 1) * LANES]
                zero = jnp.zeros_like(qp)
                lhs += [jnp.where(first, qp, zero), jnp.where(first, zero, qp)]
            lhs = jnp.concatenate(lhs, axis=0)
            s_cur = _dot_nt(lhs, kc)
            s_prev = _dot_nt(lhs, kp)
            prob_c, prob_p = [], []
            for g in range(group):
                head = hk * group + g
                sink = sinks_ref[head]
                gs = slice(g * BLK, (g + 1) * BLK)
                sc = jnp.where(cur, s_cur[gs], s_prev[gs]) - slopes_ref[head] * dist
                if s == 0:
                    sc = jnp.where(jnp.logical_or(cur, ti > 0), sc, -jnp.inf)
                m = jnp.maximum(jnp.max(sc, axis=-1, keepdims=True), sink)
                e = jnp.exp(sc - m)
                denom = jnp.sum(e, axis=-1, keepdims=True) + jnp.exp(sink - m)
                prob = (e / denom).astype(BF16)
                pz = jnp.zeros_like(prob)
                prob_c.append(jnp.where(cur, prob, pz))
                prob_p.append(jnp.where(cur, pz, prob))
            pv = _dot(jnp.concatenate(prob_c, axis=0), vc) + _dot(jnp.concatenate(prob_p, axis=0), vp)
            for t, p in enumerate(pairs):
                o_ref[rs, p * LANES:(p + 1) * LANES] = jnp.where(
                    first, pv[2 * t * BLK:(2 * t + 1) * BLK], pv[(2 * t + 1) * BLK:(2 * t + 2) * BLK]
                ).astype(o_ref.dtype)


def _sw_attention(pbf, sinks, slopes, batch, seq):
    nt = seq // SW_TQ
    kvw = 2 * SW_KVW
    cq, ck, cv = BF_SWQ // SW_QW, BF_SWK // kvw, BF_SWV // kvw
    per = SW_TQ // BLK

    def prev(b, i):
        return b * nt * per + jnp.maximum(i * per - 1, 0)

    smem = pl.BlockSpec(memory_space=pltpu.SMEM)
    return pl.pallas_call(
        _sw_kernel,
        grid=(batch, nt),
        in_specs=[
            smem, smem,
            pl.BlockSpec((SW_TQ, SW_QW), lambda b, i: (b * nt + i, cq)),
            pl.BlockSpec((BLK, kvw), lambda b, i: (prev(b, i), ck)),
            pl.BlockSpec((SW_TQ, kvw), lambda b, i: (b * nt + i, ck)),
            pl.BlockSpec((BLK, kvw), lambda b, i: (prev(b, i), cv)),
            pl.BlockSpec((SW_TQ, kvw), lambda b, i: (b * nt + i, cv)),
        ],
        out_specs=pl.BlockSpec((SW_TQ, SW_QW), lambda b, i: (b * nt + i, 0)),
        out_shape=jax.ShapeDtypeStruct((batch * seq, SW_QW), BF16),
        compiler_params=pltpu.CompilerParams(
            dimension_semantics=("arbitrary", "arbitrary"), vmem_limit_bytes=VMEM_LIMIT),
        name="sw_attention",
    )(sinks, slopes, pbf, pbf, pbf, pbf, pbf)


GDN_CHUNK = 128


def _split3(x):
    x1 = x.astype(BF16)
    r1 = x - x1.astype(F32)
    x2 = r1.astype(BF16)
    x3 = (r1 - x2.astype(F32)).astype(BF16)
    return x1, x2, x3


def _gdn_kernel(q_ref, k_ref, v_ref, z_ref, ab_ref, conv_ref, small_ref, norm_ref, o_ref,
                carry_ref, state_ref, yq_ref, yk_ref, yv_ref, gb_ref,
                u_ref, w_ref, qd_ref, kdt_ref, a_ref, gl_ref, *, rows):
    n = GDN_CHUNK
    nchunks = rows // n
    step = pl.program_id(1)

    @pl.when(step == 0)
    def _():
        carry_ref[...] = jnp.zeros_like(carry_ref)
        state_ref[...] = jnp.zeros_like(state_ref)

    for idx, (src, dst) in enumerate(((q_ref, yq_ref), (k_ref, yk_ref), (v_ref, yv_ref))):
        cols = slice(idx * GDN_W, (idx + 1) * GDN_W)
        w = conv_ref[:, cols]
        for ci in range(nchunks):
            x = src[ci * n:(ci + 1) * n, :]
            prev8 = carry_ref[:, cols] if ci == 0 else src[ci * n - 8:ci * n, :]
            y = w[GDN_CONV - 1:GDN_CONV, :] * x
            for s in range(1, GDN_CONV):
                y = y + w[GDN_CONV - 1 - s:GDN_CONV - s, :] * _shift_rows(x, prev8, s)
            y = y * jax.nn.sigmoid(y)
            for h in range(GDN_HEADS):
                yh = y[:, h * GDN_HEAD_DIM:(h + 1) * GDN_HEAD_DIM]
                if idx < 2:
                    yh = yh * lax.rsqrt(jnp.sum(yh * yh, axis=-1, keepdims=True) + NORM_EPS)
                if idx == 0:
                    yh = yh * (GDN_HEAD_DIM ** -0.5)
                dst[ci * n:(ci + 1) * n, h * GDN_HEAD_DIM:(h + 1) * GDN_HEAD_DIM] = yh
        carry_ref[:, cols] = src[rows - 8:rows, :]

    ab = ab_ref[...]
    zab = ab + small_ref[1:2, :]
    softplus = jnp.maximum(zab, 0.0) + jnp.log(1.0 + jnp.exp(-jnp.abs(zab)))
    gb_ref[...] = jnp.where(_iota(ab.shape, 1) < GDN_HEADS,
                            -jnp.exp(small_ref[0:1, :]) * softplus, jax.nn.sigmoid(ab))

    r = _iota((n, n), 0)
    c = _iota((n, n), 1)
    causal = r >= c
    strict = r > c
    lower_ones = jnp.where(causal, 1.0, 0.0).astype(BF16)

    def sub_blocks(ls):
        return ((r >> (ls + 1)) == (c >> (ls + 1))) & ((r >> ls) != (c >> ls))

    def phase_a(pi, carry):
        probs = []
        for cc in range(2):
            ci = pi * 2 + cc
            r0 = pl.multiple_of(ci * n, n)
            gbeta = gb_ref[pl.ds(r0, n), :]
            gc_all = sum(_dot(lower_ones, part) for part in _split3(gbeta))
            gc_t = gc_all.T
            for h in range(GDN_HEADS):
                hs = slice(h * GDN_HEAD_DIM, (h + 1) * GDN_HEAD_DIM)
                slot = ci * GDN_HEADS + h
                q = yq_ref[pl.ds(r0, n), hs]
                k = yk_ref[pl.ds(r0, n), hs]
                v = yv_ref[pl.ds(r0, n), hs]
                gc = jnp.broadcast_to(gc_all[:, h:h + 1], (n, n))
                gc_row = jnp.broadcast_to(gc_t[h:h + 1, :], (n, n))
                g_tot = jnp.broadcast_to(gc[n - 1:n, :], (n, n))
                beta_b = jnp.broadcast_to(gbeta[:, GDN_HEADS + h:GDN_HEADS + h + 1], (n, n))
                decay = jnp.exp(jnp.where(causal, gc - gc_row, -jnp.inf))
                k_beta = k * beta_b
                k16 = k.astype(BF16)
                lower = jnp.where(strict, _dot_nt(k_beta.astype(BF16), k16) * decay, 0.0)
                a_ref[slot] = jnp.where(causal, _dot_nt(q.astype(BF16), k16) * decay, 0.0).astype(BF16)
                e_gc = jnp.exp(gc)
                qd_ref[slot] = (q * e_gc).astype(BF16)
                kdt_ref[slot] = (k * jnp.exp(g_tot - gc)).T.astype(BF16)
                gl_ref[slot] = jnp.exp(g_tot[0:8, :])
                rhs = jnp.concatenate([v * beta_b, k_beta * e_gc], axis=1)
                probs.append((slot, lower, rhs))
        es = [-jnp.where((r >> 1) == (c >> 1), lower, 0.0) for _, lower, _ in probs]
        for ls in range(1, n.bit_length() - 1):
            ms = [jnp.where(sub_blocks(ls), lower, 0.0) for _, lower, _ in probs]
            ps = [_dot(e.astype(BF16), m.astype(BF16)) for e, m in zip(es, ms)]
            es = [e - m - (p + _dot((m + p).astype(BF16), e.astype(BF16))) for e, m, p in zip(es, ms, ps)]
        for (slot, _, rhs), e in zip(probs, es):
            uw = rhs + _dot(e.astype(BF16), rhs.astype(BF16))
            u_ref[slot] = uw[:, :n]
            w_ref[slot] = uw[:, n:].astype(BF16)
        return carry

    lax.fori_loop(0, nchunks // 2, phase_a, 0)

    gain = norm_ref[...]

    def phase_b(ci, carry):
        r0 = pl.multiple_of(ci * n, n)
        for h in range(GDN_HEADS):
            hs = slice(h * GDN_HEAD_DIM, (h + 1) * GDN_HEAD_DIM)
            slot = ci * GDN_HEADS + h
            state = state_ref[h]
            s16 = state.astype(BF16)
            v_new = u_ref[slot] - _dot(w_ref[slot], s16)
            vn16 = v_new.astype(BF16)
            o = _dot(qd_ref[slot], s16) + _dot(a_ref[slot], vn16)
            state_ref[h] = state * gl_ref[slot][0:1, :] + _dot(kdt_ref[slot], vn16)
            zg = z_ref[pl.ds(r0, n), hs]
            o_ref[pl.ds(r0, n), hs] = (_rms(o, gain) * (zg * jax.nn.sigmoid(zg))).astype(o_ref.dtype)
        return carry

    lax.fori_loop(0, nchunks, phase_b, 0)


def _gdn(pfp, conv_w, small, norm, batch, seq, rows):
    steps = seq // rows
    w = GDN_W
    cq = FP_CQKV // w
    slots = (rows // GDN_CHUNK) * GDN_HEADS
    tile = (slots, GDN_CHUNK, GDN_CHUNK)

    def blk(col):
        return pl.BlockSpec((rows, w), lambda b, i: (b * steps + i, col))

    return pl.pallas_call(
        functools.partial(_gdn_kernel, rows=rows),
        grid=(batch, steps),
        in_specs=[
            blk(cq), blk(cq + 1), blk(cq + 2), blk(FP_CZ // w),
            pl.BlockSpec((rows, LANES), lambda b, i: (b * steps + i, FP_AB // LANES)),
            pl.BlockSpec((GDN_CONV, 3 * w), lambda b, i: (0, 0)),
            pl.BlockSpec((8, LANES), lambda b, i: (0, 0)),
            pl.BlockSpec((1, GDN_HEAD_DIM), lambda b, i: (0, 0)),
        ],
        out_specs=pl.BlockSpec((rows, w), lambda b, i: (b * steps + i, 0)),
        out_shape=jax.ShapeDtypeStruct((batch * seq, w), BF16),
        scratch_shapes=[
            pltpu.VMEM((8, 3 * w), F32),
            pltpu.VMEM((GDN_HEADS, GDN_HEAD_DIM, GDN_HEAD_DIM), F32),
            pltpu.VMEM((rows, w), F32), pltpu.VMEM((rows, w), F32), pltpu.VMEM((rows, w), F32),
            pltpu.VMEM((rows, LANES), F32),
            pltpu.VMEM(tile, F32), pltpu.VMEM(tile, BF16), pltpu.VMEM(tile, BF16),
            pltpu.VMEM(tile, BF16), pltpu.VMEM(tile, BF16), pltpu.VMEM((slots, 8, LANES), F32),
        ],
        compiler_params=pltpu.CompilerParams(
            dimension_semantics=("arbitrary", "arbitrary"), vmem_limit_bytes=VMEM_LIMIT),
        name="gdn",
    )(pfp, pfp, pfp, pfp, pfp, conv_w, small, norm)


def _merge_kernel(ya_ref, yb_ref, yc_ref, ga_ref, gb_ref, gc_ref, x_ref,
                  wa_ref, wb_ref, wc_ref, wo_ref, gain_ref, o_ref):
    merged = (jax.nn.sigmoid(ga_ref[...]) * _dot(ya_ref[...], wa_ref[...])
              + jax.nn.sigmoid(gb_ref[...]) * _dot(yb_ref[...], wb_ref[...])
              + jax.nn.sigmoid(gc_ref[...]) * _dot(yc_ref[...], wc_ref[...]))
    r = _dot(merged.astype(BF16), wo_ref[...])
    o_ref[...] = x_ref[...] + _rms(r, gain_ref[...])


def _merge(ya, yb, yc, pfp, x, wa, wb, wc, wo, gain, tm):
    t, d = x.shape

    def rows(width, col=0):
        return pl.BlockSpec((tm, width), lambda i: (i, col))

    def whole(a):
        return pl.BlockSpec(a.shape, lambda i: (0, 0))

    return pl.pallas_call(
        _merge_kernel,
        grid=(t // tm,),
        in_specs=[rows(SB_W), rows(SW_QW), rows(GDN_W),
                  rows(d, FP_GA // d), rows(d, FP_GB // d), rows(d, FP_GC // d), rows(d),
                  whole(wa), whole(wb), whole(wc), whole(wo), whole(gain)],
        out_specs=rows(d),
        out_shape=jax.ShapeDtypeStruct((t, d), F32),
        compiler_params=pltpu.CompilerParams(
            dimension_semantics=("arbitrary",), vmem_limit_bytes=VMEM_LIMIT),
        name="merge",
    )(ya, yb, yc, pfp, pfp, pfp, x, wa, wb, wc, wo, gain)


FFN_CK = 256


def _ffn_kernel(x_ref, gpre_ref, wup_ref, conv_ref, wdn_ref, gpost_ref, o_ref, carry_ref, *, tm):
    @pl.when(pl.program_id(1) == 0)
    def _():
        carry_ref[...] = jnp.zeros_like(carry_ref)

    x = x_ref[...]
    hn = _rms(x, gpre_ref[...]).astype(BF16)

    def conv(col):
        cs = slice(col, col + FFN_CK)
        hid = _dot(hn, wup_ref[:, cs])
        prev8 = carry_ref[:, cs]
        w = conv_ref[:, cs]
        y = w[FFN_CONV - 1:FFN_CONV, :] * hid
        for s in range(1, FFN_CONV):
            y = y + w[FFN_CONV - 1 - s:FFN_CONV - s, :] * _shift_rows(hid, prev8, s)
        carry_ref[:, cs] = hid[tm - 8:tm, :]
        return y

    acc = jnp.zeros((tm, D_MODEL), F32)
    for ci in range(D_FF // FFN_CK):
        f_gate = conv(ci * FFN_CK)
        f_up = conv(D_FF + ci * FFN_CK)
        f = jax.nn.gelu(f_gate, approximate=True) * f_up
        acc = acc + _dot(f.astype(BF16), wdn_ref[ci * FFN_CK:(ci + 1) * FFN_CK, :])
    o_ref[...] = x + _rms(acc, gpost_ref[...])


def _ffn(x, gpre, wup, conv_w, wdn, gpost, batch, seq, tm):
    steps = seq // tm
    d = x.shape[1]

    def whole(a):
        return pl.BlockSpec(a.shape, lambda b, i: (0, 0))

    return pl.pallas_call(
        functools.partial(_ffn_kernel, tm=tm),
        grid=(batch, steps),
        in_specs=[pl.BlockSpec((tm, d), lambda b, i: (b * steps + i, 0)),
                  whole(gpre), whole(wup), whole(conv_w), whole(wdn), whole(gpost)],
        out_specs=pl.BlockSpec((tm, d), lambda b, i: (b * steps + i, 0)),
        out_shape=jax.ShapeDtypeStruct(x.shape, F32),
        scratch_shapes=[pltpu.VMEM((8, 2 * D_FF), F32)],
        compiler_params=pltpu.CompilerParams(
            dimension_semantics=("arbitrary", "arbitrary"), vmem_limit_bytes=VMEM_LIMIT),
        name="ffn",
    )(x, gpre, wup, conv_w, wdn, gpost)


def _pack_w_in(w):
    o = 0
    parts = {}
    for name, width in (("aq", SB_W), ("ak", SB_W), ("av", SB_W), ("bq", SW_QW), ("bk", SW_KVW),
                        ("bv", SW_KVW), ("cqkv", 3 * GDN_W), ("cz", GDN_W), ("ca", GDN_HEADS),
                        ("cb", GDN_HEADS), ("ga", D_MODEL), ("gb", D_MODEL), ("gc", D_MODEL)):
        parts[name] = w[:, o:o + width]
        o += width

    def dup(t):
        return jnp.concatenate([t[:, h * HALF:(h + 1) * HALF] for h in range(SW_KV_HEADS) for _ in range(2)], axis=1)

    pad = jnp.zeros((w.shape[0], FP_COLS - FP_AB - 2 * GDN_HEADS), w.dtype)
    cols = [parts["aq"] * (SB_HEAD_DIM ** -0.5), parts["ak"], parts["av"],
            parts["bq"] * (SW_HEAD_DIM ** -0.5), dup(parts["bk"]), dup(parts["bv"]),
            parts["ga"], parts["gb"], parts["gc"], parts["cqkv"], parts["cz"],
            parts["ca"], parts["cb"], pad]
    return jnp.concatenate(cols, axis=1).astype(BF16)


def _layer(x, batch, seq, p):
    pbf, pfp = _inproj(x, p["ln_mix_pre"], p["w_in"], tm=1024)
    ya = _sb_attention(pbf, batch, seq)
    yb = _sw_attention(pbf, p["sinks"], p["slopes"], batch, seq)
    yc = _gdn(pfp, p["gdn_conv"], p["gdn_small"], p["gdn_norm"], batch, seq, rows=512)
    x = _merge(ya, yb, yc, pfp, x, p["wa"], p["wb"], p["wc"], p["wo"], p["ln_mix_post"], tm=512)
    return _ffn(x, p["ln_ffn_pre"], p["w_up"], p["ffn_conv"], p["w_down"], p["ln_ffn_post"],
                batch, seq, tm=512)


def kernel(x, ln_mix_pre, w_in, sw_sinks, gdn_conv, gdn_a_log, gdn_dt_bias, gdn_norm, w_branch_a,
           w_branch_b, w_branch_c, w_out, ln_mix_post, ln_ffn_pre, w_up, ffn_conv, w_down, ln_ffn_post):
    batch, seq, d = x.shape
    depth = w_in.shape[0]
    slopes = jnp.exp2(-8.0 * jnp.arange(1, SW_Q_HEADS + 1, dtype=F32) / SW_Q_HEADS)
    h = x.reshape(batch * seq, d)
    for l in range(depth):
        small = jnp.zeros((8, LANES), F32)
        small = small.at[0, :GDN_HEADS].set(gdn_a_log[l]).at[1, :GDN_HEADS].set(gdn_dt_bias[l])
        p = {
            "ln_mix_pre": ln_mix_pre[l][None, :], "w_in": _pack_w_in(w_in[l]),
            "sinks": sw_sinks[l], "slopes": slopes,
            "gdn_conv": gdn_conv[l], "gdn_small": small, "gdn_norm": gdn_norm[l][None, :],
            "wa": w_branch_a[l].astype(BF16), "wb": w_branch_b[l].astype(BF16),
            "wc": w_branch_c[l].astype(BF16), "wo": w_out[l].astype(BF16),
            "ln_mix_post": ln_mix_post[l][None, :], "ln_ffn_pre": ln_ffn_pre[l][None, :],
            "w_up": w_up[l].astype(BF16), "ffn_conv": ffn_conv[l], "w_down": w_down[l].astype(BF16),
            "ln_ffn_post": ln_ffn_post[l][None, :],
        }
        h = _layer(h, batch, seq, p)
    return h.reshape(batch, seq, d)
```

```python
import functools

import jax
import jax.numpy as jnp
from jax import lax
from jax.experimental import pallas as pl
from jax.experimental.pallas import tpu as pltpu

F32 = jnp.float32
BF16 = jnp.bfloat16
NORM_EPS = 1e-6

D_MODEL = 1024
SB_HEADS, SB_HEAD_DIM = 8, 64
SW_Q_HEADS, SW_KV_HEADS, SW_HEAD_DIM = 8, 2, 64
GDN_HEADS, GDN_HEAD_DIM, GDN_CONV = 4, 128, 4
D_FF, FFN_CONV = 2816, 3
SB_W = SB_HEADS * SB_HEAD_DIM
SW_QW = SW_Q_HEADS * SW_HEAD_DIM
SW_KVW = SW_KV_HEADS * SW_HEAD_DIM
GDN_W = GDN_HEADS * GDN_HEAD_DIM

BLK = 128
LANES = 128
HALF = 64
VMEM_LIMIT = 56 * 1024 * 1024

BF_SBQ, BF_SBK, BF_SBV, BF_SWQ, BF_SWK, BF_SWV, BF_COLS = 0, 512, 1024, 1536, 2048, 2304, 2560
FP_GA, FP_GB, FP_GC, FP_CQKV, FP_CZ, FP_COLS = 0, 1024, 2048, 3072, 4608, 5120
PROJ_TN = 512
F32_EXP_ZERO = -104.0


def _dot(a, b):
    return jnp.dot(a, b, preferred_element_type=F32)


def _dot_nt(a, b):
    return lax.dot_general(a, b, (((1,), (1,)), ((), ())), preferred_element_type=F32)


def _iota(shape, dim):
    return lax.broadcasted_iota(jnp.int32, shape, dim)


def _rms(t, gain):
    return t * lax.rsqrt(jnp.mean(t * t, axis=-1, keepdims=True) + NORM_EPS) * gain


def _shift_rows(h, prev8, s):
    r = pltpu.roll(h, s, axis=0)
    row = _iota(h.shape, 0)
    for t in range(s):
        r = jnp.where(row == t, prev8[8 - s + t:8 - s + t + 1, :], r)
    return r


def _inproj_kernel(x_ref, g_ref, w_ref, wab_ref, obf_ref, of_ref, oab_ref, xn_ref, *, nbf):
    j = pl.program_id(1)

    @pl.when(j == 0)
    def _():
        xn = _rms(x_ref[...], g_ref[...]).astype(BF16)
        xn_ref[...] = xn
        oab_ref[...] = _dot(xn, wab_ref[...])

    r = _dot(xn_ref[...], w_ref[...])

    @pl.when(j < nbf)
    def _():
        obf_ref[...] = r.astype(BF16)

    @pl.when(j >= nbf)
    def _():
        of_ref[...] = r


def _inproj(x, gain, w, wab, tm):
    t, d = x.shape
    tn = PROJ_TN
    nbf, nfp = BF_COLS // tn, FP_COLS // tn
    return pl.pallas_call(
        functools.partial(_inproj_kernel, nbf=nbf),
        grid=(t // tm, nbf + nfp),
        in_specs=[
            pl.BlockSpec((tm, d), lambda i, j: (i, 0)),
            pl.BlockSpec((1, d), lambda i, j: (0, 0)),
            pl.BlockSpec((d, tn), lambda i, j: (0, j)),
            pl.BlockSpec((d, LANES), lambda i, j: (0, 0)),
        ],
        out_specs=[
            pl.BlockSpec((tm, tn), lambda i, j: (i, jnp.minimum(j, nbf - 1))),
            pl.BlockSpec((tm, tn), lambda i, j: (i, jnp.maximum(j - nbf, 0))),
            pl.BlockSpec((tm, LANES), lambda i, j: (i, 0)),
        ],
        out_shape=[jax.ShapeDtypeStruct((t, BF_COLS), BF16), jax.ShapeDtypeStruct((t, FP_COLS), F32),
                   jax.ShapeDtypeStruct((t, LANES), F32)],
        scratch_shapes=[pltpu.VMEM((tm, d), BF16)],
        compiler_params=pltpu.CompilerParams(
            dimension_semantics=("arbitrary", "arbitrary"), vmem_limit_bytes=VMEM_LIMIT),
        name="inproj",
    )(x, gain, w, wab)


SB_TQ = 2 * BLK
SB_PAIRS = 2


def _sb_kernel(q_ref, k_ref, v_ref, o_ref, acc_ref, lr_ref):
    ti = pl.program_id(2)
    lane = _iota((BLK, LANES), 1)
    row = _iota((BLK, LANES), 0)
    first = lane < HALF
    tri = lane < row
    full = lane >= 0
    zero_tile = jnp.zeros((BLK, LANES), F32)
    r2 = _iota((BLK, 2 * LANES), 0)
    c2 = _iota((BLK, 2 * LANES), 1)
    suffix_ones = jnp.where((r2 > c2) | (c2 >= LANES), 1.0, 0.0).astype(BF16)
    suffix_ones = jnp.concatenate([suffix_ones, suffix_ones], axis=0)

    def split_heads(q):
        sel = jnp.concatenate([first] * (q.shape[0] // BLK), axis=0)
        zero = jnp.zeros_like(q)
        return jnp.where(sel, q, zero), jnp.where(sel, zero, q)

    def log_parts(z):
        log_stay = -(jnp.maximum(z, 0.0) + jnp.log(1.0 + jnp.exp(-jnp.abs(z))))
        return log_stay, log_stay + z

    def suffix_sums(parts):
        hi = [x.astype(BF16) for x in parts]
        lo = [(x - h.astype(F32)).astype(BF16) for x, h in zip(parts, hi)]
        hilo = jnp.concatenate([jnp.concatenate(hi, axis=0), jnp.concatenate(lo, axis=0)], axis=1)
        sums = _dot(hilo, suffix_ones)
        out, o = [], 0
        for x in parts:
            out.append((sums[o:o + x.shape[0], :LANES], sums[o:o + x.shape[0], LANES:]))
            o += x.shape[0]
        return out

    def generic_block(pp, s, j, valid):
        ps = slice(pp * LANES, (pp + 1) * LANES)
        q0, q1 = split_heads(q_ref[s * BLK:(s + 1) * BLK, ps])
        off = pl.multiple_of(j * BLK, BLK)
        k, v = k_ref[pl.ds(off, BLK), ps], v_ref[pl.ds(off, BLK), ps]
        log_stay, log_take = log_parts(_dot_nt(jnp.concatenate([q0, q1], axis=0), k))
        if valid is not None:
            vm = jnp.concatenate([valid, valid], axis=0)
            log_stay = jnp.where(vm, log_stay, 0.0)
        (later, total), = suffix_sums([log_stay])
        w = jnp.exp(log_take + later + lr_ref[pp, s])
        if valid is not None:
            w = jnp.where(vm, w, 0.0)
        pv = _dot(w.astype(BF16), v)
        acc_ref[pp, s] += jnp.where(first, pv[:BLK], pv[BLK:])
        lr_ref[pp, s] += total

    def tail(pp, s, j0):
        def cond(c):
            j, live = c
            return jnp.logical_and(j >= 0, live)

        def body(c):
            j, _ = c
            generic_block(pp, s, j, None)
            return j - 1, jnp.max(lr_ref[pp, s]) > F32_EXP_ZERO

        lax.while_loop(cond, body, (j0, jnp.max(lr_ref[pp, s]) > F32_EXP_ZERO))

    @pl.when(ti == 0)
    def _():
        for pp in range(SB_PAIRS):
            for s in range(2):
                acc_ref[pp, s] = jnp.zeros((BLK, LANES), F32)
                lr_ref[pp, s] = jnp.zeros((2 * BLK, LANES), F32)
                generic_block(pp, s, 2 * ti + s, tri)
                tail(pp, s, 2 * ti + s - 1)

    @pl.when(ti > 0)
    def _():
        base = pl.multiple_of((2 * ti - 2) * BLK, BLK)
        m2 = jnp.concatenate([tri, full, tri, full], axis=0)
        m3 = jnp.concatenate([tri, tri], axis=0)
        for pp in range(SB_PAIRS):
            ps = slice(pp * LANES, (pp + 1) * LANES)
            q0, q1 = split_heads(q_ref[:, ps])
            kw = k_ref[pl.ds(base, 4 * BLK), ps]
            vw = v_ref[pl.ds(base, 4 * BLK), ps]
            kb = [kw[t * BLK:(t + 1) * BLK] for t in range(4)]
            l_ab = jnp.concatenate([q0, q1], axis=0)
            l_a = jnp.concatenate([q0[:BLK], q1[:BLK]], axis=0)
            l_b = jnp.concatenate([q0[BLK:], q1[BLK:]], axis=0)
            ls0, lt0 = log_parts(_dot_nt(l_a, kb[0]))
            ls1, lt1 = log_parts(_dot_nt(l_ab, kb[1]))
            ls2, lt2 = log_parts(_dot_nt(l_ab, kb[2]))
            ls3, lt3 = log_parts(_dot_nt(l_b, kb[3]))
            ls2 = jnp.where(m2, ls2, 0.0)
            ls3 = jnp.where(m3, ls3, 0.0)
            (lat0, tot0), (lat1, tot1), (lat2, tot2), (lat3, tot3) = suffix_sums([ls0, ls1, ls2, ls3])
            lrb2 = jnp.concatenate([zero_tile, tot3[:BLK], zero_tile, tot3[BLK:]], axis=0)
            lrb1 = lrb2 + tot2
            lre1 = lrb1 + tot1
            lrb0 = jnp.concatenate([lre1[:BLK], lre1[2 * BLK:3 * BLK]], axis=0)
            w3 = jnp.where(m3, jnp.exp(lt3 + lat3), 0.0)
            w2 = jnp.where(m2, jnp.exp(lt2 + lat2 + lrb2), 0.0)
            w1 = jnp.exp(lt1 + lat1 + lrb1)
            w0 = jnp.exp(lt0 + lat0 + lrb0)
            w12 = jnp.concatenate([w1.astype(BF16), w2.astype(BF16)], axis=1)
            pv12 = _dot(w12, vw[BLK:3 * BLK])
            pv0 = _dot(w0.astype(BF16), vw[:BLK])
            pv3 = _dot(w3.astype(BF16), vw[3 * BLK:])
            acc_ref[pp, 0] = jnp.where(first, pv12[:BLK] + pv0[:BLK], pv12[2 * BLK:3 * BLK] + pv0[BLK:])
            acc_ref[pp, 1] = jnp.where(first, pv12[BLK:2 * BLK] + pv3[:BLK], pv12[3 * BLK:] + pv3[BLK:])
            lr_ref[pp, 0] = lrb0 + tot0
            lr_ref[pp, 1] = jnp.concatenate([lre1[BLK:2 * BLK], lre1[3 * BLK:]], axis=0)

        @pl.when(jnp.max(lr_ref[...]) > F32_EXP_ZERO)
        def _():
            for pp in range(SB_PAIRS):
                tail(pp, 0, 2 * ti - 3)
                tail(pp, 1, 2 * ti - 2)

    for pp in range(SB_PAIRS):
        for s in range(2):
            o_ref[s * BLK:(s + 1) * BLK, pp * LANES:(pp + 1) * LANES] = acc_ref[pp, s].astype(o_ref.dtype)


def _sb_attention(pbf, batch, seq):
    nt = seq // SB_TQ
    w = SB_PAIRS * LANES
    cq, ck, cv = BF_SBQ // w, BF_SBK // w, BF_SBV // w
    return pl.pallas_call(
        _sb_kernel,
        grid=(batch, SB_W // w, nt),
        in_specs=[
            pl.BlockSpec((SB_TQ, w), lambda b, p, i: (b * nt + i, cq + p)),
            pl.BlockSpec((seq, w), lambda b, p, i: (b, ck + p)),
            pl.BlockSpec((seq, w), lambda b, p, i: (b, cv + p)),
        ],
        out_specs=pl.BlockSpec((SB_TQ, w), lambda b, p, i: (b * nt + i, p)),
        out_shape=jax.ShapeDtypeStruct((batch * seq, SB_W), BF16),
        scratch_shapes=[pltpu.VMEM((SB_PAIRS, 2, BLK, LANES), F32),
                        pltpu.VMEM((SB_PAIRS, 2, 2 * BLK, LANES), F32)],
        compiler_params=pltpu.CompilerParams(
            dimension_semantics=("arbitrary", "arbitrary", "arbitrary"), vmem_limit_bytes=VMEM_LIMIT),
        name="sb_attention",
    )(pbf, pbf, pbf)


SW_TQ = 2 * BLK


def _sw_kernel(sinks_ref, slopes_ref, q_ref, kp_ref, kc_ref, vp_ref, vc_ref, o_ref):
    ti = pl.program_id(1)
    lane = _iota((BLK, LANES), 1)
    row = _iota((BLK, LANES), 0)
    first = lane < HALF
    cur = lane <= row
    dist = jnp.where(cur, row - lane, row - lane + BLK).astype(F32)
    group = SW_Q_HEADS // SW_KV_HEADS
    for s in range(SW_TQ // BLK):
        rs = slice(s * BLK, (s + 1) * BLK)
        for hk in range(SW_KV_HEADS):
            ks = slice(hk * LANES, (hk + 1) * LANES)
            kc, vc = kc_ref[rs, ks], vc_ref[rs, ks]
            if s == 0:
                kp, vp = kp_ref[:, ks], vp_ref[:, ks]
            else:
                kp, vp = kc_ref[(s - 1) * BLK:s * BLK, ks], vc_ref[(s - 1) * BLK:s * BLK, ks]
            pairs = range(hk * group // 2, (hk + 1) * group // 2)
            lhs = []
            for p in pairs:
                qp = q_ref[rs, p * LANES:(p + 1) * LANES]
                zero = jnp.zeros_like(qp)
                lhs += [jnp.where(first, qp, zero), jnp.where(first, zero, qp)]
            lhs = jnp.concatenate(lhs, axis=0)
            s_cur = _dot_nt(lhs, kc)
            s_prev = _dot_nt(lhs, kp)
            prob_c, prob_p = [], []
            for g in range(group):
                head = hk * group + g
                sink = sinks_ref[head]
                gs = slice(g * BLK, (g + 1) * BLK)
                sc = jnp.where(cur, s_cur[gs], s_prev[gs]) - slopes_ref[head] * dist
                if s == 0:
                    sc = jnp.where(jnp.logical_or(cur, ti > 0), sc, -jnp.inf)
                m = jnp.maximum(jnp.max(sc, axis=-1, keepdims=True), sink)
                e = jnp.exp(sc - m)
                denom = jnp.sum(e, axis=-1, keepdims=True) + jnp.exp(sink - m)
                prob = (e / denom).astype(BF16)
                pz = jnp.zeros_like(prob)
                prob_c.append(jnp.where(cur, prob, pz))
                prob_p.append(jnp.where(cur, pz, prob))
            pv = _dot(jnp.concatenate(prob_c, axis=0), vc) + _dot(jnp.concatenate(prob_p, axis=0), vp)
            for t, p in enumerate(pairs):
                o_ref[rs, p * LANES:(p + 1) * LANES] = jnp.where(
                    first, pv[2 * t * BLK:(2 * t + 1) * BLK], pv[(2 * t + 1) * BLK:(2 * t + 2) * BLK]
                ).astype(o_ref.dtype)


def _sw_attention(pbf, sinks, slopes, batch, seq):
    nt = seq // SW_TQ
    kvw = 2 * SW_KVW
    cq, ck, cv = BF_SWQ // SW_QW, BF_SWK // kvw, BF_SWV // kvw
    per = SW_TQ // BLK

    def prev(b, i):
        return b * nt * per + jnp.maximum(i * per - 1, 0)

    smem = pl.BlockSpec(memory_space=pltpu.SMEM)
    return pl.pallas_call(
        _sw_kernel,
        grid=(batch, nt),
        in_specs=[
            smem, smem,
            pl.BlockSpec((SW_TQ, SW_QW), lambda b, i: (b * nt + i, cq)),
            pl.BlockSpec((BLK, kvw), lambda b, i: (prev(b, i), ck)),
            pl.BlockSpec((SW_TQ, kvw), lambda b, i: (b * nt + i, ck)),
            pl.BlockSpec((BLK, kvw), lambda b, i: (prev(b, i), cv)),
            pl.BlockSpec((SW_TQ, kvw), lambda b, i: (b * nt + i, cv)),
        ],
        out_specs=pl.BlockSpec((SW_TQ, SW_QW), lambda b, i: (b * nt + i, 0)),
        out_shape=jax.ShapeDtypeStruct((batch * seq, SW_QW), BF16),
        compiler_params=pltpu.CompilerParams(
            dimension_semantics=("arbitrary", "arbitrary"), vmem_limit_bytes=VMEM_LIMIT),
        name="sw_attention",
    )(sinks, slopes, pbf, pbf, pbf, pbf, pbf)


GDN_CHUNK = 128


def _split3(x):
    x1 = x.astype(BF16)
    r1 = x - x1.astype(F32)
    x2 = r1.astype(BF16)
    x3 = (r1 - x2.astype(F32)).astype(BF16)
    return x1, x2, x3


def _gdn_kernel(q_ref, k_ref, v_ref, z_ref, ab_ref, conv_ref, small_ref, norm_ref, o_ref,
                carry_ref, state_ref, yq_ref, yk_ref, yv_ref, *, rows):
    n = GDN_CHUNK
    nchunks = rows // n
    step = pl.program_id(1)

    @pl.when(step == 0)
    def _():
        carry_ref[...] = jnp.zeros_like(carry_ref)
        state_ref[...] = jnp.zeros_like(state_ref)

    for idx, (src, dst) in enumerate(((q_ref, yq_ref), (k_ref, yk_ref), (v_ref, yv_ref))):
        cols = slice(idx * GDN_W, (idx + 1) * GDN_W)
        w = conv_ref[:, cols]
        for ci in range(nchunks):
            x = src[ci * n:(ci + 1) * n, :]
            prev8 = carry_ref[:, cols] if ci == 0 else src[ci * n - 8:ci * n, :]
            y = w[GDN_CONV - 1:GDN_CONV, :] * x
            for s in range(1, GDN_CONV):
                y = y + w[GDN_CONV - 1 - s:GDN_CONV - s, :] * _shift_rows(x, prev8, s)
            y = y * jax.nn.sigmoid(y)
            for h in range(GDN_HEADS):
                yh = y[:, h * GDN_HEAD_DIM:(h + 1) * GDN_HEAD_DIM]
                if idx < 2:
                    inv = lax.rsqrt(jnp.sum(yh * yh, axis=-1, keepdims=True) + NORM_EPS)
                    yh = yh * (inv * (GDN_HEAD_DIM ** -0.5) if idx == 0 else inv)
                dst[ci * n:(ci + 1) * n, h * GDN_HEAD_DIM:(h + 1) * GDN_HEAD_DIM] = yh
        carry_ref[:, cols] = src[rows - 8:rows, :]

    r = _iota((n, n), 0)
    c = _iota((n, n), 1)
    causal = r >= c
    strict = r > c
    lower_ones = jnp.where(causal, 1.0, 0.0).astype(BF16)
    head_lane = _iota((n, LANES), 1) < GDN_HEADS

    def sub_blocks(ls):
        return ((r >> (ls + 1)) == (c >> (ls + 1))) & ((r >> ls) != (c >> ls))

    probs = []
    for ci in range(nchunks):
        rs = slice(ci * n, (ci + 1) * n)
        ab = ab_ref[rs, :]
        zab = ab + small_ref[1:2, :]
        softplus = jnp.maximum(zab, 0.0) + jnp.log(1.0 + jnp.exp(-jnp.abs(zab)))
        gbeta = jnp.where(head_lane, -jnp.exp(small_ref[0:1, :]) * softplus, jax.nn.sigmoid(ab))
        gc_all = sum(_dot(lower_ones, part) for part in _split3(gbeta))
        gc_t = gc_all.T
        for h in range(GDN_HEADS):
            hs = slice(h * GDN_HEAD_DIM, (h + 1) * GDN_HEAD_DIM)
            q, k, v = yq_ref[rs, hs], yk_ref[rs, hs], yv_ref[rs, hs]
            gc = jnp.broadcast_to(gc_all[:, h:h + 1], (n, n))
            gc_row = jnp.broadcast_to(gc_t[h:h + 1, :], (n, n))
            g_tot = jnp.broadcast_to(gc[n - 1:n, :], (n, n))
            beta_b = jnp.broadcast_to(gbeta[:, GDN_HEADS + h:GDN_HEADS + h + 1], (n, n))
            decay = jnp.exp(jnp.where(causal, gc - gc_row, -jnp.inf))
            k_beta = k * beta_b
            k16 = k.astype(BF16)
            e_gc = jnp.exp(gc)
            probs.append(dict(
                rs=rs, hs=hs, h=h,
                lower=jnp.where(strict, _dot_nt(k_beta.astype(BF16), k16) * decay, 0.0),
                a=jnp.where(causal, _dot_nt(q.astype(BF16), k16) * decay, 0.0).astype(BF16),
                qd=(q * e_gc).astype(BF16),
                kdt=(k * jnp.exp(g_tot - gc)).T.astype(BF16),
                gl=jnp.exp(g_tot),
                rhs=jnp.concatenate([v * beta_b, k_beta * e_gc], axis=1)))
    es = [-jnp.where((r >> 1) == (c >> 1), p["lower"], 0.0) for p in probs]
    for ls in range(1, n.bit_length() - 1):
        ms = [jnp.where(sub_blocks(ls), p["lower"], 0.0) for p in probs]
        ps = [_dot(e.astype(BF16), m.astype(BF16)) for e, m in zip(es, ms)]
        es = [e - m - (pp + _dot((m + pp).astype(BF16), e.astype(BF16))) for e, m, pp in zip(es, ms, ps)]
    for p, e in zip(probs, es):
        uw = p["rhs"] + _dot(e.astype(BF16), p["rhs"].astype(BF16))
        p["u"] = uw[:, :n]
        p["wq"] = jnp.concatenate([uw[:, n:].astype(BF16), p["qd"]], axis=0)
        p["ak"] = jnp.concatenate([p["a"], p["kdt"]], axis=0)

    gain = norm_ref[...]
    states = [state_ref[h] for h in range(GDN_HEADS)]
    for p in probs:
        h = p["h"]
        sr = _dot(p["wq"], states[h].astype(BF16))
        vn16 = (p["u"] - sr[:n]).astype(BF16)
        vr = _dot(p["ak"], vn16)
        o = sr[n:] + vr[:n]
        states[h] = states[h] * p["gl"] + vr[n:]
        zg = z_ref[p["rs"], p["hs"]]
        o_ref[p["rs"], p["hs"]] = (_rms(o, gain) * (zg * jax.nn.sigmoid(zg))).astype(o_ref.dtype)
    for h in range(GDN_HEADS):
        state_ref[h] = states[h]


def _gdn(pfp, pab, conv_w, small, norm, batch, seq, rows):
    steps = seq // rows
    w = GDN_W
    cq = FP_CQKV // w

    def blk(col):
        return pl.BlockSpec((rows, w), lambda b, i: (b * steps + i, col))

    return pl.pallas_call(
        functools.partial(_gdn_kernel, rows=rows),
        grid=(batch, steps),
        in_specs=[
            blk(cq), blk(cq + 1), blk(cq + 2), blk(FP_CZ // w),
            pl.BlockSpec((rows, LANES), lambda b, i: (b * steps + i, 0)),
            pl.BlockSpec((GDN_CONV, 3 * w), lambda b, i: (0, 0)),
            pl.BlockSpec((8, LANES), lambda b, i: (0, 0)),
            pl.BlockSpec((1, GDN_HEAD_DIM), lambda b, i: (0, 0)),
        ],
        out_specs=pl.BlockSpec((rows, w), lambda b, i: (b * steps + i, 0)),
        out_shape=jax.ShapeDtypeStruct((batch * seq, w), BF16),
        scratch_shapes=[
            pltpu.VMEM((8, 3 * w), F32),
            pltpu.VMEM((GDN_HEADS, GDN_HEAD_DIM, GDN_HEAD_DIM), F32),
            pltpu.VMEM((rows, w), F32), pltpu.VMEM((rows, w), F32), pltpu.VMEM((rows, w), F32),
        ],
        compiler_params=pltpu.CompilerParams(
            dimension_semantics=("arbitrary", "arbitrary"), vmem_limit_bytes=VMEM_LIMIT),
        name="gdn",
    )(pfp, pfp, pfp, pfp, pab, conv_w, small, norm)


def _merge_kernel(ya_ref, yb_ref, yc_ref, ga_ref, gb_ref, gc_ref, x_ref,
                  wa_ref, wb_ref, wc_ref, wo_ref, gain_ref, o_ref):
    merged = (jax.nn.sigmoid(ga_ref[...]) * _dot(ya_ref[...], wa_ref[...])
              + jax.nn.sigmoid(gb_ref[...]) * _dot(yb_ref[...], wb_ref[...])
              + jax.nn.sigmoid(gc_ref[...]) * _dot(yc_ref[...], wc_ref[...]))
    r = _dot(merged.astype(BF16), wo_ref[...])
    o_ref[...] = x_ref[...] + _rms(r, gain_ref[...])


def _merge(ya, yb, yc, pfp, x, wa, wb, wc, wo, gain, tm):
    t, d = x.shape

    def rows(width, col=0):
        return pl.BlockSpec((tm, width), lambda i: (i, col))

    def whole(a):
        return pl.BlockSpec(a.shape, lambda i: (0, 0))

    return pl.pallas_call(
        _merge_kernel,
        grid=(t // tm,),
        in_specs=[rows(SB_W), rows(SW_QW), rows(GDN_W),
                  rows(d, FP_GA // d), rows(d, FP_GB // d), rows(d, FP_GC // d), rows(d),
                  whole(wa), whole(wb), whole(wc), whole(wo), whole(gain)],
        out_specs=rows(d),
        out_shape=jax.ShapeDtypeStruct((t, d), F32),
        compiler_params=pltpu.CompilerParams(
            dimension_semantics=("arbitrary",), vmem_limit_bytes=VMEM_LIMIT),
        name="merge",
    )(ya, yb, yc, pfp, pfp, pfp, x, wa, wb, wc, wo, gain)


FFN_CK = 256


def _ffn_kernel(x_ref, gpre_ref, wup_ref, conv_ref, wdn_ref, gpost_ref, o_ref, carry_ref, f_ref, *, tm):
    @pl.when(pl.program_id(1) == 0)
    def _():
        carry_ref[...] = jnp.zeros_like(carry_ref)

    x = x_ref[...]
    hn = _rms(x, gpre_ref[...]).astype(BF16)

    def conv(col):
        cs = slice(col, col + FFN_CK)
        hid = _dot(hn, wup_ref[:, cs])
        prev8 = carry_ref[:, cs]
        w = conv_ref[:, cs]
        y = w[FFN_CONV - 1:FFN_CONV, :] * hid
        for s in range(1, FFN_CONV):
            y = y + w[FFN_CONV - 1 - s:FFN_CONV - s, :] * _shift_rows(hid, prev8, s)
        carry_ref[:, cs] = hid[tm - 8:tm, :]
        return y

    for ci in range(D_FF // FFN_CK):
        f_gate = conv(ci * FFN_CK)
        f_up = conv(D_FF + ci * FFN_CK)
        f_ref[:, ci * FFN_CK:(ci + 1) * FFN_CK] = (jax.nn.gelu(f_gate, approximate=True) * f_up).astype(BF16)
    o_ref[...] = x + _rms(_dot(f_ref[...], wdn_ref[...]), gpost_ref[...])


def _ffn(x, gpre, wup, conv_w, wdn, gpost, batch, seq, tm):
    steps = seq // tm
    d = x.shape[1]

    def whole(a):
        return pl.BlockSpec(a.shape, lambda b, i: (0, 0))

    return pl.pallas_call(
        functools.partial(_ffn_kernel, tm=tm),
        grid=(batch, steps),
        in_specs=[pl.BlockSpec((tm, d), lambda b, i: (b * steps + i, 0)),
                  whole(gpre), whole(wup), whole(conv_w), whole(wdn), whole(gpost)],
        out_specs=pl.BlockSpec((tm, d), lambda b, i: (b * steps + i, 0)),
        out_shape=jax.ShapeDtypeStruct(x.shape, F32),
        scratch_shapes=[pltpu.VMEM((8, 2 * D_FF), F32), pltpu.VMEM((tm, D_FF), BF16)],
        compiler_params=pltpu.CompilerParams(
            dimension_semantics=("arbitrary", "arbitrary"), vmem_limit_bytes=VMEM_LIMIT),
        name="ffn",
    )(x, gpre, wup, conv_w, wdn, gpost)


def _pack_w_in(w):
    o = 0
    parts = {}
    for name, width in (("aq", SB_W), ("ak", SB_W), ("av", SB_W), ("bq", SW_QW), ("bk", SW_KVW),
                        ("bv", SW_KVW), ("cqkv", 3 * GDN_W), ("cz", GDN_W), ("ca", GDN_HEADS),
                        ("cb", GDN_HEADS), ("ga", D_MODEL), ("gb", D_MODEL), ("gc", D_MODEL)):
        parts[name] = w[:, o:o + width]
        o += width

    def dup(t):
        return jnp.concatenate([t[:, h * HALF:(h + 1) * HALF] for h in range(SW_KV_HEADS) for _ in range(2)], axis=1)

    cols = [parts["aq"] * (SB_HEAD_DIM ** -0.5), parts["ak"], parts["av"],
            parts["bq"] * (SW_HEAD_DIM ** -0.5), dup(parts["bk"]), dup(parts["bv"]),
            parts["ga"], parts["gb"], parts["gc"], parts["cqkv"], parts["cz"]]
    pad = jnp.zeros((w.shape[0], LANES - 2 * GDN_HEADS), w.dtype)
    wab = jnp.concatenate([parts["ca"], parts["cb"], pad], axis=1)
    return jnp.concatenate(cols, axis=1).astype(BF16), wab.astype(BF16)


def _layer(x, batch, seq, p):
    pbf, pfp, pab = _inproj(x, p["ln_mix_pre"], p["w_in"], p["w_ab"], tm=2048)
    ya = _sb_attention(pbf, batch, seq)
    yb = _sw_attention(pbf, p["sinks"], p["slopes"], batch, seq)
    yc = _gdn(pfp, pab, p["gdn_conv"], p["gdn_small"], p["gdn_norm"], batch, seq, rows=512)
    x = _merge(ya, yb, yc, pfp, x, p["wa"], p["wb"], p["wc"], p["wo"], p["ln_mix_post"], tm=512)
    return _ffn(x, p["ln_ffn_pre"], p["w_up"], p["ffn_conv"], p["w_down"], p["ln_ffn_post"],
                batch, seq, tm=512)


def kernel(x, ln_mix_pre, w_in, sw_sinks, gdn_conv, gdn_a_log, gdn_dt_bias, gdn_norm, w_branch_a,
           w_branch_b, w_branch_c, w_out, ln_mix_post, ln_ffn_pre, w_up, ffn_conv, w_down, ln_ffn_post):
    batch, seq, d = x.shape
    depth = w_in.shape[0]
    slopes = jnp.exp2(-8.0 * jnp.arange(1, SW_Q_HEADS + 1, dtype=F32) / SW_Q_HEADS)
    h = x.reshape(batch * seq, d)
    for l in range(depth):
        small = jnp.zeros((8, LANES), F32)
        small = small.at[0, :GDN_HEADS].set(gdn_a_log[l]).at[1, :GDN_HEADS].set(gdn_dt_bias[l])
        w_main, w_ab = _pack_w_in(w_in[l])
        p = {
            "ln_mix_pre": ln_mix_pre[l][None, :], "w_in": w_main, "w_ab": w_ab,
            "sinks": sw_sinks[l], "slopes": slopes,
            "gdn_conv": gdn_conv[l], "gdn_small": small, "gdn_norm": gdn_norm[l][None, :],
            "wa": w_branch_a[l].astype(BF16), "wb": w_branch_b[l].astype(BF16),
            "wc": w_branch_c[l].astype(BF16), "wo": w_out[l].astype(BF16),
            "ln_mix_post": ln_mix_post[l][None, :], "ln_ffn_pre": ln_ffn_pre[l][None, :],
            "w_up": w_up[l].astype(BF16), "ffn_conv": ffn_conv[l], "w_down": w_down[l].astype(BF16),
            "ln_ffn_post": ln_ffn_post[l][None, :],
        }
        h = _layer(h, batch, seq, p)
    return h.reshape(batch, seq, d)
```

```python
import functools

import jax
import jax.numpy as jnp
from jax import lax
from jax.experimental import pallas as pl
from jax.experimental.pallas import tpu as pltpu

F32 = jnp.float32
BF16 = jnp.bfloat16
NORM_EPS = 1e-6

D_MODEL = 1024
SB_HEADS, SB_HEAD_DIM = 8, 64
SW_Q_HEADS, SW_KV_HEADS, SW_HEAD_DIM = 8, 2, 64
GDN_HEADS, GDN_HEAD_DIM, GDN_CONV = 4, 128, 4
D_FF, FFN_CONV = 2816, 3
SB_W = SB_HEADS * SB_HEAD_DIM
SW_QW = SW_Q_HEADS * SW_HEAD_DIM
SW_KVW = SW_KV_HEADS * SW_HEAD_DIM
GDN_W = GDN_HEADS * GDN_HEAD_DIM

BLK = 128
LANES = 128
HALF = 64
VMEM_LIMIT = 56 * 1024 * 1024

BF_SBQ, BF_SBK, BF_SBV, BF_SWQ, BF_SWK, BF_SWV, BF_COLS = 0, 512, 1024, 1536, 2048, 2304, 2560
FP_GA, FP_GB, FP_GC, FP_CQKV, FP_CZ, FP_COLS = 0, 1024, 2048, 3072, 4608, 5120
PROJ_TN = 512
F32_EXP_ZERO = -104.0


def _dot(a, b):
    return jnp.dot(a, b, preferred_element_type=F32)


def _dot_nt(a, b):
    return lax.dot_general(a, b, (((1,), (1,)), ((), ())), preferred_element_type=F32)


def _iota(shape, dim):
    return lax.broadcasted_iota(jnp.int32, shape, dim)


def _rms(t, gain):
    return t * lax.rsqrt(jnp.mean(t * t, axis=-1, keepdims=True) + NORM_EPS) * gain


def _shift_rows(h, prev8, s):
    r = pltpu.roll(h, s, axis=0)
    row = _iota(h.shape, 0)
    for t in range(s):
        r = jnp.where(row == t, prev8[8 - s + t:8 - s + t + 1, :], r)
    return r


def _inproj_kernel(x_ref, g_ref, w_ref, wab_ref, obf_ref, of_ref, oab_ref, xn_ref, *, nbf):
    j = pl.program_id(1)

    @pl.when(j == 0)
    def _():
        xn = _rms(x_ref[...], g_ref[...]).astype(BF16)
        xn_ref[...] = xn
        oab_ref[...] = _dot(xn, wab_ref[...])

    @pl.when(j < nbf)
    def _():
        obf_ref[...] = _dot(xn_ref[...], w_ref[...]).astype(BF16)

    @pl.when(j >= nbf)
    def _():
        of_ref[...] = _dot(xn_ref[...], w_ref[...])


def _inproj(x, gain, w, wab, layer, tm):
    t, d = x.shape
    tn = PROJ_TN
    nbf, nfp = BF_COLS // tn, FP_COLS // tn
    return pl.pallas_call(
        functools.partial(_inproj_kernel, nbf=nbf),
        grid=(t // tm, nbf + nfp),
        in_specs=[
            pl.BlockSpec((tm, d), lambda i, j: (i, 0)),
            pl.BlockSpec((None, 1, d), lambda i, j: (layer, 0, 0)),
            pl.BlockSpec((None, d, tn), lambda i, j: (layer, 0, j)),
            pl.BlockSpec((None, d, LANES), lambda i, j: (layer, 0, 0)),
        ],
        out_specs=[
            pl.BlockSpec((tm, tn), lambda i, j: (i, jnp.minimum(j, nbf - 1))),
            pl.BlockSpec((tm, tn), lambda i, j: (i, jnp.maximum(j - nbf, 0))),
            pl.BlockSpec((tm, LANES), lambda i, j: (i, 0)),
        ],
        out_shape=[jax.ShapeDtypeStruct((t, BF_COLS), BF16), jax.ShapeDtypeStruct((t, FP_COLS), F32),
                   jax.ShapeDtypeStruct((t, LANES), F32)],
        scratch_shapes=[pltpu.VMEM((tm, d), BF16)],
        compiler_params=pltpu.CompilerParams(
            dimension_semantics=("arbitrary", "arbitrary"), vmem_limit_bytes=VMEM_LIMIT),
        name="inproj",
    )(x, gain, w, wab)


SB_TQ = 2 * BLK
SB_PAIRS = 2


def _sb_kernel(q_ref, k_ref, v_ref, o_ref, acc_ref, lr_ref):
    ti = pl.program_id(2)
    lane = _iota((BLK, LANES), 1)
    row = _iota((BLK, LANES), 0)
    first = lane < HALF
    tri = lane < row
    full = lane >= 0
    zero_tile = jnp.zeros((BLK, LANES), F32)
    r2 = _iota((BLK, 2 * LANES), 0)
    c2 = _iota((BLK, 2 * LANES), 1)
    suffix_ones = jnp.where((r2 > c2) | (c2 >= LANES), 1.0, 0.0).astype(BF16)
    suffix_ones = jnp.concatenate([suffix_ones, suffix_ones], axis=0)

    def split_heads(q):
        sel = jnp.concatenate([first] * (q.shape[0] // BLK), axis=0)
        zero = jnp.zeros_like(q)
        return jnp.where(sel, q, zero), jnp.where(sel, zero, q)

    def log_parts(z):
        log_stay = -(jnp.maximum(z, 0.0) + jnp.log(1.0 + jnp.exp(-jnp.abs(z))))
        return log_stay, log_stay + z

    def suffix_sums(parts):
        hi = [x.astype(BF16) for x in parts]
        lo = [(x - h.astype(F32)).astype(BF16) for x, h in zip(parts, hi)]
        hilo = jnp.concatenate([jnp.concatenate(hi, axis=0), jnp.concatenate(lo, axis=0)], axis=1)
        sums = _dot(hilo, suffix_ones)
        out, o = [], 0
        for x in parts:
            out.append((sums[o:o + x.shape[0], :LANES], sums[o:o + x.shape[0], LANES:]))
            o += x.shape[0]
        return out

    def generic_block(pp, s, j, valid):
        ps = slice(pp * LANES, (pp + 1) * LANES)
        q0, q1 = split_heads(q_ref[s * BLK:(s + 1) * BLK, ps])
        off = pl.multiple_of(j * BLK, BLK)
        k, v = k_ref[pl.ds(off, BLK), ps], v_ref[pl.ds(off, BLK), ps]
        log_stay, log_take = log_parts(_dot_nt(jnp.concatenate([q0, q1], axis=0), k))
        if valid is not None:
            vm = jnp.concatenate([valid, valid], axis=0)
            log_stay = jnp.where(vm, log_stay, 0.0)
        (later, total), = suffix_sums([log_stay])
        w = jnp.exp(log_take + later + lr_ref[pp, s])
        if valid is not None:
            w = jnp.where(vm, w, 0.0)
        pv = _dot(w.astype(BF16), v)
        acc_ref[pp, s] += jnp.where(first, pv[:BLK], pv[BLK:])
        lr_ref[pp, s] += total

    def tail(pp, s, j0):
        def cond(c):
            j, live = c
            return jnp.logical_and(j >= 0, live)

        def body(c):
            j, _ = c
            generic_block(pp, s, j, None)
            return j - 1, jnp.max(lr_ref[pp, s]) > F32_EXP_ZERO

        lax.while_loop(cond, body, (j0, jnp.max(lr_ref[pp, s]) > F32_EXP_ZERO))

    @pl.when(ti == 0)
    def _():
        for pp in range(SB_PAIRS):
            for s in range(2):
                acc_ref[pp, s] = jnp.zeros((BLK, LANES), F32)
                lr_ref[pp, s] = jnp.zeros((2 * BLK, LANES), F32)
                generic_block(pp, s, 2 * ti + s, tri)
                tail(pp, s, 2 * ti + s - 1)

    @pl.when(ti > 0)
    def _():
        base = pl.multiple_of((2 * ti - 2) * BLK, BLK)
        m2 = jnp.concatenate([tri, full, tri, full], axis=0)
        m3 = jnp.concatenate([tri, tri], axis=0)
        for pp in range(SB_PAIRS):
            ps = slice(pp * LANES, (pp + 1) * LANES)
            q0, q1 = split_heads(q_ref[:, ps])
            kw = k_ref[pl.ds(base, 4 * BLK), ps]
            vw = v_ref[pl.ds(base, 4 * BLK), ps]
            kb = [kw[t * BLK:(t + 1) * BLK] for t in range(4)]
            l_ab = jnp.concatenate([q0, q1], axis=0)
            l_a = jnp.concatenate([q0[:BLK], q1[:BLK]], axis=0)
            l_b = jnp.concatenate([q0[BLK:], q1[BLK:]], axis=0)
            ls0, lt0 = log_parts(_dot_nt(l_a, kb[0]))
            ls1, lt1 = log_parts(_dot_nt(l_ab, kb[1]))
            ls2, lt2 = log_parts(_dot_nt(l_ab, kb[2]))
            ls3, lt3 = log_parts(_dot_nt(l_b, kb[3]))
            ls2 = jnp.where(m2, ls2, 0.0)
            ls3 = jnp.where(m3, ls3, 0.0)
            (lat0, tot0), (lat1, tot1), (lat2, tot2), (lat3, tot3) = suffix_sums([ls0, ls1, ls2, ls3])
            lrb2 = jnp.concatenate([zero_tile, tot3[:BLK], zero_tile, tot3[BLK:]], axis=0)
            lrb1 = lrb2 + tot2
            lre1 = lrb1 + tot1
            lrb0 = jnp.concatenate([lre1[:BLK], lre1[2 * BLK:3 * BLK]], axis=0)
            w3 = jnp.where(m3, jnp.exp(lt3 + lat3), 0.0)
            w2 = jnp.where(m2, jnp.exp(lt2 + lat2 + lrb2), 0.0)
            w1 = jnp.exp(lt1 + lat1 + lrb1)
            w0 = jnp.exp(lt0 + lat0 + lrb0)
            w12 = jnp.concatenate([w1.astype(BF16), w2.astype(BF16)], axis=1)
            pv12 = _dot(w12, vw[BLK:3 * BLK])
            pv0 = _dot(w0.astype(BF16), vw[:BLK])
            pv3 = _dot(w3.astype(BF16), vw[3 * BLK:])
            acc_ref[pp, 0] = jnp.where(first, pv12[:BLK] + pv0[:BLK], pv12[2 * BLK:3 * BLK] + pv0[BLK:])
            acc_ref[pp, 1] = jnp.where(first, pv12[BLK:2 * BLK] + pv3[:BLK], pv12[3 * BLK:] + pv3[BLK:])
            lr_ref[pp, 0] = lrb0 + tot0
            lr_ref[pp, 1] = jnp.concatenate([lre1[BLK:2 * BLK], lre1[3 * BLK:]], axis=0)

        @pl.when(jnp.max(lr_ref[...]) > F32_EXP_ZERO)
        def _():
            for pp in range(SB_PAIRS):
                tail(pp, 0, 2 * ti - 3)
                tail(pp, 1, 2 * ti - 2)

    for pp in range(SB_PAIRS):
        for s in range(2):
            o_ref[s * BLK:(s + 1) * BLK, pp * LANES:(pp + 1) * LANES] = acc_ref[pp, s].astype(o_ref.dtype)


def _sb_attention(pbf, batch, seq):
    nt = seq // SB_TQ
    w = SB_PAIRS * LANES
    cq, ck, cv = BF_SBQ // w, BF_SBK // w, BF_SBV // w
    return pl.pallas_call(
        _sb_kernel,
        grid=(batch, SB_W // w, nt),
        in_specs=[
            pl.BlockSpec((SB_TQ, w), lambda b, p, i: (b * nt + i, cq + p)),
            pl.BlockSpec((seq, w), lambda b, p, i: (b, ck + p)),
            pl.BlockSpec((seq, w), lambda b, p, i: (b, cv + p)),
        ],
        out_specs=pl.BlockSpec((SB_TQ, w), lambda b, p, i: (b * nt + i, p)),
        out_shape=jax.ShapeDtypeStruct((batch * seq, SB_W), BF16),
        scratch_shapes=[pltpu.VMEM((SB_PAIRS, 2, BLK, LANES), F32),
                        pltpu.VMEM((SB_PAIRS, 2, 2 * BLK, LANES), F32)],
        compiler_params=pltpu.CompilerParams(
            dimension_semantics=("arbitrary", "arbitrary", "arbitrary"), vmem_limit_bytes=VMEM_LIMIT),
        name="sb_attention",
    )(pbf, pbf, pbf)


SW_TQ = 2 * BLK


def _sw_kernel(sinks_ref, slopes_ref, q_ref, kp_ref, kc_ref, vp_ref, vc_ref, o_ref, *, layer):
    ti = pl.program_id(1)
    lane = _iota((BLK, LANES), 1)
    row = _iota((BLK, LANES), 0)
    first = lane < HALF
    cur = lane <= row
    dist = jnp.where(cur, row - lane, row - lane + BLK).astype(F32)
    group = SW_Q_HEADS // SW_KV_HEADS
    for s in range(SW_TQ // BLK):
        rs = slice(s * BLK, (s + 1) * BLK)
        for hk in range(SW_KV_HEADS):
            ks = slice(hk * LANES, (hk + 1) * LANES)
            kc, vc = kc_ref[rs, ks], vc_ref[rs, ks]
            if s == 0:
                kp, vp = kp_ref[:, ks], vp_ref[:, ks]
            else:
                kp, vp = kc_ref[(s - 1) * BLK:s * BLK, ks], vc_ref[(s - 1) * BLK:s * BLK, ks]
            pairs = range(hk * group // 2, (hk + 1) * group // 2)
            lhs = []
            for p in pairs:
                qp = q_ref[rs, p * LANES:(p + 1) * LANES]
                zero = jnp.zeros_like(qp)
                lhs += [jnp.where(first, qp, zero), jnp.where(first, zero, qp)]
            lhs = jnp.concatenate(lhs, axis=0)
            s_cur = _dot_nt(lhs, kc)
            s_prev = _dot_nt(lhs, kp)
            prob_c, prob_p = [], []
            for g in range(group):
                head = hk * group + g
                sink = sinks_ref[layer, head]
                gs = slice(g * BLK, (g + 1) * BLK)
                sc = jnp.where(cur, s_cur[gs], s_prev[gs]) - slopes_ref[head] * dist
                if s == 0:
                    sc = jnp.where(jnp.logical_or(cur, ti > 0), sc, -jnp.inf)
                m = jnp.maximum(jnp.max(sc, axis=-1, keepdims=True), sink)
                e = jnp.exp(sc - m)
                denom = jnp.sum(e, axis=-1, keepdims=True) + jnp.exp(sink - m)
                prob = (e / denom).astype(BF16)
                pz = jnp.zeros_like(prob)
                prob_c.append(jnp.where(cur, prob, pz))
                prob_p.append(jnp.where(cur, pz, prob))
            pv = _dot(jnp.concatenate(prob_c, axis=0), vc) + _dot(jnp.concatenate(prob_p, axis=0), vp)
            for t, p in enumerate(pairs):
                o_ref[rs, p * LANES:(p + 1) * LANES] = jnp.where(
                    first, pv[2 * t * BLK:(2 * t + 1) * BLK], pv[(2 * t + 1) * BLK:(2 * t + 2) * BLK]
                ).astype(o_ref.dtype)


def _sw_attention(pbf, sinks, slopes, layer, batch, seq):
    nt = seq // SW_TQ
    kvw = 2 * SW_KVW
    cq, ck, cv = BF_SWQ // SW_QW, BF_SWK // kvw, BF_SWV // kvw
    per = SW_TQ // BLK

    def prev(b, i):
        return b * nt * per + jnp.maximum(i * per - 1, 0)

    smem = pl.BlockSpec(memory_space=pltpu.SMEM)
    return pl.pallas_call(
        functools.partial(_sw_kernel, layer=layer),
        grid=(batch, nt),
        in_specs=[
            smem, smem,
            pl.BlockSpec((SW_TQ, SW_QW), lambda b, i: (b * nt + i, cq)),
            pl.BlockSpec((BLK, kvw), lambda b, i: (prev(b, i), ck)),
            pl.BlockSpec((SW_TQ, kvw), lambda b, i: (b * nt + i, ck)),
            pl.BlockSpec((BLK, kvw), lambda b, i: (prev(b, i), cv)),
            pl.BlockSpec((SW_TQ, kvw), lambda b, i: (b * nt + i, cv)),
        ],
        out_specs=pl.BlockSpec((SW_TQ, SW_QW), lambda b, i: (b * nt + i, 0)),
        out_shape=jax.ShapeDtypeStruct((batch * seq, SW_QW), BF16),
        compiler_params=pltpu.CompilerParams(
            dimension_semantics=("arbitrary", "arbitrary"), vmem_limit_bytes=VMEM_LIMIT),
        name="sw_attention",
    )(sinks, slopes, pbf, pbf, pbf, pbf, pbf)


GDN_CHUNK = 128


def _split3(x):
    x1 = x.astype(BF16)
    r1 = x - x1.astype(F32)
    x2 = r1.astype(BF16)
    x3 = (r1 - x2.astype(F32)).astype(BF16)
    return x1, x2, x3


def _gdn_kernel(q_ref, k_ref, v_ref, z_ref, ab_ref, conv_ref, small_ref, norm_ref, o_ref,
                carry_ref, state_ref, yq_ref, yk_ref, yv_ref, *, rows):
    n = GDN_CHUNK
    nchunks = rows // n
    step = pl.program_id(1)

    @pl.when(step == 0)
    def _():
        carry_ref[...] = jnp.zeros_like(carry_ref)
        state_ref[...] = jnp.zeros_like(state_ref)

    for idx, (src, dst) in enumerate(((q_ref, yq_ref), (k_ref, yk_ref), (v_ref, yv_ref))):
        cols = slice(idx * GDN_W, (idx + 1) * GDN_W)
        w = conv_ref[:, cols]
        for ci in range(nchunks):
            x = src[ci * n:(ci + 1) * n, :]
            prev8 = carry_ref[:, cols] if ci == 0 else src[ci * n - 8:ci * n, :]
            y = w[GDN_CONV - 1:GDN_CONV, :] * x
            for s in range(1, GDN_CONV):
                y = y + w[GDN_CONV - 1 - s:GDN_CONV - s, :] * _shift_rows(x, prev8, s)
            y = y * jax.nn.sigmoid(y)
            for h in range(GDN_HEADS):
                yh = y[:, h * GDN_HEAD_DIM:(h + 1) * GDN_HEAD_DIM]
                if idx < 2:
                    inv = lax.rsqrt(jnp.sum(yh * yh, axis=-1, keepdims=True) + NORM_EPS)
                    yh = yh * (inv * (GDN_HEAD_DIM ** -0.5) if idx == 0 else inv)
                dst[ci * n:(ci + 1) * n, h * GDN_HEAD_DIM:(h + 1) * GDN_HEAD_DIM] = yh
        carry_ref[:, cols] = src[rows - 8:rows, :]

    r = _iota((n, n), 0)
    c = _iota((n, n), 1)
    causal = r >= c
    strict = r > c
    lower_ones = jnp.where(causal, 1.0, 0.0).astype(BF16)
    head_lane = _iota((n, LANES), 1) < GDN_HEADS

    def sub_blocks(ls):
        return ((r >> (ls + 1)) == (c >> (ls + 1))) & ((r >> ls) != (c >> ls))

    probs = []
    for ci in range(nchunks):
        rs = slice(ci * n, (ci + 1) * n)
        ab = ab_ref[rs, :]
        zab = ab + small_ref[1:2, :]
        softplus = jnp.maximum(zab, 0.0) + jnp.log(1.0 + jnp.exp(-jnp.abs(zab)))
        gbeta = jnp.where(head_lane, -jnp.exp(small_ref[0:1, :]) * softplus, jax.nn.sigmoid(ab))
        gc_all = sum(_dot(lower_ones, part) for part in _split3(gbeta))
        gc_t = gc_all.T
        for h in range(GDN_HEADS):
            hs = slice(h * GDN_HEAD_DIM, (h + 1) * GDN_HEAD_DIM)
            q, k, v = yq_ref[rs, hs], yk_ref[rs, hs], yv_ref[rs, hs]
            gc = jnp.broadcast_to(gc_all[:, h:h + 1], (n, n))
            gc_row = jnp.broadcast_to(gc_t[h:h + 1, :], (n, n))
            g_tot = jnp.broadcast_to(gc[n - 1:n, :], (n, n))
            beta_b = jnp.broadcast_to(gbeta[:, GDN_HEADS + h:GDN_HEADS + h + 1], (n, n))
            decay = jnp.exp(jnp.where(causal, gc - gc_row, -jnp.inf))
            k_beta = k * beta_b
            k16 = k.astype(BF16)
            e_gc = jnp.exp(gc)
            probs.append(dict(
                rs=rs, hs=hs, h=h,
                lower=jnp.where(strict, _dot_nt(k_beta.astype(BF16), k16) * decay, 0.0),
                a=jnp.where(causal, _dot_nt(q.astype(BF16), k16) * decay, 0.0).astype(BF16),
                qd=(q * e_gc).astype(BF16),
                kdt=(k * jnp.exp(g_tot - gc)).T.astype(BF16),
                gl=jnp.exp(g_tot),
                rhs=jnp.concatenate([v * beta_b, k_beta * e_gc], axis=1)))
    es = [-jnp.where((r >> 1) == (c >> 1), p["lower"], 0.0) for p in probs]
    for ls in range(1, n.bit_length() - 1):
        ms = [jnp.where(sub_blocks(ls), p["lower"], 0.0) for p in probs]
        ps = [_dot(e.astype(BF16), m.astype(BF16)) for e, m in zip(es, ms)]
        es = [e - m - (pp + _dot((m + pp).astype(BF16), e.astype(BF16))) for e, m, pp in zip(es, ms, ps)]
    for p, e in zip(probs, es):
        uw = p["rhs"] + _dot(e.astype(BF16), p["rhs"].astype(BF16))
        p["u"] = uw[:, :n]
        p["wq"] = jnp.concatenate([uw[:, n:].astype(BF16), p["qd"]], axis=0)
        p["ak"] = jnp.concatenate([p["a"], p["kdt"]], axis=0)

    gain = norm_ref[...]
    states = [state_ref[h] for h in range(GDN_HEADS)]
    for p in probs:
        h = p["h"]
        sr = _dot(p["wq"], states[h].astype(BF16))
        vn16 = (p["u"] - sr[:n]).astype(BF16)
        vr = _dot(p["ak"], vn16)
        o = sr[n:] + vr[:n]
        states[h] = states[h] * p["gl"] + vr[n:]
        zg = z_ref[p["rs"], p["hs"]]
        o_ref[p["rs"], p["hs"]] = (_rms(o, gain) * (zg * jax.nn.sigmoid(zg))).astype(o_ref.dtype)
    for h in range(GDN_HEADS):
        state_ref[h] = states[h]


def _gdn(pfp, pab, conv_w, small, norm, layer, batch, seq, rows):
    steps = seq // rows
    w = GDN_W
    cq = FP_CQKV // w

    def blk(col):
        return pl.BlockSpec((rows, w), lambda b, i: (b * steps + i, col))

    return pl.pallas_call(
        functools.partial(_gdn_kernel, rows=rows),
        grid=(batch, steps),
        in_specs=[
            blk(cq), blk(cq + 1), blk(cq + 2), blk(FP_CZ // w),
            pl.BlockSpec((rows, LANES), lambda b, i: (b * steps + i, 0)),
            pl.BlockSpec((None, GDN_CONV, 3 * w), lambda b, i: (layer, 0, 0)),
            pl.BlockSpec((None, 8, LANES), lambda b, i: (layer, 0, 0)),
            pl.BlockSpec((None, 1, GDN_HEAD_DIM), lambda b, i: (layer, 0, 0)),
        ],
        out_specs=pl.BlockSpec((rows, w), lambda b, i: (b * steps + i, 0)),
        out_shape=jax.ShapeDtypeStruct((batch * seq, w), BF16),
        scratch_shapes=[
            pltpu.VMEM((8, 3 * w), F32),
            pltpu.VMEM((GDN_HEADS, GDN_HEAD_DIM, GDN_HEAD_DIM), F32),
            pltpu.VMEM((rows, w), F32), pltpu.VMEM((rows, w), F32), pltpu.VMEM((rows, w), F32),
        ],
        compiler_params=pltpu.CompilerParams(
            dimension_semantics=("arbitrary", "arbitrary"), vmem_limit_bytes=VMEM_LIMIT),
        name="gdn",
    )(pfp, pfp, pfp, pfp, pab, conv_w, small, norm)


def _merge_kernel(ya_ref, yb_ref, yc_ref, ga_ref, gb_ref, gc_ref, x_ref,
                  wa_ref, wb_ref, wc_ref, wo_ref, gain_ref, o_ref):
    merged = (jax.nn.sigmoid(ga_ref[...]) * _dot(ya_ref[...], wa_ref[...])
              + jax.nn.sigmoid(gb_ref[...]) * _dot(yb_ref[...], wb_ref[...])
              + jax.nn.sigmoid(gc_ref[...]) * _dot(yc_ref[...], wc_ref[...]))
    r = _dot(merged.astype(BF16), wo_ref[...])
    o_ref[...] = x_ref[...] + _rms(r, gain_ref[...])


def _merge(ya, yb, yc, pfp, x, wa, wb, wc, wo, gain, layer, tm):
    t, d = x.shape

    def rows(width, col=0):
        return pl.BlockSpec((tm, width), lambda i: (i, col))

    def whole(a):
        return pl.BlockSpec((None,) + a.shape[1:], lambda i: (layer, 0, 0))

    return pl.pallas_call(
        _merge_kernel,
        grid=(t // tm,),
        in_specs=[rows(SB_W), rows(SW_QW), rows(GDN_W),
                  rows(d, FP_GA // d), rows(d, FP_GB // d), rows(d, FP_GC // d), rows(d),
                  whole(wa), whole(wb), whole(wc), whole(wo), whole(gain)],
        out_specs=rows(d),
        out_shape=jax.ShapeDtypeStruct((t, d), F32),
        compiler_params=pltpu.CompilerParams(
            dimension_semantics=("arbitrary",), vmem_limit_bytes=VMEM_LIMIT),
        name="merge",
    )(ya, yb, yc, pfp, pfp, pfp, x, wa, wb, wc, wo, gain)


FFN_CK = 256


def _ffn_kernel(x_ref, gpre_ref, wup_ref, conv_ref, wdn_ref, gpost_ref, o_ref, carry_ref, f_ref, *, tm):
    @pl.when(pl.program_id(1) == 0)
    def _():
        carry_ref[...] = jnp.zeros_like(carry_ref)

    x = x_ref[...]
    hn = _rms(x, gpre_ref[...]).astype(BF16)

    def conv(col):
        cs = slice(col, col + FFN_CK)
        hid = _dot(hn, wup_ref[:, cs])
        prev8 = carry_ref[:, cs]
        w = conv_ref[:, cs]
        y = w[FFN_CONV - 1:FFN_CONV, :] * hid
        for s in range(1, FFN_CONV):
            y = y + w[FFN_CONV - 1 - s:FFN_CONV - s, :] * _shift_rows(hid, prev8, s)
        carry_ref[:, cs] = hid[tm - 8:tm, :]
        return y

    for ci in range(D_FF // FFN_CK):
        f_gate = conv(ci * FFN_CK)
        f_up = conv(D_FF + ci * FFN_CK)
        f_ref[:, ci * FFN_CK:(ci + 1) * FFN_CK] = (jax.nn.gelu(f_gate, approximate=True) * f_up).astype(BF16)
    o_ref[...] = x + _rms(_dot(f_ref[...], wdn_ref[...]), gpost_ref[...])


def _ffn(x, gpre, wup, conv_w, wdn, gpost, layer, batch, seq, tm):
    steps = seq // tm
    d = x.shape[1]

    def whole(a):
        return pl.BlockSpec((None,) + a.shape[1:], lambda b, i: (layer, 0, 0))

    return pl.pallas_call(
        functools.partial(_ffn_kernel, tm=tm),
        grid=(batch, steps),
        in_specs=[pl.BlockSpec((tm, d), lambda b, i: (b * steps + i, 0)),
                  whole(gpre), whole(wup), whole(conv_w), whole(wdn), whole(gpost)],
        out_specs=pl.BlockSpec((tm, d), lambda b, i: (b * steps + i, 0)),
        out_shape=jax.ShapeDtypeStruct(x.shape, F32),
        scratch_shapes=[pltpu.VMEM((8, 2 * D_FF), F32), pltpu.VMEM((tm, D_FF), BF16)],
        compiler_params=pltpu.CompilerParams(
            dimension_semantics=("arbitrary", "arbitrary"), vmem_limit_bytes=VMEM_LIMIT),
        name="ffn",
    )(x, gpre, wup, conv_w, wdn, gpost)


def _pack_w_in(w):
    w = w.astype(BF16)
    o = 0
    parts = {}
    for name, width in (("aq", SB_W), ("ak", SB_W), ("av", SB_W), ("bq", SW_QW), ("bk", SW_KVW),
                        ("bv", SW_KVW), ("cqkv", 3 * GDN_W), ("cz", GDN_W), ("ca", GDN_HEADS),
                        ("cb", GDN_HEADS), ("ga", D_MODEL), ("gb", D_MODEL), ("gc", D_MODEL)):
        parts[name] = w[..., o:o + width]
        o += width

    def dup(t):
        return jnp.concatenate([t[..., h * HALF:(h + 1) * HALF] for h in range(SW_KV_HEADS) for _ in range(2)], axis=-1)

    cols = [parts["aq"] * (SB_HEAD_DIM ** -0.5), parts["ak"], parts["av"],
            parts["bq"] * (SW_HEAD_DIM ** -0.5), dup(parts["bk"]), dup(parts["bv"]),
            parts["ga"], parts["gb"], parts["gc"], parts["cqkv"], parts["cz"]]
    pad = jnp.zeros(w.shape[:-1] + (LANES - 2 * GDN_HEADS,), w.dtype)
    wab = jnp.concatenate([parts["ca"], parts["cb"], pad], axis=-1)
    return jnp.concatenate(cols, axis=-1).astype(BF16), wab


def _layer(x, batch, seq, layer, p):
    pbf, pfp, pab = _inproj(x, p["ln_mix_pre"], p["w_in"], p["w_ab"], layer, tm=2048)
    ya = _sb_attention(pbf, batch, seq)
    yb = _sw_attention(pbf, p["sinks"], p["slopes"], layer, batch, seq)
    yc = _gdn(pfp, pab, p["gdn_conv"], p["gdn_small"], p["gdn_norm"], layer, batch, seq, rows=512)
    x = _merge(ya, yb, yc, pfp, x, p["wa"], p["wb"], p["wc"], p["wo"], p["ln_mix_post"], layer, tm=512)
    return _ffn(x, p["ln_ffn_pre"], p["w_up"], p["ffn_conv"], p["w_down"], p["ln_ffn_post"],
                layer, batch, seq, tm=512)


def kernel(x, ln_mix_pre, w_in, sw_sinks, gdn_conv, gdn_a_log, gdn_dt_bias, gdn_norm, w_branch_a,
           w_branch_b, w_branch_c, w_out, ln_mix_post, ln_ffn_pre, w_up, ffn_conv, w_down, ln_ffn_post):
    batch, seq, d = x.shape
    depth = w_in.shape[0]
    small = jnp.zeros((depth, 8, LANES), F32)
    small = small.at[:, 0, :GDN_HEADS].set(gdn_a_log).at[:, 1, :GDN_HEADS].set(gdn_dt_bias)
    w_main, w_ab = _pack_w_in(w_in)
    p = {
        "ln_mix_pre": ln_mix_pre[:, None, :], "w_in": w_main, "w_ab": w_ab,
        "sinks": sw_sinks, "slopes": jnp.exp2(-8.0 * jnp.arange(1, SW_Q_HEADS + 1, dtype=F32) / SW_Q_HEADS),
        "gdn_conv": gdn_conv, "gdn_small": small, "gdn_norm": gdn_norm[:, None, :],
        "wa": w_branch_a.astype(BF16), "wb": w_branch_b.astype(BF16),
        "wc": w_branch_c.astype(BF16), "wo": w_out.astype(BF16),
        "ln_mix_post": ln_mix_post[:, None, :], "ln_ffn_pre": ln_ffn_pre[:, None, :],
        "w_up": w_up.astype(BF16), "ffn_conv": ffn_conv, "w_down": w_down.astype(BF16),
        "ln_ffn_post": ln_ffn_post[:, None, :],
    }
    h = x.reshape(batch * seq, d)
    for layer in range(depth):
        h = _layer(h, batch, seq, layer, p)
    return h.reshape(batch, seq, d)
```

```python
import functools

import jax
import jax.numpy as jnp
from jax import lax
from jax.experimental import pallas as pl
from jax.experimental.pallas import tpu as pltpu

F32 = jnp.float32
BF16 = jnp.bfloat16
NORM_EPS = 1e-6

D_MODEL = 1024
SB_HEADS, SB_HEAD_DIM = 8, 64
SW_Q_HEADS, SW_KV_HEADS, SW_HEAD_DIM = 8, 2, 64
GDN_HEADS, GDN_HEAD_DIM, GDN_CONV = 4, 128, 4
D_FF, FFN_CONV = 2816, 3
SB_W = SB_HEADS * SB_HEAD_DIM
SW_QW = SW_Q_HEADS * SW_HEAD_DIM
SW_KVW = SW_KV_HEADS * SW_HEAD_DIM
GDN_W = GDN_HEADS * GDN_HEAD_DIM

BLK = 128
LANES = 128
HALF = 64
VMEM_LIMIT = 56 * 1024 * 1024

BF_GA, BF_GB, BF_GC, BF_SBQ, BF_SBK, BF_SBV, BF_SWQ, BF_SWK, BF_SWV, BF_COLS = (
    0, 1024, 2048, 3072, 3584, 4096, 4608, 5120, 5376, 5632)
FP_CQKV, FP_CZ, FP_COLS = 0, 1536, 2048
PROJ_TN = 512
F32_EXP_ZERO = -104.0


def _dot(a, b):
    return jnp.dot(a, b, preferred_element_type=F32)


def _dot_nt(a, b):
    return lax.dot_general(a, b, (((1,), (1,)), ((), ())), preferred_element_type=F32)


def _iota(shape, dim):
    return lax.broadcasted_iota(jnp.int32, shape, dim)


def _rms(t, gain):
    return t * lax.rsqrt(jnp.mean(t * t, axis=-1, keepdims=True) + NORM_EPS) * gain


def _shift_rows(h, prev8, s):
    r = pltpu.roll(h, s, axis=0)
    row = _iota(h.shape, 0)
    for t in range(s):
        r = jnp.where(row == t, prev8[8 - s + t:8 - s + t + 1, :], r)
    return r


def _inproj_kernel(x_ref, g_ref, w_ref, wab_ref, obf_ref, of_ref, oab_ref, xn_ref, *, nbf):
    j = pl.program_id(1)

    @pl.when(j == 0)
    def _():
        xn = _rms(x_ref[...], g_ref[...]).astype(BF16)
        xn_ref[...] = xn
        oab_ref[...] = _dot(xn, wab_ref[...])

    @pl.when(j < nbf)
    def _():
        obf_ref[...] = _dot(xn_ref[...], w_ref[...]).astype(BF16)

    @pl.when(j >= nbf)
    def _():
        of_ref[...] = _dot(xn_ref[...], w_ref[...])


def _inproj(x, gain, w, wab, layer, tm):
    t, d = x.shape
    tn = PROJ_TN
    nbf, nfp = BF_COLS // tn, FP_COLS // tn
    return pl.pallas_call(
        functools.partial(_inproj_kernel, nbf=nbf),
        grid=(t // tm, nbf + nfp),
        in_specs=[
            pl.BlockSpec((tm, d), lambda i, j: (i, 0)),
            pl.BlockSpec((None, 1, d), lambda i, j: (layer, 0, 0)),
            pl.BlockSpec((None, d, tn), lambda i, j: (layer, 0, j)),
            pl.BlockSpec((None, d, LANES), lambda i, j: (layer, 0, 0)),
        ],
        out_specs=[
            pl.BlockSpec((tm, tn), lambda i, j: (i, jnp.minimum(j, nbf - 1))),
            pl.BlockSpec((tm, tn), lambda i, j: (i, jnp.maximum(j - nbf, 0))),
            pl.BlockSpec((tm, LANES), lambda i, j: (i, 0)),
        ],
        out_shape=[jax.ShapeDtypeStruct((t, BF_COLS), BF16), jax.ShapeDtypeStruct((t, FP_COLS), F32),
                   jax.ShapeDtypeStruct((t, LANES), F32)],
        scratch_shapes=[pltpu.VMEM((tm, d), BF16)],
        compiler_params=pltpu.CompilerParams(
            dimension_semantics=("arbitrary", "arbitrary"), vmem_limit_bytes=VMEM_LIMIT),
        name="inproj",
    )(x, gain, w, wab)


SB_TQ = 2 * BLK
SB_PAIRS = 2


def _sb_kernel(q_ref, k_ref, v_ref, o_ref, acc_ref, lr_ref):
    ti = pl.program_id(2)
    lane = _iota((BLK, LANES), 1)
    row = _iota((BLK, LANES), 0)
    first = lane < HALF
    tri = lane < row
    full = lane >= 0
    zero_tile = jnp.zeros((BLK, LANES), F32)
    r2 = _iota((BLK, 2 * LANES), 0)
    c2 = _iota((BLK, 2 * LANES), 1)
    suffix_ones = jnp.where((r2 > c2) | (c2 >= LANES), 1.0, 0.0).astype(BF16)
    suffix_ones = jnp.concatenate([suffix_ones, suffix_ones], axis=0)

    def split_heads(q):
        sel = jnp.concatenate([first] * (q.shape[0] // BLK), axis=0)
        zero = jnp.zeros_like(q)
        return jnp.where(sel, q, zero), jnp.where(sel, zero, q)

    def log_parts(z):
        log_stay = -(jnp.maximum(z, 0.0) + jnp.log(1.0 + jnp.exp(-jnp.abs(z))))
        return log_stay, log_stay + z

    def suffix_sums(parts):
        hi = [x.astype(BF16) for x in parts]
        lo = [(x - h.astype(F32)).astype(BF16) for x, h in zip(parts, hi)]
        hilo = jnp.concatenate([jnp.concatenate(hi, axis=0), jnp.concatenate(lo, axis=0)], axis=1)
        sums = _dot(hilo, suffix_ones)
        out, o = [], 0
        for x in parts:
            out.append((sums[o:o + x.shape[0], :LANES], sums[o:o + x.shape[0], LANES:]))
            o += x.shape[0]
        return out

    def generic_block(pp, s, j, valid):
        ps = slice(pp * LANES, (pp + 1) * LANES)
        q0, q1 = split_heads(q_ref[s * BLK:(s + 1) * BLK, ps])
        off = pl.multiple_of(j * BLK, BLK)
        k, v = k_ref[pl.ds(off, BLK), ps], v_ref[pl.ds(off, BLK), ps]
        log_stay, log_take = log_parts(_dot_nt(jnp.concatenate([q0, q1], axis=0), k))
        if valid is not None:
            vm = jnp.concatenate([valid, valid], axis=0)
            log_stay = jnp.where(vm, log_stay, 0.0)
        (later, total), = suffix_sums([log_stay])
        w = jnp.exp(log_take + later + lr_ref[pp, s])
        if valid is not None:
            w = jnp.where(vm, w, 0.0)
        pv = _dot(w.astype(BF16), v)
        acc_ref[pp, s] += jnp.where(first, pv[:BLK], pv[BLK:])
        lr_ref[pp, s] += total

    def tail(pp, s, j0):
        def cond(c):
            j, live = c
            return jnp.logical_and(j >= 0, live)

        def body(c):
            j, _ = c
            generic_block(pp, s, j, None)
            return j - 1, jnp.max(lr_ref[pp, s]) > F32_EXP_ZERO

        lax.while_loop(cond, body, (j0, jnp.max(lr_ref[pp, s]) > F32_EXP_ZERO))

    @pl.when(ti == 0)
    def _():
        for pp in range(SB_PAIRS):
            for s in range(2):
                acc_ref[pp, s] = jnp.zeros((BLK, LANES), F32)
                lr_ref[pp, s] = jnp.zeros((2 * BLK, LANES), F32)
                generic_block(pp, s, 2 * ti + s, tri)
                tail(pp, s, 2 * ti + s - 1)

    @pl.when(ti > 0)
    def _():
        base = pl.multiple_of((2 * ti - 2) * BLK, BLK)
        m2 = jnp.concatenate([tri, full, tri, full], axis=0)
        m3 = jnp.concatenate([tri, tri], axis=0)
        for pp in range(SB_PAIRS):
            ps = slice(pp * LANES, (pp + 1) * LANES)
            q0, q1 = split_heads(q_ref[:, ps])
            kw = k_ref[pl.ds(base, 4 * BLK), ps]
            vw = v_ref[pl.ds(base, 4 * BLK), ps]
            kb = [kw[t * BLK:(t + 1) * BLK] for t in range(4)]
            l_ab = jnp.concatenate([q0, q1], axis=0)
            l_a = jnp.concatenate([q0[:BLK], q1[:BLK]], axis=0)
            l_b = jnp.concatenate([q0[BLK:], q1[BLK:]], axis=0)
            ls0, lt0 = log_parts(_dot_nt(l_a, kb[0]))
            ls1, lt1 = log_parts(_dot_nt(l_ab, kb[1]))
            ls2, lt2 = log_parts(_dot_nt(l_ab, kb[2]))
            ls3, lt3 = log_parts(_dot_nt(l_b, kb[3]))
            ls2 = jnp.where(m2, ls2, 0.0)
            ls3 = jnp.where(m3, ls3, 0.0)
            (lat0, tot0), (lat1, tot1), (lat2, tot2), (lat3, tot3) = suffix_sums([ls0, ls1, ls2, ls3])
            lrb2 = jnp.concatenate([zero_tile, tot3[:BLK], zero_tile, tot3[BLK:]], axis=0)
            lrb1 = lrb2 + tot2
            lre1 = lrb1 + tot1
            lrb0 = jnp.concatenate([lre1[:BLK], lre1[2 * BLK:3 * BLK]], axis=0)
            w3 = jnp.where(m3, jnp.exp(lt3 + lat3), 0.0)
            w2 = jnp.where(m2, jnp.exp(lt2 + lat2 + lrb2), 0.0)
            w1 = jnp.exp(lt1 + lat1 + lrb1)
            w0 = jnp.exp(lt0 + lat0 + lrb0)
            w12 = jnp.concatenate([w1.astype(BF16), w2.astype(BF16)], axis=1)
            pv12 = _dot(w12, vw[BLK:3 * BLK])
            pv0 = _dot(w0.astype(BF16), vw[:BLK])
            pv3 = _dot(w3.astype(BF16), vw[3 * BLK:])
            acc_ref[pp, 0] = jnp.where(first, pv12[:BLK] + pv0[:BLK], pv12[2 * BLK:3 * BLK] + pv0[BLK:])
            acc_ref[pp, 1] = jnp.where(first, pv12[BLK:2 * BLK] + pv3[:BLK], pv12[3 * BLK:] + pv3[BLK:])
            lr_ref[pp, 0] = lrb0 + tot0
            lr_ref[pp, 1] = jnp.concatenate([lre1[BLK:2 * BLK], lre1[3 * BLK:]], axis=0)

        @pl.when(jnp.max(lr_ref[...]) > F32_EXP_ZERO)
        def _():
            for pp in range(SB_PAIRS):
                tail(pp, 0, 2 * ti - 3)
                tail(pp, 1, 2 * ti - 2)

    for pp in range(SB_PAIRS):
        for s in range(2):
            o_ref[s * BLK:(s + 1) * BLK, pp * LANES:(pp + 1) * LANES] = acc_ref[pp, s].astype(o_ref.dtype)


def _sb_attention(pbf, batch, seq):
    nt = seq // SB_TQ
    w = SB_PAIRS * LANES
    cq, ck, cv = BF_SBQ // w, BF_SBK // w, BF_SBV // w
    return pl.pallas_call(
        _sb_kernel,
        grid=(batch, SB_W // w, nt),
        in_specs=[
            pl.BlockSpec((SB_TQ, w), lambda b, p, i: (b * nt + i, cq + p)),
            pl.BlockSpec((seq, w), lambda b, p, i: (b, ck + p)),
            pl.BlockSpec((seq, w), lambda b, p, i: (b, cv + p)),
        ],
        out_specs=pl.BlockSpec((SB_TQ, w), lambda b, p, i: (b * nt + i, p)),
        out_shape=jax.ShapeDtypeStruct((batch * seq, SB_W), BF16),
        scratch_shapes=[pltpu.VMEM((SB_PAIRS, 2, BLK, LANES), F32),
                        pltpu.VMEM((SB_PAIRS, 2, 2 * BLK, LANES), F32)],
        compiler_params=pltpu.CompilerParams(
            dimension_semantics=("arbitrary", "arbitrary", "arbitrary"), vmem_limit_bytes=VMEM_LIMIT),
        name="sb_attention",
    )(pbf, pbf, pbf)


SW_TQ = 2 * BLK


def _sw_kernel(sinks_ref, slopes_ref, q_ref, kp_ref, kc_ref, vp_ref, vc_ref, o_ref, *, layer):
    ti = pl.program_id(1)
    lane = _iota((BLK, LANES), 1)
    row = _iota((BLK, LANES), 0)
    first = lane < HALF
    cur = lane <= row
    dist = jnp.where(cur, row - lane, row - lane + BLK).astype(F32)
    group = SW_Q_HEADS // SW_KV_HEADS
    for s in range(SW_TQ // BLK):
        rs = slice(s * BLK, (s + 1) * BLK)
        for hk in range(SW_KV_HEADS):
            ks = slice(hk * LANES, (hk + 1) * LANES)
            kc, vc = kc_ref[rs, ks], vc_ref[rs, ks]
            if s == 0:
                kp, vp = kp_ref[:, ks], vp_ref[:, ks]
            else:
                kp, vp = kc_ref[(s - 1) * BLK:s * BLK, ks], vc_ref[(s - 1) * BLK:s * BLK, ks]
            pairs = range(hk * group // 2, (hk + 1) * group // 2)
            lhs = []
            for p in pairs:
                qp = q_ref[rs, p * LANES:(p + 1) * LANES]
                zero = jnp.zeros_like(qp)
                lhs += [jnp.where(first, qp, zero), jnp.where(first, zero, qp)]
            lhs = jnp.concatenate(lhs, axis=0)
            s_cur = _dot_nt(lhs, kc)
            s_prev = _dot_nt(lhs, kp)
            prob_c, prob_p = [], []
            for g in range(group):
                head = hk * group + g
                sink = sinks_ref[layer, head]
                gs = slice(g * BLK, (g + 1) * BLK)
                sc = jnp.where(cur, s_cur[gs], s_prev[gs]) - slopes_ref[head] * dist
                if s == 0:
                    sc = jnp.where(jnp.logical_or(cur, ti > 0), sc, -jnp.inf)
                m = jnp.maximum(jnp.max(sc, axis=-1, keepdims=True), sink)
                e = jnp.exp(sc - m)
                denom = jnp.sum(e, axis=-1, keepdims=True) + jnp.exp(sink - m)
                prob = (e / denom).astype(BF16)
                pz = jnp.zeros_like(prob)
                prob_c.append(jnp.where(cur, prob, pz))
                prob_p.append(jnp.where(cur, pz, prob))
            pv = _dot(jnp.concatenate(prob_c, axis=0), vc) + _dot(jnp.concatenate(prob_p, axis=0), vp)
            for t, p in enumerate(pairs):
                o_ref[rs, p * LANES:(p + 1) * LANES] = jnp.where(
                    first, pv[2 * t * BLK:(2 * t + 1) * BLK], pv[(2 * t + 1) * BLK:(2 * t + 2) * BLK]
                ).astype(o_ref.dtype)


def _sw_attention(pbf, sinks, slopes, layer, batch, seq):
    nt = seq // SW_TQ
    kvw = 2 * SW_KVW
    cq, ck, cv = BF_SWQ // SW_QW, BF_SWK // kvw, BF_SWV // kvw
    per = SW_TQ // BLK

    def prev(b, i):
        return b * nt * per + jnp.maximum(i * per - 1, 0)

    smem = pl.BlockSpec(memory_space=pltpu.SMEM)
    return pl.pallas_call(
        functools.partial(_sw_kernel, layer=layer),
        grid=(batch, nt),
        in_specs=[
            smem, smem,
            pl.BlockSpec((SW_TQ, SW_QW), lambda b, i: (b * nt + i, cq)),
            pl.BlockSpec((BLK, kvw), lambda b, i: (prev(b, i), ck)),
            pl.BlockSpec((SW_TQ, kvw), lambda b, i: (b * nt + i, ck)),
            pl.BlockSpec((BLK, kvw), lambda b, i: (prev(b, i), cv)),
            pl.BlockSpec((SW_TQ, kvw), lambda b, i: (b * nt + i, cv)),
        ],
        out_specs=pl.BlockSpec((SW_TQ, SW_QW), lambda b, i: (b * nt + i, 0)),
        out_shape=jax.ShapeDtypeStruct((batch * seq, SW_QW), BF16),
        compiler_params=pltpu.CompilerParams(
            dimension_semantics=("arbitrary", "arbitrary"), vmem_limit_bytes=VMEM_LIMIT),
        name="sw_attention",
    )(sinks, slopes, pbf, pbf, pbf, pbf, pbf)


GDN_CHUNK = 128


def _split3(x):
    x1 = x.astype(BF16)
    r1 = x - x1.astype(F32)
    x2 = r1.astype(BF16)
    x3 = (r1 - x2.astype(F32)).astype(BF16)
    return x1, x2, x3


def _gdn_kernel(q_ref, k_ref, v_ref, z_ref, ab_ref, conv_ref, small_ref, norm_ref, o_ref,
                carry_ref, state_ref, yq_ref, yk_ref, yv_ref, *, rows):
    n = GDN_CHUNK
    nchunks = rows // n
    step = pl.program_id(1)

    @pl.when(step == 0)
    def _():
        carry_ref[...] = jnp.zeros_like(carry_ref)
        state_ref[...] = jnp.zeros_like(state_ref)

    for idx, (src, dst) in enumerate(((q_ref, yq_ref), (k_ref, yk_ref), (v_ref, yv_ref))):
        cols = slice(idx * GDN_W, (idx + 1) * GDN_W)
        w = conv_ref[:, cols]
        for ci in range(nchunks):
            x = src[ci * n:(ci + 1) * n, :]
            prev8 = carry_ref[:, cols] if ci == 0 else src[ci * n - 8:ci * n, :]
            y = w[GDN_CONV - 1:GDN_CONV, :] * x
            for s in range(1, GDN_CONV):
                y = y + w[GDN_CONV - 1 - s:GDN_CONV - s, :] * _shift_rows(x, prev8, s)
            y = y * jax.nn.sigmoid(y)
            for h in range(GDN_HEADS):
                yh = y[:, h * GDN_HEAD_DIM:(h + 1) * GDN_HEAD_DIM]
                if idx < 2:
                    inv = lax.rsqrt(jnp.sum(yh * yh, axis=-1, keepdims=True) + NORM_EPS)
                    yh = yh * (inv * (GDN_HEAD_DIM ** -0.5) if idx == 0 else inv)
                dst[ci * n:(ci + 1) * n, h * GDN_HEAD_DIM:(h + 1) * GDN_HEAD_DIM] = yh
        carry_ref[:, cols] = src[rows - 8:rows, :]

    r = _iota((n, n), 0)
    c = _iota((n, n), 1)
    causal = r >= c
    strict = r > c
    lower_ones = jnp.where(causal, 1.0, 0.0).astype(BF16)
    head_lane = _iota((n, LANES), 1) < GDN_HEADS

    def sub_blocks(ls):
        return ((r >> (ls + 1)) == (c >> (ls + 1))) & ((r >> ls) != (c >> ls))

    probs = []
    for ci in range(nchunks):
        rs = slice(ci * n, (ci + 1) * n)
        ab = ab_ref[rs, :]
        zab = ab + small_ref[1:2, :]
        softplus = jnp.maximum(zab, 0.0) + jnp.log(1.0 + jnp.exp(-jnp.abs(zab)))
        gbeta = jnp.where(head_lane, -jnp.exp(small_ref[0:1, :]) * softplus, jax.nn.sigmoid(ab))
        gc_all = sum(_dot(lower_ones, part) for part in _split3(gbeta))
        gc_t = gc_all.T
        for h in range(GDN_HEADS):
            hs = slice(h * GDN_HEAD_DIM, (h + 1) * GDN_HEAD_DIM)
            q, k, v = yq_ref[rs, hs], yk_ref[rs, hs], yv_ref[rs, hs]
            gc = jnp.broadcast_to(gc_all[:, h:h + 1], (n, n))
            gc_row = jnp.broadcast_to(gc_t[h:h + 1, :], (n, n))
            g_tot = jnp.broadcast_to(gc[n - 1:n, :], (n, n))
            beta_b = jnp.broadcast_to(gbeta[:, GDN_HEADS + h:GDN_HEADS + h + 1], (n, n))
            decay = jnp.exp(jnp.where(causal, gc - gc_row, -jnp.inf))
            k_beta = k * beta_b
            k16 = k.astype(BF16)
            e_gc = jnp.exp(gc)
            probs.append(dict(
                rs=rs, hs=hs, h=h,
                lower=jnp.where(strict, _dot_nt(k_beta.astype(BF16), k16) * decay, 0.0),
                a=jnp.where(causal, _dot_nt(q.astype(BF16), k16) * decay, 0.0).astype(BF16),
                qd=(q * e_gc).astype(BF16),
                kdt=(k * jnp.exp(g_tot - gc)).T.astype(BF16),
                gl=jnp.exp(g_tot),
                rhs=jnp.concatenate([v * beta_b, k_beta * e_gc], axis=1)))
    es = [-jnp.where((r >> 1) == (c >> 1), p["lower"], 0.0) for p in probs]
    for ls in range(1, n.bit_length() - 1):
        ms = [jnp.where(sub_blocks(ls), p["lower"], 0.0) for p in probs]
        ps = [_dot(e.astype(BF16), m.astype(BF16)) for e, m in zip(es, ms)]
        es = [e - m - (pp + _dot((m + pp).astype(BF16), e.astype(BF16))) for e, m, pp in zip(es, ms, ps)]
    for p, e in zip(probs, es):
        uw = p["rhs"] + _dot(e.astype(BF16), p["rhs"].astype(BF16))
        p["u"] = uw[:, :n]
        p["wq"] = jnp.concatenate([uw[:, n:].astype(BF16), p["qd"]], axis=0)
        p["ak"] = jnp.concatenate([p["a"], p["kdt"]], axis=0)

    gain = norm_ref[...]
    states = [state_ref[h] for h in range(GDN_HEADS)]
    for p in probs:
        h = p["h"]
        sr = _dot(p["wq"], states[h].astype(BF16))
        vn16 = (p["u"] - sr[:n]).astype(BF16)
        vr = _dot(p["ak"], vn16)
        o = sr[n:] + vr[:n]
        states[h] = states[h] * p["gl"] + vr[n:]
        zg = z_ref[p["rs"], p["hs"]]
        o_ref[p["rs"], p["hs"]] = (_rms(o, gain) * (zg * jax.nn.sigmoid(zg))).astype(o_ref.dtype)
    for h in range(GDN_HEADS):
        state_ref[h] = states[h]


def _gdn(pfp, pab, conv_w, small, norm, layer, batch, seq, rows):
    steps = seq // rows
    w = GDN_W
    cq = FP_CQKV // w

    def blk(col):
        return pl.BlockSpec((rows, w), lambda b, i: (b * steps + i, col))

    return pl.pallas_call(
        functools.partial(_gdn_kernel, rows=rows),
        grid=(batch, steps),
        in_specs=[
            blk(cq), blk(cq + 1), blk(cq + 2), blk(FP_CZ // w),
            pl.BlockSpec((rows, LANES), lambda b, i: (b * steps + i, 0)),
            pl.BlockSpec((None, GDN_CONV, 3 * w), lambda b, i: (layer, 0, 0)),
            pl.BlockSpec((None, 8, LANES), lambda b, i: (layer, 0, 0)),
            pl.BlockSpec((None, 1, GDN_HEAD_DIM), lambda b, i: (layer, 0, 0)),
        ],
        out_specs=pl.BlockSpec((rows, w), lambda b, i: (b * steps + i, 0)),
        out_shape=jax.ShapeDtypeStruct((batch * seq, w), BF16),
        scratch_shapes=[
            pltpu.VMEM((8, 3 * w), F32),
            pltpu.VMEM((GDN_HEADS, GDN_HEAD_DIM, GDN_HEAD_DIM), F32),
            pltpu.VMEM((rows, w), F32), pltpu.VMEM((rows, w), F32), pltpu.VMEM((rows, w), F32),
        ],
        compiler_params=pltpu.CompilerParams(
            dimension_semantics=("arbitrary", "arbitrary"), vmem_limit_bytes=VMEM_LIMIT),
        name="gdn",
    )(pfp, pfp, pfp, pfp, pab, conv_w, small, norm)


def _merge_kernel(ya_ref, yb_ref, yc_ref, ga_ref, gb_ref, gc_ref, x_ref,
                  wa_ref, wb_ref, wc_ref, wo_ref, gain_ref, o_ref):
    merged = (jax.nn.sigmoid(ga_ref[...].astype(F32)) * _dot(ya_ref[...], wa_ref[...])
              + jax.nn.sigmoid(gb_ref[...].astype(F32)) * _dot(yb_ref[...], wb_ref[...])
              + jax.nn.sigmoid(gc_ref[...].astype(F32)) * _dot(yc_ref[...], wc_ref[...]))
    r = _dot(merged.astype(BF16), wo_ref[...])
    o_ref[...] = x_ref[...] + _rms(r, gain_ref[...])


def _merge(ya, yb, yc, pbf, x, wa, wb, wc, wo, gain, layer, tm):
    t, d = x.shape

    def rows(width, col=0):
        return pl.BlockSpec((tm, width), lambda i: (i, col))

    def whole(a):
        return pl.BlockSpec((None,) + a.shape[1:], lambda i: (layer, 0, 0))

    return pl.pallas_call(
        _merge_kernel,
        grid=(t // tm,),
        in_specs=[rows(SB_W), rows(SW_QW), rows(GDN_W),
                  rows(d, BF_GA // d), rows(d, BF_GB // d), rows(d, BF_GC // d), rows(d),
                  whole(wa), whole(wb), whole(wc), whole(wo), whole(gain)],
        out_specs=rows(d),
        out_shape=jax.ShapeDtypeStruct((t, d), F32),
        compiler_params=pltpu.CompilerParams(
            dimension_semantics=("arbitrary",), vmem_limit_bytes=VMEM_LIMIT),
        name="merge",
    )(ya, yb, yc, pbf, pbf, pbf, x, wa, wb, wc, wo, gain)


FFN_CK = 256


def _ffn_kernel(x_ref, gpre_ref, wup_ref, conv_ref, wdn_ref, gpost_ref, o_ref, carry_ref, f_ref, *, tm):
    @pl.when(pl.program_id(1) == 0)
    def _():
        carry_ref[...] = jnp.zeros_like(carry_ref)

    x = x_ref[...]
    hn = _rms(x, gpre_ref[...]).astype(BF16)

    def conv(col):
        cs = slice(col, col + FFN_CK)
        hid = _dot(hn, wup_ref[:, cs])
        prev8 = carry_ref[:, cs]
        w = conv_ref[:, cs]
        y = w[FFN_CONV - 1:FFN_CONV, :] * hid
        for s in range(1, FFN_CONV):
            y = y + w[FFN_CONV - 1 - s:FFN_CONV - s, :] * _shift_rows(hid, prev8, s)
        carry_ref[:, cs] = hid[tm - 8:tm, :]
        return y

    for ci in range(D_FF // FFN_CK):
        f_gate = conv(ci * FFN_CK)
        f_up = conv(D_FF + ci * FFN_CK)
        f_ref[:, ci * FFN_CK:(ci + 1) * FFN_CK] = (jax.nn.gelu(f_gate, approximate=True) * f_up).astype(BF16)
    o_ref[...] = x + _rms(_dot(f_ref[...], wdn_ref[...]), gpost_ref[...])


def _ffn(x, gpre, wup, conv_w, wdn, gpost, layer, batch, seq, tm):
    steps = seq // tm
    d = x.shape[1]

    def whole(a):
        return pl.BlockSpec((None,) + a.shape[1:], lambda b, i: (layer, 0, 0))

    return pl.pallas_call(
        functools.partial(_ffn_kernel, tm=tm),
        grid=(batch, steps),
        in_specs=[pl.BlockSpec((tm, d), lambda b, i: (b * steps + i, 0)),
                  whole(gpre), whole(wup), whole(conv_w), whole(wdn), whole(gpost)],
        out_specs=pl.BlockSpec((tm, d), lambda b, i: (b * steps + i, 0)),
        out_shape=jax.ShapeDtypeStruct(x.shape, F32),
        scratch_shapes=[pltpu.VMEM((8, 2 * D_FF), F32), pltpu.VMEM((tm, D_FF), BF16)],
        compiler_params=pltpu.CompilerParams(
            dimension_semantics=("arbitrary", "arbitrary"), vmem_limit_bytes=VMEM_LIMIT),
        name="ffn",
    )(x, gpre, wup, conv_w, wdn, gpost)


def _pack_w_in(w):
    w = w.astype(BF16)
    o = 0
    parts = {}
    for name, width in (("aq", SB_W), ("ak", SB_W), ("av", SB_W), ("bq", SW_QW), ("bk", SW_KVW),
                        ("bv", SW_KVW), ("cqkv", 3 * GDN_W), ("cz", GDN_W), ("ca", GDN_HEADS),
                        ("cb", GDN_HEADS), ("ga", D_MODEL), ("gb", D_MODEL), ("gc", D_MODEL)):
        parts[name] = w[..., o:o + width]
        o += width

    def dup(t):
        return jnp.concatenate([t[..., h * HALF:(h + 1) * HALF] for h in range(SW_KV_HEADS) for _ in range(2)], axis=-1)

    cols = [parts["ga"], parts["gb"], parts["gc"],
            parts["aq"] * (SB_HEAD_DIM ** -0.5), parts["ak"], parts["av"],
            parts["bq"] * (SW_HEAD_DIM ** -0.5), dup(parts["bk"]), dup(parts["bv"]),
            parts["cqkv"], parts["cz"]]
    pad = jnp.zeros(w.shape[:-1] + (LANES - 2 * GDN_HEADS,), w.dtype)
    wab = jnp.concatenate([parts["ca"], parts["cb"], pad], axis=-1)
    return jnp.concatenate(cols, axis=-1).astype(BF16), wab


def _layer(x, batch, seq, layer, p):
    pbf, pfp, pab = _inproj(x, p["ln_mix_pre"], p["w_in"], p["w_ab"], layer, tm=2048)
    ya = _sb_attention(pbf, batch, seq)
    yb = _sw_attention(pbf, p["sinks"], p["slopes"], layer, batch, seq)
    yc = _gdn(pfp, pab, p["gdn_conv"], p["gdn_small"], p["gdn_norm"], layer, batch, seq, rows=512)
    x = _merge(ya, yb, yc, pbf, x, p["wa"], p["wb"], p["wc"], p["wo"], p["ln_mix_post"], layer, tm=512)
    return _ffn(x, p["ln_ffn_pre"], p["w_up"], p["ffn_conv"], p["w_down"], p["ln_ffn_post"],
                layer, batch, seq, tm=512)


def kernel(x, ln_mix_pre, w_in, sw_sinks, gdn_conv, gdn_a_log, gdn_dt_bias, gdn_norm, w_branch_a,
           w_branch_b, w_branch_c, w_out, ln_mix_post, ln_ffn_pre, w_up, ffn_conv, w_down, ln_ffn_post):
    batch, seq, d = x.shape
    depth = w_in.shape[0]
    small = jnp.zeros((depth, 8, LANES), F32)
    small = small.at[:, 0, :GDN_HEADS].set(gdn_a_log).at[:, 1, :GDN_HEADS].set(gdn_dt_bias)
    w_main, w_ab = _pack_w_in(w_in)
    p = {
        "ln_mix_pre": ln_mix_pre[:, None, :], "w_in": w_main, "w_ab": w_ab,
        "sinks": sw_sinks, "slopes": jnp.exp2(-8.0 * jnp.arange(1, SW_Q_HEADS + 1, dtype=F32) / SW_Q_HEADS),
        "gdn_conv": gdn_conv, "gdn_small": small, "gdn_norm": gdn_norm[:, None, :],
        "wa": w_branch_a.astype(BF16), "wb": w_branch_b.astype(BF16),
        "wc": w_branch_c.astype(BF16), "wo": w_out.astype(BF16),
        "ln_mix_post": ln_mix_post[:, None, :], "ln_ffn_pre": ln_ffn_pre[:, None, :],
        "w_up": w_up.astype(BF16), "ffn_conv": ffn_conv, "w_down": w_down.astype(BF16),
        "ln_ffn_post": ln_ffn_post[:, None, :],
    }
    h = x.reshape(batch * seq, d)
    for layer in range(depth):
        h = _layer(h, batch, seq, layer, p)
    return h.reshape(batch, seq, d)
```

```python
import functools

import jax
import jax.numpy as jnp
from jax import lax
from jax.experimental import pallas as pl
from jax.experimental.pallas import tpu as pltpu

F32 = jnp.float32
BF16 = jnp.bfloat16
NORM_EPS = 1e-6

D_MODEL = 1024
SB_HEADS, SB_HEAD_DIM = 8, 64
SW_Q_HEADS, SW_KV_HEADS, SW_HEAD_DIM = 8, 2, 64
GDN_HEADS, GDN_HEAD_DIM, GDN_CONV = 4, 128, 4
D_FF, FFN_CONV = 2816, 3
SB_W = SB_HEADS * SB_HEAD_DIM
SW_QW = SW_Q_HEADS * SW_HEAD_DIM
SW_KVW = SW_KV_HEADS * SW_HEAD_DIM
GDN_W = GDN_HEADS * GDN_HEAD_DIM

BLK = 128
LANES = 128
HALF = 64
VMEM_LIMIT = 56 * 1024 * 1024

BF_GA, BF_GB, BF_GC, BF_SBQ, BF_SBK, BF_SBV, BF_SWQ, BF_SWK, BF_SWV, BF_COLS = (
    0, 1024, 2048, 3072, 3584, 4096, 4608, 5120, 5376, 5632)
FP_CQKV, FP_CZ, FP_COLS = 0, 1536, 2048
PROJ_TN = 512
LOG2E = 1.4426950408889634
F32_EXP2_ZERO = -104.0 * LOG2E


def _dot(a, b):
    return jnp.dot(a, b, preferred_element_type=F32)


def _dot_nt(a, b):
    return lax.dot_general(a, b, (((1,), (1,)), ((), ())), preferred_element_type=F32)


def _iota(shape, dim):
    return lax.broadcasted_iota(jnp.int32, shape, dim)


def _rms(t, gain):
    return t * lax.rsqrt(jnp.mean(t * t, axis=-1, keepdims=True) + NORM_EPS) * gain


def _shift_rows(h, prev8, s):
    r = pltpu.roll(h, s, axis=0)
    row = _iota(h.shape, 0)
    for t in range(s):
        r = jnp.where(row == t, prev8[8 - s + t:8 - s + t + 1, :], r)
    return r


def _inproj_kernel(x_ref, g_ref, w_ref, wab_ref, obf_ref, of_ref, oab_ref, xn_ref, *, nbf):
    j = pl.program_id(1)

    @pl.when(j == 0)
    def _():
        xn = _rms(x_ref[...], g_ref[...]).astype(BF16)
        xn_ref[...] = xn
        oab_ref[...] = _dot(xn, wab_ref[...])

    @pl.when(j < nbf)
    def _():
        obf_ref[...] = _dot(xn_ref[...], w_ref[...]).astype(BF16)

    @pl.when(j >= nbf)
    def _():
        of_ref[...] = _dot(xn_ref[...], w_ref[...])


def _inproj(x, gain, w, wab, layer, tm):
    t, d = x.shape
    tn = PROJ_TN
    nbf, nfp = BF_COLS // tn, FP_COLS // tn
    return pl.pallas_call(
        functools.partial(_inproj_kernel, nbf=nbf),
        grid=(t // tm, nbf + nfp),
        in_specs=[
            pl.BlockSpec((tm, d), lambda i, j: (i, 0)),
            pl.BlockSpec((None, 1, d), lambda i, j: (layer, 0, 0)),
            pl.BlockSpec((None, d, tn), lambda i, j: (layer, 0, j)),
            pl.BlockSpec((None, d, LANES), lambda i, j: (layer, 0, 0)),
        ],
        out_specs=[
            pl.BlockSpec((tm, tn), lambda i, j: (i, jnp.minimum(j, nbf - 1))),
            pl.BlockSpec((tm, tn), lambda i, j: (i, jnp.maximum(j - nbf, 0))),
            pl.BlockSpec((tm, LANES), lambda i, j: (i, 0)),
        ],
        out_shape=[jax.ShapeDtypeStruct((t, BF_COLS), BF16), jax.ShapeDtypeStruct((t, FP_COLS), F32),
                   jax.ShapeDtypeStruct((t, LANES), F32)],
        scratch_shapes=[pltpu.VMEM((tm, d), BF16)],
        compiler_params=pltpu.CompilerParams(
            dimension_semantics=("arbitrary", "arbitrary"), vmem_limit_bytes=VMEM_LIMIT),
        name="inproj",
    )(x, gain, w, wab)


SB_TQ = 2 * BLK
SB_PAIRS = 2


def _sb_kernel(q_ref, k_ref, v_ref, o_ref, acc_ref, lr_ref):
    ti = pl.program_id(2)
    lane = _iota((BLK, LANES), 1)
    row = _iota((BLK, LANES), 0)
    first = lane < HALF
    tri = lane < row
    full = lane >= 0
    zero_tile = jnp.zeros((BLK, LANES), F32)
    r2 = _iota((BLK, 2 * LANES), 0)
    c2 = _iota((BLK, 2 * LANES), 1)
    suffix_ones = jnp.where((r2 > c2) | (c2 >= LANES), 1.0, 0.0).astype(BF16)
    suffix_ones = jnp.concatenate([suffix_ones, suffix_ones], axis=0)

    def split_heads(q):
        sel = jnp.concatenate([first] * (q.shape[0] // BLK), axis=0)
        zero = jnp.zeros_like(q)
        return jnp.where(sel, q, zero), jnp.where(sel, zero, q)

    def log_parts(z):
        zs = z * LOG2E
        log_stay = -(jnp.maximum(zs, 0.0) + jnp.log2(1.0 + jnp.exp2(-jnp.abs(zs))))
        return log_stay, log_stay + zs

    def suffix_sums(parts):
        hi = [x.astype(BF16) for x in parts]
        lo = [(x - h.astype(F32)).astype(BF16) for x, h in zip(parts, hi)]
        hilo = jnp.concatenate([jnp.concatenate(hi, axis=0), jnp.concatenate(lo, axis=0)], axis=1)
        sums = _dot(hilo, suffix_ones)
        out, o = [], 0
        for x in parts:
            out.append((sums[o:o + x.shape[0], :LANES], sums[o:o + x.shape[0], LANES:]))
            o += x.shape[0]
        return out

    def generic_block(pp, s, j, valid):
        ps = slice(pp * LANES, (pp + 1) * LANES)
        q0, q1 = split_heads(q_ref[s * BLK:(s + 1) * BLK, ps])
        off = pl.multiple_of(j * BLK, BLK)
        k, v = k_ref[pl.ds(off, BLK), ps], v_ref[pl.ds(off, BLK), ps]
        log_stay, log_take = log_parts(_dot_nt(jnp.concatenate([q0, q1], axis=0), k))
        if valid is not None:
            vm = jnp.concatenate([valid, valid], axis=0)
            log_stay = jnp.where(vm, log_stay, 0.0)
        (later, total), = suffix_sums([log_stay])
        w = jnp.exp2(log_take + later + lr_ref[pp, s])
        if valid is not None:
            w = jnp.where(vm, w, 0.0)
        pv = _dot(w.astype(BF16), v)
        acc_ref[pp, s] += jnp.where(first, pv[:BLK], pv[BLK:])
        lr_ref[pp, s] += total

    def tail(pp, s, j0):
        def cond(c):
            j, live = c
            return jnp.logical_and(j >= 0, live)

        def body(c):
            j, _ = c
            generic_block(pp, s, j, None)
            return j - 1, jnp.max(lr_ref[pp, s]) > F32_EXP2_ZERO

        lax.while_loop(cond, body, (j0, jnp.max(lr_ref[pp, s]) > F32_EXP2_ZERO))

    @pl.when(ti == 0)
    def _():
        for pp in range(SB_PAIRS):
            for s in range(2):
                acc_ref[pp, s] = jnp.zeros((BLK, LANES), F32)
                lr_ref[pp, s] = jnp.zeros((2 * BLK, LANES), F32)
                generic_block(pp, s, 2 * ti + s, tri)
                tail(pp, s, 2 * ti + s - 1)

    @pl.when(ti > 0)
    def _():
        base = pl.multiple_of((2 * ti - 2) * BLK, BLK)
        m2 = jnp.concatenate([tri, full, tri, full], axis=0)
        m3 = jnp.concatenate([tri, tri], axis=0)
        for pp in range(SB_PAIRS):
            ps = slice(pp * LANES, (pp + 1) * LANES)
            q0, q1 = split_heads(q_ref[:, ps])
            kw = k_ref[pl.ds(base, 4 * BLK), ps]
            vw = v_ref[pl.ds(base, 4 * BLK), ps]
            kb = [kw[t * BLK:(t + 1) * BLK] for t in range(4)]
            l_ab = jnp.concatenate([q0, q1], axis=0)
            l_a = jnp.concatenate([q0[:BLK], q1[:BLK]], axis=0)
            l_b = jnp.concatenate([q0[BLK:], q1[BLK:]], axis=0)
            ls0, lt0 = log_parts(_dot_nt(l_a, kb[0]))
            ls1, lt1 = log_parts(_dot_nt(l_ab, kb[1]))
            ls2, lt2 = log_parts(_dot_nt(l_ab, kb[2]))
            ls3, lt3 = log_parts(_dot_nt(l_b, kb[3]))
            ls2 = jnp.where(m2, ls2, 0.0)
            ls3 = jnp.where(m3, ls3, 0.0)
            (lat0, tot0), (lat1, tot1), (lat2, tot2), (lat3, tot3) = suffix_sums([ls0, ls1, ls2, ls3])
            lrb2 = jnp.concatenate([zero_tile, tot3[:BLK], zero_tile, tot3[BLK:]], axis=0)
            lrb1 = lrb2 + tot2
            lre1 = lrb1 + tot1
            lrb0 = jnp.concatenate([lre1[:BLK], lre1[2 * BLK:3 * BLK]], axis=0)
            w3 = jnp.where(m3, jnp.exp2(lt3 + lat3), 0.0)
            w2 = jnp.where(m2, jnp.exp2(lt2 + lat2 + lrb2), 0.0)
            w1 = jnp.exp2(lt1 + lat1 + lrb1)
            w0 = jnp.exp2(lt0 + lat0 + lrb0)
            w12 = jnp.concatenate([w1.astype(BF16), w2.astype(BF16)], axis=1)
            pv12 = _dot(w12, vw[BLK:3 * BLK])
            pv0 = _dot(w0.astype(BF16), vw[:BLK])
            pv3 = _dot(w3.astype(BF16), vw[3 * BLK:])
            acc_ref[pp, 0] = jnp.where(first, pv12[:BLK] + pv0[:BLK], pv12[2 * BLK:3 * BLK] + pv0[BLK:])
            acc_ref[pp, 1] = jnp.where(first, pv12[BLK:2 * BLK] + pv3[:BLK], pv12[3 * BLK:] + pv3[BLK:])
            lr_ref[pp, 0] = lrb0 + tot0
            lr_ref[pp, 1] = jnp.concatenate([lre1[BLK:2 * BLK], lre1[3 * BLK:]], axis=0)

        @pl.when(jnp.max(lr_ref[...]) > F32_EXP2_ZERO)
        def _():
            for pp in range(SB_PAIRS):
                tail(pp, 0, 2 * ti - 3)
                tail(pp, 1, 2 * ti - 2)

    for pp in range(SB_PAIRS):
        for s in range(2):
            o_ref[s * BLK:(s + 1) * BLK, pp * LANES:(pp + 1) * LANES] = acc_ref[pp, s].astype(o_ref.dtype)


def _sb_attention(pbf, batch, seq):
    nt = seq // SB_TQ
    w = SB_PAIRS * LANES
    cq, ck, cv = BF_SBQ // w, BF_SBK // w, BF_SBV // w
    return pl.pallas_call(
        _sb_kernel,
        grid=(batch, SB_W // w, nt),
        in_specs=[
            pl.BlockSpec((SB_TQ, w), lambda b, p, i: (b * nt + i, cq + p)),
            pl.BlockSpec((seq, w), lambda b, p, i: (b, ck + p)),
            pl.BlockSpec((seq, w), lambda b, p, i: (b, cv + p)),
        ],
        out_specs=pl.BlockSpec((SB_TQ, w), lambda b, p, i: (b * nt + i, p)),
        out_shape=jax.ShapeDtypeStruct((batch * seq, SB_W), BF16),
        scratch_shapes=[pltpu.VMEM((SB_PAIRS, 2, BLK, LANES), F32),
                        pltpu.VMEM((SB_PAIRS, 2, 2 * BLK, LANES), F32)],
        compiler_params=pltpu.CompilerParams(
            dimension_semantics=("arbitrary", "arbitrary", "arbitrary"), vmem_limit_bytes=VMEM_LIMIT),
        name="sb_attention",
    )(pbf, pbf, pbf)


SW_TQ = 2 * BLK


def _sw_kernel(sinks_ref, slopes_ref, q_ref, kp_ref, kc_ref, vp_ref, vc_ref, o_ref, *, layer):
    ti = pl.program_id(1)
    lane = _iota((BLK, LANES), 1)
    row = _iota((BLK, LANES), 0)
    first = lane < HALF
    cur = lane <= row
    dist = jnp.where(cur, row - lane, row - lane + BLK).astype(F32)
    group = SW_Q_HEADS // SW_KV_HEADS
    units = []
    for s in range(SW_TQ // BLK):
        rs = slice(s * BLK, (s + 1) * BLK)
        for hk in range(SW_KV_HEADS):
            ks = slice(hk * LANES, (hk + 1) * LANES)
            kc, vc = kc_ref[rs, ks], vc_ref[rs, ks]
            if s == 0:
                kp, vp = kp_ref[:, ks], vp_ref[:, ks]
            else:
                kp, vp = kc_ref[(s - 1) * BLK:s * BLK, ks], vc_ref[(s - 1) * BLK:s * BLK, ks]
            pairs = range(hk * group // 2, (hk + 1) * group // 2)
            lhs = []
            for p in pairs:
                qp = q_ref[rs, p * LANES:(p + 1) * LANES]
                zero = jnp.zeros_like(qp)
                lhs += [jnp.where(first, qp, zero), jnp.where(first, zero, qp)]
            lhs = jnp.concatenate(lhs, axis=0)
            units.append(dict(s=s, rs=rs, hk=hk, pairs=pairs, vc=vc, vp=vp,
                              s_cur=_dot_nt(lhs, kc), s_prev=_dot_nt(lhs, kp)))
    for u in units:
        scores = []
        for g in range(group):
            head = u["hk"] * group + g
            gs = slice(g * BLK, (g + 1) * BLK)
            sc = jnp.where(cur, u["s_cur"][gs], u["s_prev"][gs]) - slopes_ref[head] * dist
            if u["s"] == 0:
                sc = jnp.where(jnp.logical_or(cur, ti > 0), sc, -jnp.inf)
            scores.append(sc)
        u["scores"] = scores
        u["m"] = [jnp.maximum(jnp.max(sc, axis=-1, keepdims=True), sinks_ref[layer, u["hk"] * group + g])
                  for g, sc in enumerate(scores)]
    for u in units:
        u["e"] = [jnp.exp(sc - m) for sc, m in zip(u["scores"], u["m"])]
        u["denom"] = [jnp.sum(e, axis=-1, keepdims=True) + jnp.exp(sinks_ref[layer, u["hk"] * group + g] - m)
                      for g, (e, m) in enumerate(zip(u["e"], u["m"]))]
    for u in units:
        prob = [(e / d).astype(BF16) for e, d in zip(u["e"], u["denom"])]
        pz = jnp.zeros_like(prob[0])
        prob_c = jnp.concatenate([jnp.where(cur, p, pz) for p in prob], axis=0)
        prob_p = jnp.concatenate([jnp.where(cur, pz, p) for p in prob], axis=0)
        pv = _dot(prob_c, u["vc"]) + _dot(prob_p, u["vp"])
        for t, p in enumerate(u["pairs"]):
            o_ref[u["rs"], p * LANES:(p + 1) * LANES] = jnp.where(
                first, pv[2 * t * BLK:(2 * t + 1) * BLK], pv[(2 * t + 1) * BLK:(2 * t + 2) * BLK]
            ).astype(o_ref.dtype)


def _sw_attention(pbf, sinks, slopes, layer, batch, seq):
    nt = seq // SW_TQ
    kvw = 2 * SW_KVW
    cq, ck, cv = BF_SWQ // SW_QW, BF_SWK // kvw, BF_SWV // kvw
    per = SW_TQ // BLK

    def prev(b, i):
        return b * nt * per + jnp.maximum(i * per - 1, 0)

    smem = pl.BlockSpec(memory_space=pltpu.SMEM)
    return pl.pallas_call(
        functools.partial(_sw_kernel, layer=layer),
        grid=(batch, nt),
        in_specs=[
            smem, smem,
            pl.BlockSpec((SW_TQ, SW_QW), lambda b, i: (b * nt + i, cq)),
            pl.BlockSpec((BLK, kvw), lambda b, i: (prev(b, i), ck)),
            pl.BlockSpec((SW_TQ, kvw), lambda b, i: (b * nt + i, ck)),
            pl.BlockSpec((BLK, kvw), lambda b, i: (prev(b, i), cv)),
            pl.BlockSpec((SW_TQ, kvw), lambda b, i: (b * nt + i, cv)),
        ],
        out_specs=pl.BlockSpec((SW_TQ, SW_QW), lambda b, i: (b * nt + i, 0)),
        out_shape=jax.ShapeDtypeStruct((batch * seq, SW_QW), BF16),
        compiler_params=pltpu.CompilerParams(
            dimension_semantics=("arbitrary", "arbitrary"), vmem_limit_bytes=VMEM_LIMIT),
        name="sw_attention",
    )(sinks, slopes, pbf, pbf, pbf, pbf, pbf)


GDN_CHUNK = 128


def _split3(x):
    x1 = x.astype(BF16)
    r1 = x - x1.astype(F32)
    x2 = r1.astype(BF16)
    x3 = (r1 - x2.astype(F32)).astype(BF16)
    return x1, x2, x3


def _gdn_kernel(q_ref, k_ref, v_ref, z_ref, ab_ref, conv_ref, small_ref, norm_ref, o_ref,
                carry_ref, state_ref, yq_ref, yk_ref, yv_ref, *, rows):
    n = GDN_CHUNK
    nchunks = rows // n
    step = pl.program_id(1)

    @pl.when(step == 0)
    def _():
        carry_ref[...] = jnp.zeros_like(carry_ref)
        state_ref[...] = jnp.zeros_like(state_ref)

    for idx, (src, dst) in enumerate(((q_ref, yq_ref), (k_ref, yk_ref), (v_ref, yv_ref))):
        cols = slice(idx * GDN_W, (idx + 1) * GDN_W)
        w = conv_ref[:, cols]
        for ci in range(nchunks):
            x = src[ci * n:(ci + 1) * n, :]
            prev8 = carry_ref[:, cols] if ci == 0 else src[ci * n - 8:ci * n, :]
            y = w[GDN_CONV - 1:GDN_CONV, :] * x
            for s in range(1, GDN_CONV):
                y = y + w[GDN_CONV - 1 - s:GDN_CONV - s, :] * _shift_rows(x, prev8, s)
            y = y * jax.nn.sigmoid(y)
            for h in range(GDN_HEADS):
                yh = y[:, h * GDN_HEAD_DIM:(h + 1) * GDN_HEAD_DIM]
                if idx < 2:
                    inv = lax.rsqrt(jnp.sum(yh * yh, axis=-1, keepdims=True) + NORM_EPS)
                    yh = yh * (inv * (GDN_HEAD_DIM ** -0.5) if idx == 0 else inv)
                dst[ci * n:(ci + 1) * n, h * GDN_HEAD_DIM:(h + 1) * GDN_HEAD_DIM] = yh
        carry_ref[:, cols] = src[rows - 8:rows, :]

    r = _iota((n, n), 0)
    c = _iota((n, n), 1)
    causal = r >= c
    strict = r > c
    lower_ones = jnp.where(causal, 1.0, 0.0).astype(BF16)
    head_lane = _iota((n, LANES), 1) < GDN_HEADS

    def sub_blocks(ls):
        return ((r >> (ls + 1)) == (c >> (ls + 1))) & ((r >> ls) != (c >> ls))

    probs = []
    for ci in range(nchunks):
        rs = slice(ci * n, (ci + 1) * n)
        ab = ab_ref[rs, :]
        zab = ab + small_ref[1:2, :]
        softplus = jnp.maximum(zab, 0.0) + jnp.log(1.0 + jnp.exp(-jnp.abs(zab)))
        gbeta = jnp.where(head_lane, -jnp.exp(small_ref[0:1, :]) * softplus, jax.nn.sigmoid(ab))
        gc_all = sum(_dot(lower_ones, part) for part in _split3(gbeta))
        gc_t = gc_all.T
        for h in range(GDN_HEADS):
            hs = slice(h * GDN_HEAD_DIM, (h + 1) * GDN_HEAD_DIM)
            q, k, v = yq_ref[rs, hs], yk_ref[rs, hs], yv_ref[rs, hs]
            gc = jnp.broadcast_to(gc_all[:, h:h + 1], (n, n))
            gc_row = jnp.broadcast_to(gc_t[h:h + 1, :], (n, n))
            g_tot = jnp.broadcast_to(gc[n - 1:n, :], (n, n))
            beta_b = jnp.broadcast_to(gbeta[:, GDN_HEADS + h:GDN_HEADS + h + 1], (n, n))
            decay = jnp.exp(gc - gc_row)
            k_beta = k * beta_b
            k16 = k.astype(BF16)
            e_gc = jnp.exp(gc)
            probs.append(dict(
                rs=rs, hs=hs, h=h,
                lower=jnp.where(strict, _dot_nt(k_beta.astype(BF16), k16) * decay, 0.0),
                a=jnp.where(causal, _dot_nt(q.astype(BF16), k16) * decay, 0.0).astype(BF16),
                qd=(q * e_gc).astype(BF16),
                kdt=(k * jnp.exp(g_tot - gc)).T.astype(BF16),
                gl=jnp.exp(g_tot),
                rhs=jnp.concatenate([v * beta_b, k_beta * e_gc], axis=1)))
    eye = jnp.where(r == c, 1.0, 0.0).astype(F32)
    zero16 = jnp.zeros((n, n), BF16)
    lower16 = [p["lower"].astype(BF16) for p in probs]
    xs = [eye - jnp.where((r >> 1) == (c >> 1), p["lower"], 0.0) for p in probs]
    for ls in range(1, n.bit_length() - 1):
        x16 = [x.astype(BF16) for x in xs]
        ys = [_dot(xb, jnp.where(sub_blocks(ls), l16, zero16)).astype(BF16) for xb, l16 in zip(x16, lower16)]
        xs = [x - _dot(y, xb) for x, y, xb in zip(xs, ys, x16)]
    for p, x in zip(probs, xs):
        uw = _dot(x.astype(BF16), p["rhs"].astype(BF16))
        p["u"] = uw[:, :n]
        p["wq"] = jnp.concatenate([uw[:, n:].astype(BF16), p["qd"]], axis=0)
        p["ak"] = jnp.concatenate([p["a"], p["kdt"]], axis=0)

    gain = norm_ref[...]
    states = [state_ref[h] for h in range(GDN_HEADS)]
    for p in probs:
        h = p["h"]
        sr = _dot(p["wq"], states[h].astype(BF16))
        vn16 = (p["u"] - sr[:n]).astype(BF16)
        vr = _dot(p["ak"], vn16)
        o = sr[n:] + vr[:n]
        states[h] = states[h] * p["gl"] + vr[n:]
        zg = z_ref[p["rs"], p["hs"]]
        o_ref[p["rs"], p["hs"]] = (_rms(o, gain) * (zg * jax.nn.sigmoid(zg))).astype(o_ref.dtype)
    for h in range(GDN_HEADS):
        state_ref[h] = states[h]


def _gdn(pfp, pab, conv_w, small, norm, layer, batch, seq, rows):
    steps = seq // rows
    w = GDN_W
    cq = FP_CQKV // w

    def blk(col):
        return pl.BlockSpec((rows, w), lambda b, i: (b * steps + i, col))

    return pl.pallas_call(
        functools.partial(_gdn_kernel, rows=rows),
        grid=(batch, steps),
        in_specs=[
            blk(cq), blk(cq + 1), blk(cq + 2), blk(FP_CZ // w),
            pl.BlockSpec((rows, LANES), lambda b, i: (b * steps + i, 0)),
            pl.BlockSpec((None, GDN_CONV, 3 * w), lambda b, i: (layer, 0, 0)),
            pl.BlockSpec((None, 8, LANES), lambda b, i: (layer, 0, 0)),
            pl.BlockSpec((None, 1, GDN_HEAD_DIM), lambda b, i: (layer, 0, 0)),
        ],
        out_specs=pl.BlockSpec((rows, w), lambda b, i: (b * steps + i, 0)),
        out_shape=jax.ShapeDtypeStruct((batch * seq, w), BF16),
        scratch_shapes=[
            pltpu.VMEM((8, 3 * w), F32),
            pltpu.VMEM((GDN_HEADS, GDN_HEAD_DIM, GDN_HEAD_DIM), F32),
            pltpu.VMEM((rows, w), F32), pltpu.VMEM((rows, w), F32), pltpu.VMEM((rows, w), F32),
        ],
        compiler_params=pltpu.CompilerParams(
            dimension_semantics=("arbitrary", "arbitrary"), vmem_limit_bytes=VMEM_LIMIT),
        name="gdn",
    )(pfp, pfp, pfp, pfp, pab, conv_w, small, norm)


def _merge_kernel(ya_ref, yb_ref, yc_ref, ga_ref, gb_ref, gc_ref, x_ref,
                  wa_ref, wb_ref, wc_ref, wo_ref, gain_ref, o_ref):
    merged = (jax.nn.sigmoid(ga_ref[...].astype(F32)) * _dot(ya_ref[...], wa_ref[...])
              + jax.nn.sigmoid(gb_ref[...].astype(F32)) * _dot(yb_ref[...], wb_ref[...])
              + jax.nn.sigmoid(gc_ref[...].astype(F32)) * _dot(yc_ref[...], wc_ref[...]))
    r = _dot(merged.astype(BF16), wo_ref[...])
    o_ref[...] = x_ref[...] + _rms(r, gain_ref[...])


def _merge(ya, yb, yc, pbf, x, wa, wb, wc, wo, gain, layer, tm):
    t, d = x.shape

    def rows(width, col=0):
        return pl.BlockSpec((tm, width), lambda i: (i, col))

    def whole(a):
        return pl.BlockSpec((None,) + a.shape[1:], lambda i: (layer, 0, 0))

    return pl.pallas_call(
        _merge_kernel,
        grid=(t // tm,),
        in_specs=[rows(SB_W), rows(SW_QW), rows(GDN_W),
                  rows(d, BF_GA // d), rows(d, BF_GB // d), rows(d, BF_GC // d), rows(d),
                  whole(wa), whole(wb), whole(wc), whole(wo), whole(gain)],
        out_specs=rows(d),
        out_shape=jax.ShapeDtypeStruct((t, d), F32),
        compiler_params=pltpu.CompilerParams(
            dimension_semantics=("arbitrary",), vmem_limit_bytes=VMEM_LIMIT),
        name="merge",
    )(ya, yb, yc, pbf, pbf, pbf, x, wa, wb, wc, wo, gain)


FFN_CK = 256
SQRT_2_OVER_PI = 0.7978845608028654


def _gelu_tanh(x):
    inner = x * (SQRT_2_OVER_PI + (SQRT_2_OVER_PI * 0.044715) * (x * x))
    return (0.5 * x) * (1.0 + jnp.tanh(inner))


def _ffn_kernel(x_ref, gpre_ref, wup_ref, conv_ref, wdn_ref, gpost_ref, o_ref, carry_ref, f_ref, *, tm):
    @pl.when(pl.program_id(1) == 0)
    def _():
        carry_ref[...] = jnp.zeros_like(carry_ref)

    x = x_ref[...]
    hn = _rms(x, gpre_ref[...]).astype(BF16)

    def conv(col):
        cs = slice(col, col + FFN_CK)
        hid = _dot(hn, wup_ref[:, cs])
        prev8 = carry_ref[:, cs]
        w = conv_ref[:, cs]
        y = w[FFN_CONV - 1:FFN_CONV, :] * hid
        for s in range(1, FFN_CONV):
            y = y + w[FFN_CONV - 1 - s:FFN_CONV - s, :] * _shift_rows(hid, prev8, s)
        carry_ref[:, cs] = hid[tm - 8:tm, :]
        return y

    for ci in range(D_FF // FFN_CK):
        f_gate = conv(ci * FFN_CK)
        f_up = conv(D_FF + ci * FFN_CK)
        f_ref[:, ci * FFN_CK:(ci + 1) * FFN_CK] = (_gelu_tanh(f_gate) * f_up).astype(BF16)
    o_ref[...] = x + _rms(_dot(f_ref[...], wdn_ref[...]), gpost_ref[...])


def _ffn(x, gpre, wup, conv_w, wdn, gpost, layer, batch, seq, tm):
    steps = seq // tm
    d = x.shape[1]

    def whole(a):
        return pl.BlockSpec((None,) + a.shape[1:], lambda b, i: (layer, 0, 0))

    return pl.pallas_call(
        functools.partial(_ffn_kernel, tm=tm),
        grid=(batch, steps),
        in_specs=[pl.BlockSpec((tm, d), lambda b, i: (b * steps + i, 0)),
                  whole(gpre), whole(wup), whole(conv_w), whole(wdn), whole(gpost)],
        out_specs=pl.BlockSpec((tm, d), lambda b, i: (b * steps + i, 0)),
        out_shape=jax.ShapeDtypeStruct(x.shape, F32),
        scratch_shapes=[pltpu.VMEM((8, 2 * D_FF), F32), pltpu.VMEM((tm, D_FF), BF16)],
        compiler_params=pltpu.CompilerParams(
            dimension_semantics=("arbitrary", "arbitrary"), vmem_limit_bytes=VMEM_LIMIT),
        name="ffn",
    )(x, gpre, wup, conv_w, wdn, gpost)


def _pack_w_in(w):
    w = w.astype(BF16)
    o = 0
    parts = {}
    for name, width in (("aq", SB_W), ("ak", SB_W), ("av", SB_W), ("bq", SW_QW), ("bk", SW_KVW),
                        ("bv", SW_KVW), ("cqkv", 3 * GDN_W), ("cz", GDN_W), ("ca", GDN_HEADS),
                        ("cb", GDN_HEADS), ("ga", D_MODEL), ("gb", D_MODEL), ("gc", D_MODEL)):
        parts[name] = w[..., o:o + width]
        o += width

    def dup(t):
        return jnp.concatenate([t[..., h * HALF:(h + 1) * HALF] for h in range(SW_KV_HEADS) for _ in range(2)], axis=-1)

    cols = [parts["ga"], parts["gb"], parts["gc"],
            parts["aq"] * (SB_HEAD_DIM ** -0.5), parts["ak"], parts["av"],
            parts["bq"] * (SW_HEAD_DIM ** -0.5), dup(parts["bk"]), dup(parts["bv"]),
            parts["cqkv"], parts["cz"]]
    pad = jnp.zeros(w.shape[:-1] + (LANES - 2 * GDN_HEADS,), w.dtype)
    wab = jnp.concatenate([parts["ca"], parts["cb"], pad], axis=-1)
    return jnp.concatenate(cols, axis=-1).astype(BF16), wab


def _layer(x, batch, seq, layer, p):
    pbf, pfp, pab = _inproj(x, p["ln_mix_pre"], p["w_in"], p["w_ab"], layer, tm=2048)
    ya = _sb_attention(pbf, batch, seq)
    yb = _sw_attention(pbf, p["sinks"], p["slopes"], layer, batch, seq)
    yc = _gdn(pfp, pab, p["gdn_conv"], p["gdn_small"], p["gdn_norm"], layer, batch, seq, rows=512)
    x = _merge(ya, yb, yc, pbf, x, p["wa"], p["wb"], p["wc"], p["wo"], p["ln_mix_post"], layer, tm=512)
    return _ffn(x, p["ln_ffn_pre"], p["w_up"], p["ffn_conv"], p["w_down"], p["ln_ffn_post"],
                layer, batch, seq, tm=512)


def kernel(x, ln_mix_pre, w_in, sw_sinks, gdn_conv, gdn_a_log, gdn_dt_bias, gdn_norm, w_branch_a,
           w_branch_b, w_branch_c, w_out, ln_mix_post, ln_ffn_pre, w_up, ffn_conv, w_down, ln_ffn_post):
    batch, seq, d = x.shape
    depth = w_in.shape[0]
    small = jnp.zeros((depth, 8, LANES), F32)
    small = small.at[:, 0, :GDN_HEADS].set(gdn_a_log).at[:, 1, :GDN_HEADS].set(gdn_dt_bias)
    w_main, w_ab = _pack_w_in(w_in)
    p = {
        "ln_mix_pre": ln_mix_pre[:, None, :], "w_in": w_main, "w_ab": w_ab,
        "sinks": sw_sinks, "slopes": jnp.exp2(-8.0 * jnp.arange(1, SW_Q_HEADS + 1, dtype=F32) / SW_Q_HEADS),
        "gdn_conv": gdn_conv, "gdn_small": small, "gdn_norm": gdn_norm[:, None, :],
        "wa": w_branch_a.astype(BF16), "wb": w_branch_b.astype(BF16),
        "wc": w_branch_c.astype(BF16), "wo": w_out.astype(BF16),
        "ln_mix_post": ln_mix_post[:, None, :], "ln_ffn_pre": ln_ffn_pre[:, None, :],
        "w_up": w_up.astype(BF16), "ffn_conv": ffn_conv, "w_down": w_down.astype(BF16),
        "ln_ffn_post": ln_ffn_post[:, None, :],
    }
    h = x.reshape(batch * seq, d)
    for layer in range(depth):
        h = _layer(h, batch, seq, layer, p)
    return h.reshape(batch, seq, d)
```

```python
import functools

import jax
import jax.numpy as jnp
from jax import lax
from jax.experimental import pallas as pl
from jax.experimental.pallas import tpu as pltpu

F32 = jnp.float32
BF16 = jnp.bfloat16
NORM_EPS = 1e-6

D_MODEL = 1024
SB_HEADS, SB_HEAD_DIM = 8, 64
SW_Q_HEADS, SW_KV_HEADS, SW_HEAD_DIM = 8, 2, 64
GDN_HEADS, GDN_HEAD_DIM, GDN_CONV = 4, 128, 4
D_FF, FFN_CONV = 2816, 3
SB_W = SB_HEADS * SB_HEAD_DIM
SW_QW = SW_Q_HEADS * SW_HEAD_DIM
SW_KVW = SW_KV_HEADS * SW_HEAD_DIM
GDN_W = GDN_HEADS * GDN_HEAD_DIM

BLK = 128
LANES = 128
HALF = 64
VMEM_LIMIT = 56 * 1024 * 1024

BF_GA, BF_GB, BF_GC, BF_SBQ, BF_SBK, BF_SBV, BF_SWQ, BF_SWK, BF_SWV, BF_COLS = (
    0, 1024, 2048, 3072, 3584, 4096, 4608, 5120, 5376, 5632)
FP_CQKV, FP_CZ, FP_COLS = 0, 1536, 2048
PROJ_TN = 512
LOG2E = 1.4426950408889634
F32_EXP2_ZERO = -104.0 * LOG2E


def _dot(a, b):
    return jnp.dot(a, b, preferred_element_type=F32)


def _dot_nt(a, b):
    return lax.dot_general(a, b, (((1,), (1,)), ((), ())), preferred_element_type=F32)


def _iota(shape, dim):
    return lax.broadcasted_iota(jnp.int32, shape, dim)


def _rms(t, gain):
    return t * lax.rsqrt(jnp.mean(t * t, axis=-1, keepdims=True) + NORM_EPS) * gain


def _shift_rows(h, prev8, s):
    r = pltpu.roll(h, s, axis=0)
    row = _iota(h.shape, 0)
    for t in range(s):
        r = jnp.where(row == t, prev8[8 - s + t:8 - s + t + 1, :], r)
    return r


def _inproj_kernel(x_ref, g_ref, w_ref, wab_ref, obf_ref, of_ref, oab_ref):
    xn = _rms(x_ref[...], g_ref[...]).astype(BF16)
    oab_ref[...] = _dot(xn, wab_ref[...])
    tn = PROJ_TN
    for j in range(BF_COLS // tn):
        obf_ref[:, j * tn:(j + 1) * tn] = _dot(xn, w_ref[:, j * tn:(j + 1) * tn]).astype(BF16)
    for j in range(FP_COLS // tn):
        of_ref[:, j * tn:(j + 1) * tn] = _dot(xn, w_ref[:, BF_COLS + j * tn:BF_COLS + (j + 1) * tn])


def _inproj(x, gain, w, wab, layer, tm):
    t, d = x.shape

    def whole(a):
        return pl.BlockSpec((None,) + a.shape[1:], lambda i: (layer, 0, 0), pipeline_mode=pl.Buffered(1))

    return pl.pallas_call(
        _inproj_kernel,
        grid=(t // tm,),
        in_specs=[pl.BlockSpec((tm, d), lambda i: (i, 0)), whole(gain), whole(w), whole(wab)],
        out_specs=[
            pl.BlockSpec((tm, BF_COLS), lambda i: (i, 0)),
            pl.BlockSpec((tm, FP_COLS), lambda i: (i, 0)),
            pl.BlockSpec((tm, LANES), lambda i: (i, 0)),
        ],
        out_shape=[jax.ShapeDtypeStruct((t, BF_COLS), BF16), jax.ShapeDtypeStruct((t, FP_COLS), F32),
                   jax.ShapeDtypeStruct((t, LANES), F32)],
        compiler_params=pltpu.CompilerParams(
            dimension_semantics=("arbitrary",), vmem_limit_bytes=VMEM_LIMIT),
        name="inproj",
    )(x, gain, w, wab)


SB_TQ = 2 * BLK
SB_PAIRS = 2


def _sb_kernel(q_ref, k_ref, v_ref, o_ref, acc_ref, lr_ref):
    ti = pl.program_id(2)
    lane = _iota((BLK, LANES), 1)
    row = _iota((BLK, LANES), 0)
    first = lane < HALF
    tri = lane < row
    full = lane >= 0
    zero_tile = jnp.zeros((BLK, LANES), F32)
    r2 = _iota((BLK, 2 * LANES), 0)
    c2 = _iota((BLK, 2 * LANES), 1)
    suffix_ones = jnp.where((r2 > c2) | (c2 >= LANES), 1.0, 0.0).astype(BF16)
    suffix_ones = jnp.concatenate([suffix_ones, suffix_ones], axis=0)

    def split_heads(q):
        sel = jnp.concatenate([first] * (q.shape[0] // BLK), axis=0)
        zero = jnp.zeros_like(q)
        return jnp.where(sel, q, zero), jnp.where(sel, zero, q)

    def log_parts(z):
        zs = z * LOG2E
        log_stay = -(jnp.maximum(zs, 0.0) + jnp.log2(1.0 + jnp.exp2(-jnp.abs(zs))))
        return log_stay, log_stay + zs

    def suffix_sums(parts):
        hi = [x.astype(BF16) for x in parts]
        lo = [(x - h.astype(F32)).astype(BF16) for x, h in zip(parts, hi)]
        hilo = jnp.concatenate([jnp.concatenate(hi, axis=0), jnp.concatenate(lo, axis=0)], axis=1)
        sums = _dot(hilo, suffix_ones)
        out, o = [], 0
        for x in parts:
            out.append((sums[o:o + x.shape[0], :LANES], sums[o:o + x.shape[0], LANES:]))
            o += x.shape[0]
        return out

    def generic_block(pp, s, j, valid):
        ps = slice(pp * LANES, (pp + 1) * LANES)
        q0, q1 = split_heads(q_ref[s * BLK:(s + 1) * BLK, ps])
        off = pl.multiple_of(j * BLK, BLK)
        k, v = k_ref[pl.ds(off, BLK), ps], v_ref[pl.ds(off, BLK), ps]
        log_stay, log_take = log_parts(_dot_nt(jnp.concatenate([q0, q1], axis=0), k))
        if valid is not None:
            vm = jnp.concatenate([valid, valid], axis=0)
            log_stay = jnp.where(vm, log_stay, 0.0)
        (later, total), = suffix_sums([log_stay])
        w = jnp.exp2(log_take + later + lr_ref[pp, s])
        if valid is not None:
            w = jnp.where(vm, w, 0.0)
        pv = _dot(w.astype(BF16), v)
        acc_ref[pp, s] += jnp.where(first, pv[:BLK], pv[BLK:])
        lr_ref[pp, s] += total

    def tail(pp, s, j0):
        def cond(c):
            j, live = c
            return jnp.logical_and(j >= 0, live)

        def body(c):
            j, _ = c
            generic_block(pp, s, j, None)
            return j - 1, jnp.max(lr_ref[pp, s]) > F32_EXP2_ZERO

        lax.while_loop(cond, body, (j0, jnp.max(lr_ref[pp, s]) > F32_EXP2_ZERO))

    @pl.when(ti == 0)
    def _():
        for pp in range(SB_PAIRS):
            for s in range(2):
                acc_ref[pp, s] = jnp.zeros((BLK, LANES), F32)
                lr_ref[pp, s] = jnp.zeros((2 * BLK, LANES), F32)
                generic_block(pp, s, 2 * ti + s, tri)
                tail(pp, s, 2 * ti + s - 1)

    @pl.when(ti > 0)
    def _():
        base = pl.multiple_of((2 * ti - 2) * BLK, BLK)
        m2 = jnp.concatenate([tri, full, tri, full], axis=0)
        m3 = jnp.concatenate([tri, tri], axis=0)
        P = range(SB_PAIRS)
        ps = [slice(pp * LANES, (pp + 1) * LANES) for pp in P]
        qs = [split_heads(q_ref[:, ps[pp]]) for pp in P]
        kw = [k_ref[pl.ds(base, 4 * BLK), ps[pp]] for pp in P]
        vw = [v_ref[pl.ds(base, 4 * BLK), ps[pp]] for pp in P]
        l_ab = [jnp.concatenate([q0, q1], axis=0) for q0, q1 in qs]
        l_a = [jnp.concatenate([q0[:BLK], q1[:BLK]], axis=0) for q0, q1 in qs]
        l_b = [jnp.concatenate([q0[BLK:], q1[BLK:]], axis=0) for q0, q1 in qs]
        z3 = [_dot_nt(l_b[pp], kw[pp][3 * BLK:]) for pp in P]
        z2 = [_dot_nt(l_ab[pp], kw[pp][2 * BLK:3 * BLK]) for pp in P]
        z1 = [_dot_nt(l_ab[pp], kw[pp][BLK:2 * BLK]) for pp in P]
        z0 = [_dot_nt(l_a[pp], kw[pp][:BLK]) for pp in P]
        lp3 = [log_parts(z) for z in z3]
        s3 = [suffix_sums([jnp.where(m3, ls, 0.0)])[0] for ls, _ in lp3]
        lp2 = [log_parts(z) for z in z2]
        s2 = [suffix_sums([jnp.where(m2, ls, 0.0)])[0] for ls, _ in lp2]
        lp1 = [log_parts(z) for z in z1]
        s1 = [suffix_sums([ls])[0] for ls, _ in lp1]
        lp0 = [log_parts(z) for z in z0]
        s0 = [suffix_sums([ls])[0] for ls, _ in lp0]
        for pp in P:
            (lat3, tot3), (lat2, tot2), (lat1, tot1), (lat0, tot0) = s3[pp], s2[pp], s1[pp], s0[pp]
            lt3, lt2, lt1, lt0 = lp3[pp][1], lp2[pp][1], lp1[pp][1], lp0[pp][1]
            lrb2 = jnp.concatenate([zero_tile, tot3[:BLK], zero_tile, tot3[BLK:]], axis=0)
            lrb1 = lrb2 + tot2
            lre1 = lrb1 + tot1
            lrb0 = jnp.concatenate([lre1[:BLK], lre1[2 * BLK:3 * BLK]], axis=0)
            w3 = jnp.where(m3, jnp.exp2(lt3 + lat3), 0.0)
            w2 = jnp.where(m2, jnp.exp2(lt2 + lat2 + lrb2), 0.0)
            w1 = jnp.exp2(lt1 + lat1 + lrb1)
            w0 = jnp.exp2(lt0 + lat0 + lrb0)
            w12 = jnp.concatenate([w1.astype(BF16), w2.astype(BF16)], axis=1)
            pv12 = _dot(w12, vw[pp][BLK:3 * BLK])
            pv0 = _dot(w0.astype(BF16), vw[pp][:BLK])
            pv3 = _dot(w3.astype(BF16), vw[pp][3 * BLK:])
            acc_ref[pp, 0] = jnp.where(first, pv12[:BLK] + pv0[:BLK], pv12[2 * BLK:3 * BLK] + pv0[BLK:])
            acc_ref[pp, 1] = jnp.where(first, pv12[BLK:2 * BLK] + pv3[:BLK], pv12[3 * BLK:] + pv3[BLK:])
            lr_ref[pp, 0] = lrb0 + tot0
            lr_ref[pp, 1] = jnp.concatenate([lre1[BLK:2 * BLK], lre1[3 * BLK:]], axis=0)

        @pl.when(jnp.max(lr_ref[...]) > F32_EXP2_ZERO)
        def _():
            for pp in range(SB_PAIRS):
                tail(pp, 0, 2 * ti - 3)
                tail(pp, 1, 2 * ti - 2)

    for pp in range(SB_PAIRS):
        for s in range(2):
            o_ref[s * BLK:(s + 1) * BLK, pp * LANES:(pp + 1) * LANES] = acc_ref[pp, s].astype(o_ref.dtype)


def _sb_attention(pbf, batch, seq):
    nt = seq // SB_TQ
    w = SB_PAIRS * LANES
    cq, ck, cv = BF_SBQ // w, BF_SBK // w, BF_SBV // w
    return pl.pallas_call(
        _sb_kernel,
        grid=(batch, SB_W // w, nt),
        in_specs=[
            pl.BlockSpec((SB_TQ, w), lambda b, p, i: (b * nt + i, cq + p)),
            pl.BlockSpec((seq, w), lambda b, p, i: (b, ck + p)),
            pl.BlockSpec((seq, w), lambda b, p, i: (b, cv + p)),
        ],
        out_specs=pl.BlockSpec((SB_TQ, w), lambda b, p, i: (b * nt + i, p)),
        out_shape=jax.ShapeDtypeStruct((batch * seq, SB_W), BF16),
        scratch_shapes=[pltpu.VMEM((SB_PAIRS, 2, BLK, LANES), F32),
                        pltpu.VMEM((SB_PAIRS, 2, 2 * BLK, LANES), F32)],
        compiler_params=pltpu.CompilerParams(
            dimension_semantics=("arbitrary", "arbitrary", "arbitrary"), vmem_limit_bytes=VMEM_LIMIT),
        name="sb_attention",
    )(pbf, pbf, pbf)


SW_TQ = 2 * BLK


def _sw_kernel(sinks_ref, slopes_ref, q_ref, kp_ref, kc_ref, vp_ref, vc_ref, o_ref, *, layer):
    ti = pl.program_id(1)
    lane = _iota((BLK, LANES), 1)
    row = _iota((BLK, LANES), 0)
    first = lane < HALF
    cur = lane <= row
    dist = jnp.where(cur, row - lane, row - lane + BLK).astype(F32)
    group = SW_Q_HEADS // SW_KV_HEADS
    units = []
    for s in range(SW_TQ // BLK):
        rs = slice(s * BLK, (s + 1) * BLK)
        for hk in range(SW_KV_HEADS):
            ks = slice(hk * LANES, (hk + 1) * LANES)
            kc, vc = kc_ref[rs, ks], vc_ref[rs, ks]
            if s == 0:
                kp, vp = kp_ref[:, ks], vp_ref[:, ks]
            else:
                kp, vp = kc_ref[(s - 1) * BLK:s * BLK, ks], vc_ref[(s - 1) * BLK:s * BLK, ks]
            pairs = range(hk * group // 2, (hk + 1) * group // 2)
            lhs = []
            for p in pairs:
                qp = q_ref[rs, p * LANES:(p + 1) * LANES]
                zero = jnp.zeros_like(qp)
                lhs += [jnp.where(first, qp, zero), jnp.where(first, zero, qp)]
            lhs = jnp.concatenate(lhs, axis=0)
            units.append(dict(s=s, rs=rs, hk=hk, pairs=pairs, vc=vc, vp=vp,
                              s_cur=_dot_nt(lhs, kc), s_prev=_dot_nt(lhs, kp)))
    for u in units:
        scores = []
        for g in range(group):
            head = u["hk"] * group + g
            gs = slice(g * BLK, (g + 1) * BLK)
            sc = jnp.where(cur, u["s_cur"][gs], u["s_prev"][gs]) - slopes_ref[head] * dist
            if u["s"] == 0:
                sc = jnp.where(jnp.logical_or(cur, ti > 0), sc, -jnp.inf)
            scores.append(sc)
        u["scores"] = scores
        u["m"] = [jnp.maximum(jnp.max(sc, axis=-1, keepdims=True), sinks_ref[layer, u["hk"] * group + g])
                  for g, sc in enumerate(scores)]
    for u in units:
        u["e"] = [jnp.exp(sc - m) for sc, m in zip(u["scores"], u["m"])]
        u["denom"] = [jnp.sum(e, axis=-1, keepdims=True) + jnp.exp(sinks_ref[layer, u["hk"] * group + g] - m)
                      for g, (e, m) in enumerate(zip(u["e"], u["m"]))]
    for u in units:
        prob = [(e / d).astype(BF16) for e, d in zip(u["e"], u["denom"])]
        pz = jnp.zeros_like(prob[0])
        prob_c = jnp.concatenate([jnp.where(cur, p, pz) for p in prob], axis=0)
        prob_p = jnp.concatenate([jnp.where(cur, pz, p) for p in prob], axis=0)
        pv = _dot(prob_c, u["vc"]) + _dot(prob_p, u["vp"])
        for t, p in enumerate(u["pairs"]):
            o_ref[u["rs"], p * LANES:(p + 1) * LANES] = jnp.where(
                first, pv[2 * t * BLK:(2 * t + 1) * BLK], pv[(2 * t + 1) * BLK:(2 * t + 2) * BLK]
            ).astype(o_ref.dtype)


def _sw_attention(pbf, sinks, slopes, layer, batch, seq):
    nt = seq // SW_TQ
    kvw = 2 * SW_KVW
    cq, ck, cv = BF_SWQ // SW_QW, BF_SWK // kvw, BF_SWV // kvw
    per = SW_TQ // BLK

    def prev(b, i):
        return b * nt * per + jnp.maximum(i * per - 1, 0)

    smem = pl.BlockSpec(memory_space=pltpu.SMEM)
    return pl.pallas_call(
        functools.partial(_sw_kernel, layer=layer),
        grid=(batch, nt),
        in_specs=[
            smem, smem,
            pl.BlockSpec((SW_TQ, SW_QW), lambda b, i: (b * nt + i, cq)),
            pl.BlockSpec((BLK, kvw), lambda b, i: (prev(b, i), ck)),
            pl.BlockSpec((SW_TQ, kvw), lambda b, i: (b * nt + i, ck)),
            pl.BlockSpec((BLK, kvw), lambda b, i: (prev(b, i), cv)),
            pl.BlockSpec((SW_TQ, kvw), lambda b, i: (b * nt + i, cv)),
        ],
        out_specs=pl.BlockSpec((SW_TQ, SW_QW), lambda b, i: (b * nt + i, 0)),
        out_shape=jax.ShapeDtypeStruct((batch * seq, SW_QW), BF16),
        compiler_params=pltpu.CompilerParams(
            dimension_semantics=("arbitrary", "arbitrary"), vmem_limit_bytes=VMEM_LIMIT),
        name="sw_attention",
    )(sinks, slopes, pbf, pbf, pbf, pbf, pbf)


GDN_CHUNK = 128


def _split3(x):
    x1 = x.astype(BF16)
    r1 = x - x1.astype(F32)
    x2 = r1.astype(BF16)
    x3 = (r1 - x2.astype(F32)).astype(BF16)
    return x1, x2, x3


def _gdn_kernel(q_ref, k_ref, v_ref, z_ref, ab_ref, conv_ref, small_ref, norm_ref, o_ref,
                carry_ref, state_ref, yq_ref, yk_ref, yv_ref, *, rows):
    n = GDN_CHUNK
    nchunks = rows // n
    step = pl.program_id(1)

    @pl.when(step == 0)
    def _():
        carry_ref[...] = jnp.zeros_like(carry_ref)
        state_ref[...] = jnp.zeros_like(state_ref)

    for idx, (src, dst) in enumerate(((q_ref, yq_ref), (k_ref, yk_ref), (v_ref, yv_ref))):
        cols = slice(idx * GDN_W, (idx + 1) * GDN_W)
        w = conv_ref[:, cols]
        for ci in range(nchunks):
            x = src[ci * n:(ci + 1) * n, :]
            prev8 = carry_ref[:, cols] if ci == 0 else src[ci * n - 8:ci * n, :]
            y = w[GDN_CONV - 1:GDN_CONV, :] * x
            for s in range(1, GDN_CONV):
                y = y + w[GDN_CONV - 1 - s:GDN_CONV - s, :] * _shift_rows(x, prev8, s)
            y = y * jax.nn.sigmoid(y)
            for h in range(GDN_HEADS):
                yh = y[:, h * GDN_HEAD_DIM:(h + 1) * GDN_HEAD_DIM]
                if idx < 2:
                    inv = lax.rsqrt(jnp.sum(yh * yh, axis=-1, keepdims=True) + NORM_EPS)
                    yh = yh * (inv * (GDN_HEAD_DIM ** -0.5) if idx == 0 else inv)
                dst[ci * n:(ci + 1) * n, h * GDN_HEAD_DIM:(h + 1) * GDN_HEAD_DIM] = yh
        carry_ref[:, cols] = src[rows - 8:rows, :]

    r = _iota((n, n), 0)
    c = _iota((n, n), 1)
    causal = r >= c
    strict = r > c
    lower_ones = jnp.where(causal, 1.0, 0.0).astype(BF16)
    head_lane = _iota((n, LANES), 1) < GDN_HEADS

    def sub_blocks(ls):
        return ((r >> (ls + 1)) == (c >> (ls + 1))) & ((r >> ls) != (c >> ls))

    probs = []
    for ci in range(nchunks):
        rs = slice(ci * n, (ci + 1) * n)
        ab = ab_ref[rs, :]
        zab = ab + small_ref[1:2, :]
        softplus = jnp.maximum(zab, 0.0) + jnp.log(1.0 + jnp.exp(-jnp.abs(zab)))
        gbeta = jnp.where(head_lane, -jnp.exp(small_ref[0:1, :]) * softplus, jax.nn.sigmoid(ab))
        gc_all = sum(_dot(lower_ones, part) for part in _split3(gbeta))
        gc_t = gc_all.T
        for h in range(GDN_HEADS):
            hs = slice(h * GDN_HEAD_DIM, (h + 1) * GDN_HEAD_DIM)
            q, k, v = yq_ref[rs, hs], yk_ref[rs, hs], yv_ref[rs, hs]
            gc = jnp.broadcast_to(gc_all[:, h:h + 1], (n, n))
            gc_row = jnp.broadcast_to(gc_t[h:h + 1, :], (n, n))
            g_tot = jnp.broadcast_to(gc[n - 1:n, :], (n, n))
            beta_b = jnp.broadcast_to(gbeta[:, GDN_HEADS + h:GDN_HEADS + h + 1], (n, n))
            decay = jnp.exp(gc - gc_row)
            k_beta = k * beta_b
            k16 = k.astype(BF16)
            e_gc = jnp.exp(gc)
            probs.append(dict(
                rs=rs, hs=hs, h=h,
                lower=jnp.where(strict, _dot_nt(k_beta.astype(BF16), k16) * decay, 0.0),
                a=jnp.where(causal, _dot_nt(q.astype(BF16), k16) * decay, 0.0).astype(BF16),
                qd=(q * e_gc).astype(BF16),
                kdt=(k * jnp.exp(g_tot - gc)).T.astype(BF16),
                gl=jnp.exp(g_tot),
                rhs=jnp.concatenate([v * beta_b, k_beta * e_gc], axis=1)))
    eye = jnp.where(r == c, 1.0, 0.0).astype(F32)
    zero16 = jnp.zeros((n, n), BF16)
    lower16 = [p["lower"].astype(BF16) for p in probs]
    xs = [eye - jnp.where((r >> 1) == (c >> 1), p["lower"], 0.0) for p in probs]
    for ls in range(1, n.bit_length() - 1):
        x16 = [x.astype(BF16) for x in xs]
        ys = [_dot(xb, jnp.where(sub_blocks(ls), l16, zero16)).astype(BF16) for xb, l16 in zip(x16, lower16)]
        xs = [x - _dot(y, xb) for x, y, xb in zip(xs, ys, x16)]
    for p, x in zip(probs, xs):
        uw = _dot(x.astype(BF16), p["rhs"].astype(BF16))
        p["u"] = uw[:, :n]
        p["wq"] = jnp.concatenate([uw[:, n:].astype(BF16), p["qd"]], axis=0)
        p["ak"] = jnp.concatenate([p["a"], p["kdt"]], axis=0)

    gain = norm_ref[...]
    states = [state_ref[h] for h in range(GDN_HEADS)]
    for p in probs:
        h = p["h"]
        sr = _dot(p["wq"], states[h].astype(BF16))
        vn16 = (p["u"] - sr[:n]).astype(BF16)
        vr = _dot(p["ak"], vn16)
        o = sr[n:] + vr[:n]
        states[h] = states[h] * p["gl"] + vr[n:]
        zg = z_ref[p["rs"], p["hs"]]
        o_ref[p["rs"], p["hs"]] = (_rms(o, gain) * (zg * jax.nn.sigmoid(zg))).astype(o_ref.dtype)
    for h in range(GDN_HEADS):
        state_ref[h] = states[h]


def _gdn(pfp, pab, conv_w, small, norm, layer, batch, seq, rows):
    steps = seq // rows
    w = GDN_W
    cq = FP_CQKV // w

    def blk(col):
        return pl.BlockSpec((rows, w), lambda b, i: (b * steps + i, col))

    return pl.pallas_call(
        functools.partial(_gdn_kernel, rows=rows),
        grid=(batch, steps),
        in_specs=[
            blk(cq), blk(cq + 1), blk(cq + 2), blk(FP_CZ // w),
            pl.BlockSpec((rows, LANES), lambda b, i: (b * steps + i, 0)),
            pl.BlockSpec((None, GDN_CONV, 3 * w), lambda b, i: (layer, 0, 0)),
            pl.BlockSpec((None, 8, LANES), lambda b, i: (layer, 0, 0)),
            pl.BlockSpec((None, 1, GDN_HEAD_DIM), lambda b, i: (layer, 0, 0)),
        ],
        out_specs=pl.BlockSpec((rows, w), lambda b, i: (b * steps + i, 0)),
        out_shape=jax.ShapeDtypeStruct((batch * seq, w), BF16),
        scratch_shapes=[
            pltpu.VMEM((8, 3 * w), F32),
            pltpu.VMEM((GDN_HEADS, GDN_HEAD_DIM, GDN_HEAD_DIM), F32),
            pltpu.VMEM((rows, w), F32), pltpu.VMEM((rows, w), F32), pltpu.VMEM((rows, w), F32),
        ],
        compiler_params=pltpu.CompilerParams(
            dimension_semantics=("arbitrary", "arbitrary"), vmem_limit_bytes=VMEM_LIMIT),
        name="gdn",
    )(pfp, pfp, pfp, pfp, pab, conv_w, small, norm)


def _merge_kernel(ya_ref, yb_ref, yc_ref, ga_ref, gb_ref, gc_ref, x_ref,
                  wa_ref, wb_ref, wc_ref, wo_ref, gain_ref, o_ref):
    merged = (jax.nn.sigmoid(ga_ref[...].astype(F32)) * _dot(ya_ref[...], wa_ref[...])
              + jax.nn.sigmoid(gb_ref[...].astype(F32)) * _dot(yb_ref[...], wb_ref[...])
              + jax.nn.sigmoid(gc_ref[...].astype(F32)) * _dot(yc_ref[...], wc_ref[...]))
    r = _dot(merged.astype(BF16), wo_ref[...])
    o_ref[...] = x_ref[...] + _rms(r, gain_ref[...])


def _merge(ya, yb, yc, pbf, x, wa, wb, wc, wo, gain, layer, tm):
    t, d = x.shape

    def rows(width, col=0):
        return pl.BlockSpec((tm, width), lambda i: (i, col))

    def whole(a):
        return pl.BlockSpec((None,) + a.shape[1:], lambda i: (layer, 0, 0))

    return pl.pallas_call(
        _merge_kernel,
        grid=(t // tm,),
        in_specs=[rows(SB_W), rows(SW_QW), rows(GDN_W),
                  rows(d, BF_GA // d), rows(d, BF_GB // d), rows(d, BF_GC // d), rows(d),
                  whole(wa), whole(wb), whole(wc), whole(wo), whole(gain)],
        out_specs=rows(d),
        out_shape=jax.ShapeDtypeStruct((t, d), F32),
        compiler_params=pltpu.CompilerParams(
            dimension_semantics=("arbitrary",), vmem_limit_bytes=VMEM_LIMIT),
        name="merge",
    )(ya, yb, yc, pbf, pbf, pbf, x, wa, wb, wc, wo, gain)


FFN_CK = 256
SQRT_2_OVER_PI = 0.7978845608028654


def _gelu_tanh(x):
    inner = x * (SQRT_2_OVER_PI + (SQRT_2_OVER_PI * 0.044715) * (x * x))
    return (0.5 * x) * (1.0 + jnp.tanh(inner))


def _ffn_kernel(x_ref, gpre_ref, wup_ref, conv_ref, wdn_ref, gpost_ref, o_ref, carry_ref, f_ref, *, tm):
    @pl.when(pl.program_id(1) == 0)
    def _():
        carry_ref[...] = jnp.zeros_like(carry_ref)

    x = x_ref[...]
    hn = _rms(x, gpre_ref[...]).astype(BF16)

    def conv(col):
        cs = slice(col, col + FFN_CK)
        hid = _dot(hn, wup_ref[:, cs])
        prev8 = carry_ref[:, cs]
        w = conv_ref[:, cs]
        y = w[FFN_CONV - 1:FFN_CONV, :] * hid
        for s in range(1, FFN_CONV):
            y = y + w[FFN_CONV - 1 - s:FFN_CONV - s, :] * _shift_rows(hid, prev8, s)
        carry_ref[:, cs] = hid[tm - 8:tm, :]
        return y

    for ci in range(D_FF // FFN_CK):
        f_gate = conv(ci * FFN_CK)
        f_up = conv(D_FF + ci * FFN_CK)
        f_ref[:, ci * FFN_CK:(ci + 1) * FFN_CK] = (_gelu_tanh(f_gate) * f_up).astype(BF16)
    o_ref[...] = x + _rms(_dot(f_ref[...], wdn_ref[...]), gpost_ref[...])


def _ffn(x, gpre, wup, conv_w, wdn, gpost, layer, batch, seq, tm):
    steps = seq // tm
    d = x.shape[1]

    def whole(a):
        return pl.BlockSpec((None,) + a.shape[1:], lambda b, i: (layer, 0, 0))

    return pl.pallas_call(
        functools.partial(_ffn_kernel, tm=tm),
        grid=(batch, steps),
        in_specs=[pl.BlockSpec((tm, d), lambda b, i: (b * steps + i, 0)),
                  whole(gpre), whole(wup), whole(conv_w), whole(wdn), whole(gpost)],
        out_specs=pl.BlockSpec((tm, d), lambda b, i: (b * steps + i, 0)),
        out_shape=jax.ShapeDtypeStruct(x.shape, F32),
        scratch_shapes=[pltpu.VMEM((8, 2 * D_FF), F32), pltpu.VMEM((tm, D_FF), BF16)],
        compiler_params=pltpu.CompilerParams(
            dimension_semantics=("arbitrary", "arbitrary"), vmem_limit_bytes=VMEM_LIMIT),
        name="ffn",
    )(x, gpre, wup, conv_w, wdn, gpost)


def _pack_w_in(w):
    w = w.astype(BF16)
    o = 0
    parts = {}
    for name, width in (("aq", SB_W), ("ak", SB_W), ("av", SB_W), ("bq", SW_QW), ("bk", SW_KVW),
                        ("bv", SW_KVW), ("cqkv", 3 * GDN_W), ("cz", GDN_W), ("ca", GDN_HEADS),
                        ("cb", GDN_HEADS), ("ga", D_MODEL), ("gb", D_MODEL), ("gc", D_MODEL)):
        parts[name] = w[..., o:o + width]
        o += width

    def dup(t):
        return jnp.concatenate([t[..., h * HALF:(h + 1) * HALF] for h in range(SW_KV_HEADS) for _ in range(2)], axis=-1)

    cols = [parts["ga"], parts["gb"], parts["gc"],
            parts["aq"] * (SB_HEAD_DIM ** -0.5), parts["ak"], parts["av"],
            parts["bq"] * (SW_HEAD_DIM ** -0.5), dup(parts["bk"]), dup(parts["bv"]),
            parts["cqkv"], parts["cz"]]
    pad = jnp.zeros(w.shape[:-1] + (LANES - 2 * GDN_HEADS,), w.dtype)
    wab = jnp.concatenate([parts["ca"], parts["cb"], pad], axis=-1)
    return jnp.concatenate(cols, axis=-1).astype(BF16), wab


def _layer(x, batch, seq, layer, p):
    pbf, pfp, pab = _inproj(x, p["ln_mix_pre"], p["w_in"], p["w_ab"], layer, tm=512)
    ya = _sb_attention(pbf, batch, seq)
    yb = _sw_attention(pbf, p["sinks"], p["slopes"], layer, batch, seq)
    yc = _gdn(pfp, pab, p["gdn_conv"], p["gdn_small"], p["gdn_norm"], layer, batch, seq, rows=512)
    x = _merge(ya, yb, yc, pbf, x, p["wa"], p["wb"], p["wc"], p["wo"], p["ln_mix_post"], layer, tm=1024)
    return _ffn(x, p["ln_ffn_pre"], p["w_up"], p["ffn_conv"], p["w_down"], p["ln_ffn_post"],
                layer, batch, seq, tm=512)


def kernel(x, ln_mix_pre, w_in, sw_sinks, gdn_conv, gdn_a_log, gdn_dt_bias, gdn_norm, w_branch_a,
           w_branch_b, w_branch_c, w_out, ln_mix_post, ln_ffn_pre, w_up, ffn_conv, w_down, ln_ffn_post):
    batch, seq, d = x.shape
    depth = w_in.shape[0]
    small = jnp.zeros((depth, 8, LANES), F32)
    small = small.at[:, 0, :GDN_HEADS].set(gdn_a_log).at[:, 1, :GDN_HEADS].set(gdn_dt_bias)
    w_main, w_ab = _pack_w_in(w_in)
    p = {
        "ln_mix_pre": ln_mix_pre[:, None, :], "w_in": w_main, "w_ab": w_ab,
        "sinks": sw_sinks, "slopes": jnp.exp2(-8.0 * jnp.arange(1, SW_Q_HEADS + 1, dtype=F32) / SW_Q_HEADS),
        "gdn_conv": gdn_conv, "gdn_small": small, "gdn_norm": gdn_norm[:, None, :],
        "wa": w_branch_a.astype(BF16), "wb": w_branch_b.astype(BF16),
        "wc": w_branch_c.astype(BF16), "wo": w_out.astype(BF16),
        "ln_mix_post": ln_mix_post[:, None, :], "ln_ffn_pre": ln_ffn_pre[:, None, :],
        "w_up": w_up.astype(BF16), "ffn_conv": ffn_conv, "w_down": w_down.astype(BF16),
        "ln_ffn_post": ln_ffn_post[:, None, :],
    }
    h = x.reshape(batch * seq, d)
    for layer in range(depth):
        h = _layer(h, batch, seq, layer, p)
    return h.reshape(batch, seq, d)
```

```python
import functools

import jax
import jax.numpy as jnp
from jax import lax
from jax.experimental import pallas as pl
from jax.experimental.pallas import tpu as pltpu

F32 = jnp.float32
BF16 = jnp.bfloat16
NORM_EPS = 1e-6

D_MODEL = 1024
SB_HEADS, SB_HEAD_DIM = 8, 64
SW_Q_HEADS, SW_KV_HEADS, SW_HEAD_DIM = 8, 2, 64
GDN_HEADS, GDN_HEAD_DIM, GDN_CONV = 4, 128, 4
D_FF, FFN_CONV = 2816, 3
SB_W = SB_HEADS * SB_HEAD_DIM
SW_QW = SW_Q_HEADS * SW_HEAD_DIM
SW_KVW = SW_KV_HEADS * SW_HEAD_DIM
GDN_W = GDN_HEADS * GDN_HEAD_DIM

BLK = 128
LANES = 128
HALF = 64
VMEM_LIMIT = 56 * 1024 * 1024

BF_GA, BF_GB, BF_GC, BF_SBQ, BF_SBK, BF_SBV, BF_SWQ, BF_SWK, BF_SWV, BF_COLS = (
    0, 1024, 2048, 3072, 3584, 4096, 4608, 5120, 5376, 5632)
FP_CQKV, FP_CZ, FP_COLS = 0, 1536, 2048
PROJ_TN = 512
LOG2E = 1.4426950408889634
F32_EXP2_ZERO = -104.0 * LOG2E


def _dot(a, b):
    return jnp.dot(a, b, preferred_element_type=F32)


def _dot_nt(a, b):
    return lax.dot_general(a, b, (((1,), (1,)), ((), ())), preferred_element_type=F32)


def _iota(shape, dim):
    return lax.broadcasted_iota(jnp.int32, shape, dim)


def _rms(t, gain):
    return t * lax.rsqrt(jnp.mean(t * t, axis=-1, keepdims=True) + NORM_EPS) * gain


def _shift_rows(h, prev8, s):
    r = pltpu.roll(h, s, axis=0)
    row = _iota(h.shape, 0)
    for t in range(s):
        r = jnp.where(row == t, prev8[8 - s + t:8 - s + t + 1, :], r)
    return r


def _inproj_kernel(x_ref, g_ref, w_ref, wab_ref, obf_ref, of_ref, oab_ref):
    xn = _rms(x_ref[...], g_ref[...]).astype(BF16)
    oab_ref[...] = _dot_nt(xn, wab_ref[...])
    tn = PROJ_TN
    for j in range(BF_COLS // tn):
        obf_ref[:, j * tn:(j + 1) * tn] = _dot_nt(xn, w_ref[j * tn:(j + 1) * tn, :]).astype(BF16)
    for j in range(FP_COLS // tn):
        of_ref[:, j * tn:(j + 1) * tn] = _dot_nt(xn, w_ref[BF_COLS + j * tn:BF_COLS + (j + 1) * tn, :])


def _inproj(x, gain, w, wab, layer, tm):
    t, d = x.shape

    def whole(a):
        return pl.BlockSpec((None,) + a.shape[1:], lambda i: (layer, 0, 0), pipeline_mode=pl.Buffered(1))

    return pl.pallas_call(
        _inproj_kernel,
        grid=(t // tm,),
        in_specs=[pl.BlockSpec((tm, d), lambda i: (i, 0)), whole(gain), whole(w), whole(wab)],
        out_specs=[
            pl.BlockSpec((tm, BF_COLS), lambda i: (i, 0)),
            pl.BlockSpec((tm, FP_COLS), lambda i: (i, 0)),
            pl.BlockSpec((tm, LANES), lambda i: (i, 0)),
        ],
        out_shape=[jax.ShapeDtypeStruct((t, BF_COLS), BF16), jax.ShapeDtypeStruct((t, FP_COLS), F32),
                   jax.ShapeDtypeStruct((t, LANES), F32)],
        compiler_params=pltpu.CompilerParams(
            dimension_semantics=("arbitrary",), vmem_limit_bytes=VMEM_LIMIT),
        name="inproj",
    )(x, gain, w, wab)


SB_TQ = 2 * BLK
SB_PAIRS = 2


def _sb_kernel(q_ref, k_ref, v_ref, o_ref, acc_ref, lr_ref):
    ti = pl.program_id(2)
    lane = _iota((BLK, LANES), 1)
    row = _iota((BLK, LANES), 0)
    first = lane < HALF
    tri = lane < row
    full = lane >= 0
    zero_tile = jnp.zeros((BLK, LANES), F32)
    r2 = _iota((BLK, 2 * LANES), 0)
    c2 = _iota((BLK, 2 * LANES), 1)
    suffix_ones = jnp.where((r2 > c2) | (c2 >= LANES), 1.0, 0.0).astype(BF16)
    suffix_ones = jnp.concatenate([suffix_ones, suffix_ones], axis=0)

    def split_heads(q):
        sel = jnp.concatenate([first] * (q.shape[0] // BLK), axis=0)
        zero = jnp.zeros_like(q)
        return jnp.where(sel, q, zero), jnp.where(sel, zero, q)

    def log_parts(z):
        zs = z * LOG2E
        log_stay = -(jnp.maximum(zs, 0.0) + jnp.log2(1.0 + jnp.exp2(-jnp.abs(zs))))
        return log_stay, log_stay + zs

    def suffix_sums(parts):
        hi = [x.astype(BF16) for x in parts]
        lo = [(x - h.astype(F32)).astype(BF16) for x, h in zip(parts, hi)]
        hilo = jnp.concatenate([jnp.concatenate(hi, axis=0), jnp.concatenate(lo, axis=0)], axis=1)
        sums = _dot(hilo, suffix_ones)
        out, o = [], 0
        for x in parts:
            out.append((sums[o:o + x.shape[0], :LANES], sums[o:o + x.shape[0], LANES:]))
            o += x.shape[0]
        return out

    def generic_block(pp, s, j, valid):
        ps = slice(pp * LANES, (pp + 1) * LANES)
        q0, q1 = split_heads(q_ref[s * BLK:(s + 1) * BLK, ps])
        off = pl.multiple_of(j * BLK, BLK)
        k, v = k_ref[pl.ds(off, BLK), ps], v_ref[pl.ds(off, BLK), ps]
        log_stay, log_take = log_parts(_dot_nt(jnp.concatenate([q0, q1], axis=0), k))
        if valid is not None:
            vm = jnp.concatenate([valid, valid], axis=0)
            log_stay = jnp.where(vm, log_stay, 0.0)
        (later, total), = suffix_sums([log_stay])
        w = jnp.exp2(log_take + later + lr_ref[pp, s])
        if valid is not None:
            w = jnp.where(vm, w, 0.0)
        pv = _dot(w.astype(BF16), v)
        acc_ref[pp, s] += jnp.where(first, pv[:BLK], pv[BLK:])
        lr_ref[pp, s] += total

    def tail(pp, s, j0):
        def cond(c):
            j, live = c
            return jnp.logical_and(j >= 0, live)

        def body(c):
            j, _ = c
            generic_block(pp, s, j, None)
            return j - 1, jnp.max(lr_ref[pp, s]) > F32_EXP2_ZERO

        lax.while_loop(cond, body, (j0, jnp.max(lr_ref[pp, s]) > F32_EXP2_ZERO))

    @pl.when(ti == 0)
    def _():
        for pp in range(SB_PAIRS):
            for s in range(2):
                acc_ref[pp, s] = jnp.zeros((BLK, LANES), F32)
                lr_ref[pp, s] = jnp.zeros((2 * BLK, LANES), F32)
                generic_block(pp, s, 2 * ti + s, tri)
                tail(pp, s, 2 * ti + s - 1)

    @pl.when(ti > 0)
    def _():
        base = pl.multiple_of((2 * ti - 2) * BLK, BLK)
        m2 = jnp.concatenate([tri, full, tri, full], axis=0)
        m3 = jnp.concatenate([tri, tri], axis=0)
        P = range(SB_PAIRS)
        ps = [slice(pp * LANES, (pp + 1) * LANES) for pp in P]
        qs = [split_heads(q_ref[:, ps[pp]]) for pp in P]
        kw = [k_ref[pl.ds(base, 4 * BLK), ps[pp]] for pp in P]
        vw = [v_ref[pl.ds(base, 4 * BLK), ps[pp]] for pp in P]
        l_ab = [jnp.concatenate([q0, q1], axis=0) for q0, q1 in qs]
        l_a = [jnp.concatenate([q0[:BLK], q1[:BLK]], axis=0) for q0, q1 in qs]
        l_b = [jnp.concatenate([q0[BLK:], q1[BLK:]], axis=0) for q0, q1 in qs]
        z3 = [_dot_nt(l_b[pp], kw[pp][3 * BLK:]) for pp in P]
        z2 = [_dot_nt(l_ab[pp], kw[pp][2 * BLK:3 * BLK]) for pp in P]
        z1 = [_dot_nt(l_ab[pp], kw[pp][BLK:2 * BLK]) for pp in P]
        z0 = [_dot_nt(l_a[pp], kw[pp][:BLK]) for pp in P]
        lp3 = [log_parts(z) for z in z3]
        s3 = [suffix_sums([jnp.where(m3, ls, 0.0)])[0] for ls, _ in lp3]
        lp2 = [log_parts(z) for z in z2]
        s2 = [suffix_sums([jnp.where(m2, ls, 0.0)])[0] for ls, _ in lp2]
        lp1 = [log_parts(z) for z in z1]
        s1 = [suffix_sums([ls])[0] for ls, _ in lp1]
        lp0 = [log_parts(z) for z in z0]
        s0 = [suffix_sums([ls])[0] for ls, _ in lp0]
        for pp in P:
            (lat3, tot3), (lat2, tot2), (lat1, tot1), (lat0, tot0) = s3[pp], s2[pp], s1[pp], s0[pp]
            lt3, lt2, lt1, lt0 = lp3[pp][1], lp2[pp][1], lp1[pp][1], lp0[pp][1]
            lrb2 = jnp.concatenate([zero_tile, tot3[:BLK], zero_tile, tot3[BLK:]], axis=0)
            lrb1 = lrb2 + tot2
            lre1 = lrb1 + tot1
            lrb0 = jnp.concatenate([lre1[:BLK], lre1[2 * BLK:3 * BLK]], axis=0)
            w3 = jnp.where(m3, jnp.exp2(lt3 + lat3), 0.0)
            w2 = jnp.where(m2, jnp.exp2(lt2 + lat2 + lrb2), 0.0)
            w1 = jnp.exp2(lt1 + lat1 + lrb1)
            w0 = jnp.exp2(lt0 + lat0 + lrb0)
            w12 = jnp.concatenate([w1.astype(BF16), w2.astype(BF16)], axis=1)
            pv12 = _dot(w12, vw[pp][BLK:3 * BLK])
            pv0 = _dot(w0.astype(BF16), vw[pp][:BLK])
            pv3 = _dot(w3.astype(BF16), vw[pp][3 * BLK:])
            acc_ref[pp, 0] = jnp.where(first, pv12[:BLK] + pv0[:BLK], pv12[2 * BLK:3 * BLK] + pv0[BLK:])
            acc_ref[pp, 1] = jnp.where(first, pv12[BLK:2 * BLK] + pv3[:BLK], pv12[3 * BLK:] + pv3[BLK:])
            lr_ref[pp, 0] = lrb0 + tot0
            lr_ref[pp, 1] = jnp.concatenate([lre1[BLK:2 * BLK], lre1[3 * BLK:]], axis=0)

        @pl.when(jnp.max(lr_ref[...]) > F32_EXP2_ZERO)
        def _():
            for pp in range(SB_PAIRS):
                tail(pp, 0, 2 * ti - 3)
                tail(pp, 1, 2 * ti - 2)

    for pp in range(SB_PAIRS):
        for s in range(2):
            o_ref[s * BLK:(s + 1) * BLK, pp * LANES:(pp + 1) * LANES] = acc_ref[pp, s].astype(o_ref.dtype)


def _sb_attention(pbf, batch, seq):
    nt = seq // SB_TQ
    w = SB_PAIRS * LANES
    cq, ck, cv = BF_SBQ // w, BF_SBK // w, BF_SBV // w
    return pl.pallas_call(
        _sb_kernel,
        grid=(batch, SB_W // w, nt),
        in_specs=[
            pl.BlockSpec((SB_TQ, w), lambda b, p, i: (b * nt + i, cq + p)),
            pl.BlockSpec((seq, w), lambda b, p, i: (b, ck + p)),
            pl.BlockSpec((seq, w), lambda b, p, i: (b, cv + p)),
        ],
        out_specs=pl.BlockSpec((SB_TQ, w), lambda b, p, i: (b * nt + i, p)),
        out_shape=jax.ShapeDtypeStruct((batch * seq, SB_W), BF16),
        scratch_shapes=[pltpu.VMEM((SB_PAIRS, 2, BLK, LANES), F32),
                        pltpu.VMEM((SB_PAIRS, 2, 2 * BLK, LANES), F32)],
        compiler_params=pltpu.CompilerParams(
            dimension_semantics=("arbitrary", "arbitrary", "arbitrary"), vmem_limit_bytes=VMEM_LIMIT),
        name="sb_attention",
    )(pbf, pbf, pbf)


SW_TQ = 4 * BLK


def _sw_kernel(sinks_ref, slopes_ref, q_ref, kp_ref, kc_ref, vp_ref, vc_ref, o_ref, *, layer):
    ti = pl.program_id(1)
    lane = _iota((BLK, LANES), 1)
    row = _iota((BLK, LANES), 0)
    first = lane < HALF
    cur = lane <= row
    dist = jnp.where(cur, row - lane, row - lane + BLK).astype(F32)
    group = SW_Q_HEADS // SW_KV_HEADS
    units = []
    for s in range(SW_TQ // BLK):
        rs = slice(s * BLK, (s + 1) * BLK)
        for hk in range(SW_KV_HEADS):
            ks = slice(hk * LANES, (hk + 1) * LANES)
            kc, vc = kc_ref[rs, ks], vc_ref[rs, ks]
            if s == 0:
                kp, vp = kp_ref[:, ks], vp_ref[:, ks]
            else:
                kp, vp = kc_ref[(s - 1) * BLK:s * BLK, ks], vc_ref[(s - 1) * BLK:s * BLK, ks]
            pairs = range(hk * group // 2, (hk + 1) * group // 2)
            lhs = []
            for p in pairs:
                qp = q_ref[rs, p * LANES:(p + 1) * LANES]
                zero = jnp.zeros_like(qp)
                lhs += [jnp.where(first, qp, zero), jnp.where(first, zero, qp)]
            lhs = jnp.concatenate(lhs, axis=0)
            units.append(dict(s=s, rs=rs, hk=hk, pairs=pairs, vc=vc, vp=vp,
                              s_cur=_dot_nt(lhs, kc), s_prev=_dot_nt(lhs, kp)))
    for u in units:
        scores = []
        for g in range(group):
            head = u["hk"] * group + g
            gs = slice(g * BLK, (g + 1) * BLK)
            sc = jnp.where(cur, u["s_cur"][gs], u["s_prev"][gs]) - slopes_ref[head] * dist
            if u["s"] == 0:
                sc = jnp.where(jnp.logical_or(cur, ti > 0), sc, -jnp.inf)
            scores.append(sc)
        u["scores"] = scores
        u["m"] = [jnp.maximum(jnp.max(sc, axis=-1, keepdims=True), sinks_ref[layer, u["hk"] * group + g])
                  for g, sc in enumerate(scores)]
    for u in units:
        u["e"] = [jnp.exp(sc - m) for sc, m in zip(u["scores"], u["m"])]
        u["denom"] = [jnp.sum(e, axis=-1, keepdims=True) + jnp.exp(sinks_ref[layer, u["hk"] * group + g] - m)
                      for g, (e, m) in enumerate(zip(u["e"], u["m"]))]
    for u in units:
        prob = [(e / d).astype(BF16) for e, d in zip(u["e"], u["denom"])]
        pz = jnp.zeros_like(prob[0])
        prob_c = jnp.concatenate([jnp.where(cur, p, pz) for p in prob], axis=0)
        prob_p = jnp.concatenate([jnp.where(cur, pz, p) for p in prob], axis=0)
        pv = _dot(prob_c, u["vc"]) + _dot(prob_p, u["vp"])
        for t, p in enumerate(u["pairs"]):
            o_ref[u["rs"], p * LANES:(p + 1) * LANES] = jnp.where(
                first, pv[2 * t * BLK:(2 * t + 1) * BLK], pv[(2 * t + 1) * BLK:(2 * t + 2) * BLK]
            ).astype(o_ref.dtype)


def _sw_attention(pbf, sinks, slopes, layer, batch, seq):
    nt = seq // SW_TQ
    kvw = 2 * SW_KVW
    cq, ck, cv = BF_SWQ // SW_QW, BF_SWK // kvw, BF_SWV // kvw
    per = SW_TQ // BLK

    def prev(b, i):
        return b * nt * per + jnp.maximum(i * per - 1, 0)

    smem = pl.BlockSpec(memory_space=pltpu.SMEM)
    return pl.pallas_call(
        functools.partial(_sw_kernel, layer=layer),
        grid=(batch, nt),
        in_specs=[
            smem, smem,
            pl.BlockSpec((SW_TQ, SW_QW), lambda b, i: (b * nt + i, cq)),
            pl.BlockSpec((BLK, kvw), lambda b, i: (prev(b, i), ck)),
            pl.BlockSpec((SW_TQ, kvw), lambda b, i: (b * nt + i, ck)),
            pl.BlockSpec((BLK, kvw), lambda b, i: (prev(b, i), cv)),
            pl.BlockSpec((SW_TQ, kvw), lambda b, i: (b * nt + i, cv)),
        ],
        out_specs=pl.BlockSpec((SW_TQ, SW_QW), lambda b, i: (b * nt + i, 0)),
        out_shape=jax.ShapeDtypeStruct((batch * seq, SW_QW), BF16),
        compiler_params=pltpu.CompilerParams(
            dimension_semantics=("arbitrary", "arbitrary"), vmem_limit_bytes=VMEM_LIMIT),
        name="sw_attention",
    )(sinks, slopes, pbf, pbf, pbf, pbf, pbf)


GDN_CHUNK = 128


def _split3(x):
    x1 = x.astype(BF16)
    r1 = x - x1.astype(F32)
    x2 = r1.astype(BF16)
    x3 = (r1 - x2.astype(F32)).astype(BF16)
    return x1, x2, x3


def _gdn_kernel(q_ref, k_ref, v_ref, z_ref, ab_ref, conv_ref, small_ref, norm_ref, o_ref,
                carry_ref, state_ref, yq_ref, yk_ref, yv_ref, *, rows):
    n = GDN_CHUNK
    nchunks = rows // n
    step = pl.program_id(1)

    @pl.when(step == 0)
    def _():
        carry_ref[...] = jnp.zeros_like(carry_ref)
        state_ref[...] = jnp.zeros_like(state_ref)

    for idx, (src, dst) in enumerate(((q_ref, yq_ref), (k_ref, yk_ref), (v_ref, yv_ref))):
        cols = slice(idx * GDN_W, (idx + 1) * GDN_W)
        w = conv_ref[:, cols]
        for ci in range(nchunks):
            x = src[ci * n:(ci + 1) * n, :]
            prev8 = carry_ref[:, cols] if ci == 0 else src[ci * n - 8:ci * n, :]
            y = w[GDN_CONV - 1:GDN_CONV, :] * x
            for s in range(1, GDN_CONV):
                y = y + w[GDN_CONV - 1 - s:GDN_CONV - s, :] * _shift_rows(x, prev8, s)
            y = y * jax.nn.sigmoid(y)
            for h in range(GDN_HEADS):
                yh = y[:, h * GDN_HEAD_DIM:(h + 1) * GDN_HEAD_DIM]
                if idx < 2:
                    inv = lax.rsqrt(jnp.sum(yh * yh, axis=-1, keepdims=True) + NORM_EPS)
                    yh = yh * (inv * (GDN_HEAD_DIM ** -0.5) if idx == 0 else inv)
                dst[ci * n:(ci + 1) * n, h * GDN_HEAD_DIM:(h + 1) * GDN_HEAD_DIM] = yh
        carry_ref[:, cols] = src[rows - 8:rows, :]

    r = _iota((n, n), 0)
    c = _iota((n, n), 1)
    causal = r >= c
    strict = r > c
    lower_ones = jnp.where(causal, 1.0, 0.0).astype(BF16)
    head_lane = _iota((n, LANES), 1) < GDN_HEADS

    def sub_blocks(ls):
        return ((r >> (ls + 1)) == (c >> (ls + 1))) & ((r >> ls) != (c >> ls))

    probs = []
    for ci in range(nchunks):
        rs = slice(ci * n, (ci + 1) * n)
        ab = ab_ref[rs, :]
        zab = ab + small_ref[1:2, :]
        softplus = jnp.maximum(zab, 0.0) + jnp.log(1.0 + jnp.exp(-jnp.abs(zab)))
        gbeta = jnp.where(head_lane, -jnp.exp(small_ref[0:1, :]) * softplus, jax.nn.sigmoid(ab))
        gc_all = sum(_dot(lower_ones, part) for part in _split3(gbeta))
        gc_t = gc_all.T
        for h in range(GDN_HEADS):
            hs = slice(h * GDN_HEAD_DIM, (h + 1) * GDN_HEAD_DIM)
            q, k, v = yq_ref[rs, hs], yk_ref[rs, hs], yv_ref[rs, hs]
            gc = jnp.broadcast_to(gc_all[:, h:h + 1], (n, n))
            gc_row = jnp.broadcast_to(gc_t[h:h + 1, :], (n, n))
            g_tot = jnp.broadcast_to(gc[n - 1:n, :], (n, n))
            beta_b = jnp.broadcast_to(gbeta[:, GDN_HEADS + h:GDN_HEADS + h + 1], (n, n))
            decay = jnp.exp(gc - gc_row)
            k_beta = k * beta_b
            k16 = k.astype(BF16)
            e_gc = jnp.exp(gc)
            probs.append(dict(
                rs=rs, hs=hs, h=h,
                lower=jnp.where(strict, _dot_nt(k_beta.astype(BF16), k16) * decay, 0.0),
                a=jnp.where(causal, _dot_nt(q.astype(BF16), k16) * decay, 0.0).astype(BF16),
                qd=(q * e_gc).astype(BF16),
                kdt=(k * jnp.exp(g_tot - gc)).T.astype(BF16),
                gl=jnp.exp(g_tot),
                rhs=jnp.concatenate([v * beta_b, k_beta * e_gc], axis=1)))
    eye = jnp.where(r == c, 1.0, 0.0).astype(F32)
    zero16 = jnp.zeros((n, n), BF16)
    lower16 = [p["lower"].astype(BF16) for p in probs]
    xs = [eye - jnp.where((r >> 1) == (c >> 1), p["lower"], 0.0) for p in probs]
    for ls in range(1, n.bit_length() - 1):
        x16 = [x.astype(BF16) for x in xs]
        ys = [_dot(xb, jnp.where(sub_blocks(ls), l16, zero16)).astype(BF16) for xb, l16 in zip(x16, lower16)]
        xs = [x - _dot(y, xb) for x, y, xb in zip(xs, ys, x16)]
    for p, x in zip(probs, xs):
        uw = _dot(x.astype(BF16), p["rhs"].astype(BF16))
        p["u"] = uw[:, :n]
        p["wq"] = jnp.concatenate([uw[:, n:].astype(BF16), p["qd"]], axis=0)
        p["ak"] = jnp.concatenate([p["a"], p["kdt"]], axis=0)

    gain = norm_ref[...]
    states = [state_ref[h] for h in range(GDN_HEADS)]
    for p in probs:
        h = p["h"]
        sr = _dot(p["wq"], states[h].astype(BF16))
        vn16 = (p["u"] - sr[:n]).astype(BF16)
        vr = _dot(p["ak"], vn16)
        o = sr[n:] + vr[:n]
        states[h] = states[h] * p["gl"] + vr[n:]
        zg = z_ref[p["rs"], p["hs"]]
        o_ref[p["rs"], p["hs"]] = (_rms(o, gain) * (zg * jax.nn.sigmoid(zg))).astype(o_ref.dtype)
    for h in range(GDN_HEADS):
        state_ref[h] = states[h]


def _gdn(pfp, pab, conv_w, small, norm, layer, batch, seq, rows):
    steps = seq // rows
    w = GDN_W
    cq = FP_CQKV // w

    def blk(col):
        return pl.BlockSpec((rows, w), lambda b, i: (b * steps + i, col))

    return pl.pallas_call(
        functools.partial(_gdn_kernel, rows=rows),
        grid=(batch, steps),
        in_specs=[
            blk(cq), blk(cq + 1), blk(cq + 2), blk(FP_CZ // w),
            pl.BlockSpec((rows, LANES), lambda b, i: (b * steps + i, 0)),
            pl.BlockSpec((None, GDN_CONV, 3 * w), lambda b, i: (layer, 0, 0)),
            pl.BlockSpec((None, 8, LANES), lambda b, i: (layer, 0, 0)),
            pl.BlockSpec((None, 1, GDN_HEAD_DIM), lambda b, i: (layer, 0, 0)),
        ],
        out_specs=pl.BlockSpec((rows, w), lambda b, i: (b * steps + i, 0)),
        out_shape=jax.ShapeDtypeStruct((batch * seq, w), BF16),
        scratch_shapes=[
            pltpu.VMEM((8, 3 * w), F32),
            pltpu.VMEM((GDN_HEADS, GDN_HEAD_DIM, GDN_HEAD_DIM), F32),
            pltpu.VMEM((rows, w), F32), pltpu.VMEM((rows, w), F32), pltpu.VMEM((rows, w), F32),
        ],
        compiler_params=pltpu.CompilerParams(
            dimension_semantics=("arbitrary", "arbitrary"), vmem_limit_bytes=VMEM_LIMIT),
        name="gdn",
    )(pfp, pfp, pfp, pfp, pab, conv_w, small, norm)


def _merge_kernel(ya_ref, yb_ref, yc_ref, ga_ref, gb_ref, gc_ref, x_ref,
                  wa_ref, wb_ref, wc_ref, wo_ref, gain_ref, o_ref):
    merged = (jax.nn.sigmoid(ga_ref[...].astype(F32)) * _dot(ya_ref[...], wa_ref[...])
              + jax.nn.sigmoid(gb_ref[...].astype(F32)) * _dot(yb_ref[...], wb_ref[...])
              + jax.nn.sigmoid(gc_ref[...].astype(F32)) * _dot(yc_ref[...], wc_ref[...]))
    r = _dot(merged.astype(BF16), wo_ref[...])
    o_ref[...] = x_ref[...] + _rms(r, gain_ref[...])


def _merge(ya, yb, yc, pbf, x, wa, wb, wc, wo, gain, layer, tm):
    t, d = x.shape

    def rows(width, col=0):
        return pl.BlockSpec((tm, width), lambda i: (i, col))

    def whole(a):
        return pl.BlockSpec((None,) + a.shape[1:], lambda i: (layer, 0, 0))

    return pl.pallas_call(
        _merge_kernel,
        grid=(t // tm,),
        in_specs=[rows(SB_W), rows(SW_QW), rows(GDN_W),
                  rows(d, BF_GA // d), rows(d, BF_GB // d), rows(d, BF_GC // d), rows(d),
                  whole(wa), whole(wb), whole(wc), whole(wo), whole(gain)],
        out_specs=rows(d),
        out_shape=jax.ShapeDtypeStruct((t, d), F32),
        compiler_params=pltpu.CompilerParams(
            dimension_semantics=("arbitrary",), vmem_limit_bytes=VMEM_LIMIT),
        name="merge",
    )(ya, yb, yc, pbf, pbf, pbf, x, wa, wb, wc, wo, gain)


FFN_CK = 256
SQRT_2_OVER_PI = 0.7978845608028654


def _gelu_tanh(x):
    inner = x * (SQRT_2_OVER_PI + (SQRT_2_OVER_PI * 0.044715) * (x * x))
    return (0.5 * x) * (1.0 + jnp.tanh(inner))


def _ffn_kernel(x_ref, gpre_ref, wup_ref, conv_ref, wdn_ref, gpost_ref, o_ref, carry_ref, f_ref, *, tm):
    @pl.when(pl.program_id(1) == 0)
    def _():
        carry_ref[...] = jnp.zeros_like(carry_ref)

    x = x_ref[...]
    hn = _rms(x, gpre_ref[...]).astype(BF16)

    def conv(col):
        cs = slice(col, col + FFN_CK)
        hid = _dot(hn, wup_ref[:, cs])
        prev8 = carry_ref[:, cs]
        w = conv_ref[:, cs]
        y = w[FFN_CONV - 1:FFN_CONV, :] * hid
        for s in range(1, FFN_CONV):
            y = y + w[FFN_CONV - 1 - s:FFN_CONV - s, :] * _shift_rows(hid, prev8, s)
        carry_ref[:, cs] = hid[tm - 8:tm, :]
        return y

    for ci in range(D_FF // FFN_CK):
        f_gate = conv(ci * FFN_CK)
        f_up = conv(D_FF + ci * FFN_CK)
        f_ref[:, ci * FFN_CK:(ci + 1) * FFN_CK] = (_gelu_tanh(f_gate) * f_up).astype(BF16)
    o_ref[...] = x + _rms(_dot(f_ref[...], wdn_ref[...]), gpost_ref[...])


def _ffn(x, gpre, wup, conv_w, wdn, gpost, layer, batch, seq, tm):
    steps = seq // tm
    d = x.shape[1]

    def whole(a):
        return pl.BlockSpec((None,) + a.shape[1:], lambda b, i: (layer, 0, 0))

    return pl.pallas_call(
        functools.partial(_ffn_kernel, tm=tm),
        grid=(batch, steps),
        in_specs=[pl.BlockSpec((tm, d), lambda b, i: (b * steps + i, 0)),
                  whole(gpre), whole(wup), whole(conv_w), whole(wdn), whole(gpost)],
        out_specs=pl.BlockSpec((tm, d), lambda b, i: (b * steps + i, 0)),
        out_shape=jax.ShapeDtypeStruct(x.shape, F32),
        scratch_shapes=[pltpu.VMEM((8, 2 * D_FF), F32), pltpu.VMEM((tm, D_FF), BF16)],
        compiler_params=pltpu.CompilerParams(
            dimension_semantics=("arbitrary", "arbitrary"), vmem_limit_bytes=VMEM_LIMIT),
        name="ffn",
    )(x, gpre, wup, conv_w, wdn, gpost)


def _pack_w_in(w):
    w = jnp.swapaxes(w, 1, 2).astype(BF16)
    o = 0
    parts = {}
    for name, width in (("aq", SB_W), ("ak", SB_W), ("av", SB_W), ("bq", SW_QW), ("bk", SW_KVW),
                        ("bv", SW_KVW), ("cqkv", 3 * GDN_W), ("cz", GDN_W), ("ca", GDN_HEADS),
                        ("cb", GDN_HEADS), ("ga", D_MODEL), ("gb", D_MODEL), ("gc", D_MODEL)):
        parts[name] = w[:, o:o + width, :]
        o += width

    def dup(t):
        return jnp.concatenate([t[:, h * HALF:(h + 1) * HALF, :] for h in range(SW_KV_HEADS) for _ in range(2)], axis=1)

    cols = [parts["ga"], parts["gb"], parts["gc"],
            parts["aq"] * (SB_HEAD_DIM ** -0.5), parts["ak"], parts["av"],
            parts["bq"] * (SW_HEAD_DIM ** -0.5), dup(parts["bk"]), dup(parts["bv"]),
            parts["cqkv"], parts["cz"]]
    pad = jnp.zeros((w.shape[0], LANES - 2 * GDN_HEADS, w.shape[2]), w.dtype)
    wab = jnp.concatenate([parts["ca"], parts["cb"], pad], axis=1)
    return jnp.concatenate(cols, axis=1).astype(BF16), wab


def _layer(x, batch, seq, layer, p):
    pbf, pfp, pab = _inproj(x, p["ln_mix_pre"], p["w_in"], p["w_ab"], layer, tm=512)
    ya = _sb_attention(pbf, batch, seq)
    yb = _sw_attention(pbf, p["sinks"], p["slopes"], layer, batch, seq)
    yc = _gdn(pfp, pab, p["gdn_conv"], p["gdn_small"], p["gdn_norm"], layer, batch, seq, rows=512)
    x = _merge(ya, yb, yc, pbf, x, p["wa"], p["wb"], p["wc"], p["wo"], p["ln_mix_post"], layer, tm=1024)
    return _ffn(x, p["ln_ffn_pre"], p["w_up"], p["ffn_conv"], p["w_down"], p["ln_ffn_post"],
                layer, batch, seq, tm=512)


def kernel(x, ln_mix_pre, w_in, sw_sinks, gdn_conv, gdn_a_log, gdn_dt_bias, gdn_norm, w_branch_a,
           w_branch_b, w_branch_c, w_out, ln_mix_post, ln_ffn_pre, w_up, ffn_conv, w_down, ln_ffn_post):
    batch, seq, d = x.shape
    depth = w_in.shape[0]
    small = jnp.zeros((depth, 8, LANES), F32)
    small = small.at[:, 0, :GDN_HEADS].set(gdn_a_log).at[:, 1, :GDN_HEADS].set(gdn_dt_bias)
    w_main, w_ab = _pack_w_in(w_in)
    p = {
        "ln_mix_pre": ln_mix_pre[:, None, :], "w_in": w_main, "w_ab": w_ab,
        "sinks": sw_sinks, "slopes": jnp.exp2(-8.0 * jnp.arange(1, SW_Q_HEADS + 1, dtype=F32) / SW_Q_HEADS),
        "gdn_conv": gdn_conv, "gdn_small": small, "gdn_norm": gdn_norm[:, None, :],
        "wa": w_branch_a.astype(BF16), "wb": w_branch_b.astype(BF16),
        "wc": w_branch_c.astype(BF16), "wo": w_out.astype(BF16),
        "ln_mix_post": ln_mix_post[:, None, :], "ln_ffn_pre": ln_ffn_pre[:, None, :],
        "w_up": w_up.astype(BF16), "ffn_conv": ffn_conv, "w_down": w_down.astype(BF16),
        "ln_ffn_post": ln_ffn_post[:, None, :],
    }
    h = x.reshape(batch * seq, d)
    for layer in range(depth):
        h = _layer(h, batch, seq, layer, p)
    return h.reshape(batch, seq, d)
```

```python
import functools

import jax
import jax.numpy as jnp
from jax import lax
from jax.experimental import pallas as pl
from jax.experimental.pallas import tpu as pltpu

F32 = jnp.float32
BF16 = jnp.bfloat16
NORM_EPS = 1e-6

D_MODEL = 1024
SB_HEADS, SB_HEAD_DIM = 8, 64
SW_Q_HEADS, SW_KV_HEADS, SW_HEAD_DIM = 8, 2, 64
GDN_HEADS, GDN_HEAD_DIM, GDN_CONV = 4, 128, 4
D_FF, FFN_CONV = 2816, 3
SB_W = SB_HEADS * SB_HEAD_DIM
SW_QW = SW_Q_HEADS * SW_HEAD_DIM
SW_KVW = SW_KV_HEADS * SW_HEAD_DIM
GDN_W = GDN_HEADS * GDN_HEAD_DIM

BLK = 128
LANES = 128
HALF = 64
VMEM_LIMIT = 56 * 1024 * 1024

BF_GA, BF_GB, BF_GC, BF_SBQ, BF_SBK, BF_SBV, BF_SWQ, BF_SWK, BF_SWV, BF_COLS = (
    0, 1024, 2048, 3072, 3584, 4096, 4608, 5120, 5376, 5632)
FP_CQKV, FP_CZ, FP_COLS = 0, 1536, 2048
PROJ_TN = 512
LOG2E = 1.4426950408889634
F32_EXP2_ZERO = -104.0 * LOG2E


def _dot(a, b):
    return jnp.dot(a, b, preferred_element_type=F32)


def _dot_nt(a, b):
    return lax.dot_general(a, b, (((1,), (1,)), ((), ())), preferred_element_type=F32)


def _iota(shape, dim):
    return lax.broadcasted_iota(jnp.int32, shape, dim)


def _rms(t, gain):
    return t * lax.rsqrt(jnp.mean(t * t, axis=-1, keepdims=True) + NORM_EPS) * gain


def _shift_rows(h, prev8, s):
    r = pltpu.roll(h, s, axis=0)
    row = _iota(h.shape, 0)
    for t in range(s):
        r = jnp.where(row == t, prev8[8 - s + t:8 - s + t + 1, :], r)
    return r


def _inproj_kernel(x_ref, g_ref, w_ref, wab_ref, obf_ref, of_ref, oab_ref):
    xn = _rms(x_ref[...], g_ref[...]).astype(BF16)
    oab_ref[...] = _dot_nt(xn, wab_ref[...])
    tn = PROJ_TN
    for j in range(BF_COLS // tn):
        obf_ref[:, j * tn:(j + 1) * tn] = _dot_nt(xn, w_ref[j * tn:(j + 1) * tn, :]).astype(BF16)
    for j in range(FP_COLS // tn):
        of_ref[:, j * tn:(j + 1) * tn] = _dot_nt(xn, w_ref[BF_COLS + j * tn:BF_COLS + (j + 1) * tn, :])


def _inproj(x, gain, w, wab, layer, tm):
    t, d = x.shape

    def whole(a):
        return pl.BlockSpec((None,) + a.shape[1:], lambda i: (layer, 0, 0), pipeline_mode=pl.Buffered(1))

    return pl.pallas_call(
        _inproj_kernel,
        grid=(t // tm,),
        in_specs=[pl.BlockSpec((tm, d), lambda i: (i, 0)), whole(gain), whole(w), whole(wab)],
        out_specs=[
            pl.BlockSpec((tm, BF_COLS), lambda i: (i, 0)),
            pl.BlockSpec((tm, FP_COLS), lambda i: (i, 0)),
            pl.BlockSpec((tm, LANES), lambda i: (i, 0)),
        ],
        out_shape=[jax.ShapeDtypeStruct((t, BF_COLS), BF16), jax.ShapeDtypeStruct((t, FP_COLS), F32),
                   jax.ShapeDtypeStruct((t, LANES), F32)],
        compiler_params=pltpu.CompilerParams(
            dimension_semantics=("arbitrary",), vmem_limit_bytes=VMEM_LIMIT),
        name="inproj",
    )(x, gain, w, wab)


SB_TQ = 2 * BLK
SB_PAIRS = 2


def _sb_kernel(q_ref, k_ref, v_ref, o_ref, acc_ref, lr_ref):
    ti = pl.program_id(2)
    lane = _iota((BLK, LANES), 1)
    row = _iota((BLK, LANES), 0)
    first = lane < HALF
    tri = lane < row
    full = lane >= 0
    zero_tile = jnp.zeros((BLK, LANES), F32)
    r2 = _iota((BLK, 2 * LANES), 0)
    c2 = _iota((BLK, 2 * LANES), 1)
    suffix_ones = jnp.where((r2 > c2) | (c2 >= LANES), 1.0, 0.0).astype(BF16)

    def split_heads(q):
        sel = jnp.concatenate([first] * (q.shape[0] // BLK), axis=0)
        zero = jnp.zeros_like(q)
        return jnp.where(sel, q, zero), jnp.where(sel, zero, q)

    def log_parts(z):
        zs = z * LOG2E
        log_stay = -(jnp.maximum(zs, 0.0) + jnp.log2(1.0 + jnp.exp2(-jnp.abs(zs))))
        return log_stay, log_stay + zs

    def suffix_sums(parts):
        sums = _dot(jnp.concatenate([x.astype(BF16) for x in parts], axis=0), suffix_ones)
        out, o = [], 0
        for x in parts:
            out.append((sums[o:o + x.shape[0], :LANES], sums[o:o + x.shape[0], LANES:]))
            o += x.shape[0]
        return out

    def generic_block(pp, s, j, valid):
        ps = slice(pp * LANES, (pp + 1) * LANES)
        q0, q1 = split_heads(q_ref[s * BLK:(s + 1) * BLK, ps])
        off = pl.multiple_of(j * BLK, BLK)
        k, v = k_ref[pl.ds(off, BLK), ps], v_ref[pl.ds(off, BLK), ps]
        log_stay, log_take = log_parts(_dot_nt(jnp.concatenate([q0, q1], axis=0), k))
        if valid is not None:
            vm = jnp.concatenate([valid, valid], axis=0)
            log_stay = jnp.where(vm, log_stay, 0.0)
        (later, total), = suffix_sums([log_stay])
        w = jnp.exp2(log_take + later + lr_ref[pp, s])
        if valid is not None:
            w = jnp.where(vm, w, 0.0)
        pv = _dot(w.astype(BF16), v)
        acc_ref[pp, s] += jnp.where(first, pv[:BLK], pv[BLK:])
        lr_ref[pp, s] += total

    def tail(pp, s, j0):
        def cond(c):
            j, live = c
            return jnp.logical_and(j >= 0, live)

        def body(c):
            j, _ = c
            generic_block(pp, s, j, None)
            return j - 1, jnp.max(lr_ref[pp, s]) > F32_EXP2_ZERO

        lax.while_loop(cond, body, (j0, jnp.max(lr_ref[pp, s]) > F32_EXP2_ZERO))

    @pl.when(ti == 0)
    def _():
        for pp in range(SB_PAIRS):
            for s in range(2):
                acc_ref[pp, s] = jnp.zeros((BLK, LANES), F32)
                lr_ref[pp, s] = jnp.zeros((2 * BLK, LANES), F32)
                generic_block(pp, s, 2 * ti + s, tri)
                tail(pp, s, 2 * ti + s - 1)

    @pl.when(ti > 0)
    def _():
        base = pl.multiple_of((2 * ti - 2) * BLK, BLK)
        m2 = jnp.concatenate([tri, full, tri, full], axis=0)
        m3 = jnp.concatenate([tri, tri], axis=0)
        P = range(SB_PAIRS)
        ps = [slice(pp * LANES, (pp + 1) * LANES) for pp in P]
        qs = [split_heads(q_ref[:, ps[pp]]) for pp in P]
        kw = [k_ref[pl.ds(base, 4 * BLK), ps[pp]] for pp in P]
        vw = [v_ref[pl.ds(base, 4 * BLK), ps[pp]] for pp in P]
        l_ab = [jnp.concatenate([q0, q1], axis=0) for q0, q1 in qs]
        l_a = [jnp.concatenate([q0[:BLK], q1[:BLK]], axis=0) for q0, q1 in qs]
        l_b = [jnp.concatenate([q0[BLK:], q1[BLK:]], axis=0) for q0, q1 in qs]
        z3 = [_dot_nt(l_b[pp], kw[pp][3 * BLK:]) for pp in P]
        z2 = [_dot_nt(l_ab[pp], kw[pp][2 * BLK:3 * BLK]) for pp in P]
        z1 = [_dot_nt(l_ab[pp], kw[pp][BLK:2 * BLK]) for pp in P]
        z0 = [_dot_nt(l_a[pp], kw[pp][:BLK]) for pp in P]
        lp3 = [log_parts(z) for z in z3]
        s3 = [suffix_sums([jnp.where(m3, ls, 0.0)])[0] for ls, _ in lp3]
        lp2 = [log_parts(z) for z in z2]
        s2 = [suffix_sums([jnp.where(m2, ls, 0.0)])[0] for ls, _ in lp2]
        lp1 = [log_parts(z) for z in z1]
        s1 = [suffix_sums([ls])[0] for ls, _ in lp1]
        lp0 = [log_parts(z) for z in z0]
        s0 = [suffix_sums([ls])[0] for ls, _ in lp0]
        for pp in P:
            (lat3, tot3), (lat2, tot2), (lat1, tot1), (lat0, tot0) = s3[pp], s2[pp], s1[pp], s0[pp]
            lt3, lt2, lt1, lt0 = lp3[pp][1], lp2[pp][1], lp1[pp][1], lp0[pp][1]
            lrb2 = jnp.concatenate([zero_tile, tot3[:BLK], zero_tile, tot3[BLK:]], axis=0)
            lrb1 = lrb2 + tot2
            lre1 = lrb1 + tot1
            lrb0 = jnp.concatenate([lre1[:BLK], lre1[2 * BLK:3 * BLK]], axis=0)
            w3 = jnp.where(m3, jnp.exp2(lt3 + lat3), 0.0)
            w2 = jnp.where(m2, jnp.exp2(lt2 + lat2 + lrb2), 0.0)
            w1 = jnp.exp2(lt1 + lat1 + lrb1)
            w0 = jnp.exp2(lt0 + lat0 + lrb0)
            w12 = jnp.concatenate([w1.astype(BF16), w2.astype(BF16)], axis=1)
            pv12 = _dot(w12, vw[pp][BLK:3 * BLK])
            pv0 = _dot(w0.astype(BF16), vw[pp][:BLK])
            pv3 = _dot(w3.astype(BF16), vw[pp][3 * BLK:])
            acc_ref[pp, 0] = jnp.where(first, pv12[:BLK] + pv0[:BLK], pv12[2 * BLK:3 * BLK] + pv0[BLK:])
            acc_ref[pp, 1] = jnp.where(first, pv12[BLK:2 * BLK] + pv3[:BLK], pv12[3 * BLK:] + pv3[BLK:])
            lr_ref[pp, 0] = lrb0 + tot0
            lr_ref[pp, 1] = jnp.concatenate([lre1[BLK:2 * BLK], lre1[3 * BLK:]], axis=0)

        @pl.when(jnp.max(lr_ref[...]) > F32_EXP2_ZERO)
        def _():
            for pp in range(SB_PAIRS):
                tail(pp, 0, 2 * ti - 3)
                tail(pp, 1, 2 * ti - 2)

    for pp in range(SB_PAIRS):
        for s in range(2):
            o_ref[s * BLK:(s + 1) * BLK, pp * LANES:(pp + 1) * LANES] = acc_ref[pp, s].astype(o_ref.dtype)


def _sb_attention(pbf, batch, seq):
    nt = seq // SB_TQ
    w = SB_PAIRS * LANES
    cq, ck, cv = BF_SBQ // w, BF_SBK // w, BF_SBV // w
    return pl.pallas_call(
        _sb_kernel,
        grid=(batch, SB_W // w, nt),
        in_specs=[
            pl.BlockSpec((SB_TQ, w), lambda b, p, i: (b * nt + i, cq + p)),
            pl.BlockSpec((seq, w), lambda b, p, i: (b, ck + p)),
            pl.BlockSpec((seq, w), lambda b, p, i: (b, cv + p)),
        ],
        out_specs=pl.BlockSpec((SB_TQ, w), lambda b, p, i: (b * nt + i, p)),
        out_shape=jax.ShapeDtypeStruct((batch * seq, SB_W), BF16),
        scratch_shapes=[pltpu.VMEM((SB_PAIRS, 2, BLK, LANES), F32),
                        pltpu.VMEM((SB_PAIRS, 2, 2 * BLK, LANES), F32)],
        compiler_params=pltpu.CompilerParams(
            dimension_semantics=("arbitrary", "arbitrary", "arbitrary"), vmem_limit_bytes=VMEM_LIMIT),
        name="sb_attention",
    )(pbf, pbf, pbf)


SW_TQ = 4 * BLK


def _sw_kernel(sinks_ref, slopes_ref, q_ref, kp_ref, kc_ref, vp_ref, vc_ref, o_ref, *, layer):
    ti = pl.program_id(1)
    lane = _iota((BLK, LANES), 1)
    row = _iota((BLK, LANES), 0)
    first = lane < HALF
    cur = lane <= row
    dist = jnp.where(cur, row - lane, row - lane + BLK).astype(F32)
    group = SW_Q_HEADS // SW_KV_HEADS
    units = []
    for s in range(SW_TQ // BLK):
        rs = slice(s * BLK, (s + 1) * BLK)
        for hk in range(SW_KV_HEADS):
            ks = slice(hk * LANES, (hk + 1) * LANES)
            kc, vc = kc_ref[rs, ks], vc_ref[rs, ks]
            if s == 0:
                kp, vp = kp_ref[:, ks], vp_ref[:, ks]
            else:
                kp, vp = kc_ref[(s - 1) * BLK:s * BLK, ks], vc_ref[(s - 1) * BLK:s * BLK, ks]
            pairs = range(hk * group // 2, (hk + 1) * group // 2)
            lhs = []
            for p in pairs:
                qp = q_ref[rs, p * LANES:(p + 1) * LANES]
                zero = jnp.zeros_like(qp)
                lhs += [jnp.where(first, qp, zero), jnp.where(first, zero, qp)]
            lhs = jnp.concatenate(lhs, axis=0)
            units.append(dict(s=s, rs=rs, hk=hk, pairs=pairs, vc=vc, vp=vp,
                              s_cur=_dot_nt(lhs, kc), s_prev=_dot_nt(lhs, kp)))
    for u in units:
        scores = []
        for g in range(group):
            head = u["hk"] * group + g
            gs = slice(g * BLK, (g + 1) * BLK)
            sc = jnp.where(cur, u["s_cur"][gs], u["s_prev"][gs]) - slopes_ref[head] * dist
            if u["s"] == 0:
                sc = jnp.where(jnp.logical_or(cur, ti > 0), sc, -jnp.inf)
            scores.append(sc)
        u["scores"] = scores
        u["m"] = [jnp.maximum(jnp.max(sc, axis=-1, keepdims=True), sinks_ref[layer, u["hk"] * group + g])
                  for g, sc in enumerate(scores)]
    for u in units:
        u["e"] = [jnp.exp(sc - m) for sc, m in zip(u["scores"], u["m"])]
        u["denom"] = [jnp.sum(e, axis=-1, keepdims=True) + jnp.exp(sinks_ref[layer, u["hk"] * group + g] - m)
                      for g, (e, m) in enumerate(zip(u["e"], u["m"]))]
    for u in units:
        prob = [(e / d).astype(BF16) for e, d in zip(u["e"], u["denom"])]
        pz = jnp.zeros_like(prob[0])
        prob_c = jnp.concatenate([jnp.where(cur, p, pz) for p in prob], axis=0)
        prob_p = jnp.concatenate([jnp.where(cur, pz, p) for p in prob], axis=0)
        pv = _dot(prob_c, u["vc"]) + _dot(prob_p, u["vp"])
        for t, p in enumerate(u["pairs"]):
            o_ref[u["rs"], p * LANES:(p + 1) * LANES] = jnp.where(
                first, pv[2 * t * BLK:(2 * t + 1) * BLK], pv[(2 * t + 1) * BLK:(2 * t + 2) * BLK]
            ).astype(o_ref.dtype)


def _sw_attention(pbf, sinks, slopes, layer, batch, seq):
    nt = seq // SW_TQ
    kvw = 2 * SW_KVW
    cq, ck, cv = BF_SWQ // SW_QW, BF_SWK // kvw, BF_SWV // kvw
    per = SW_TQ // BLK

    def prev(b, i):
        return b * nt * per + jnp.maximum(i * per - 1, 0)

    smem = pl.BlockSpec(memory_space=pltpu.SMEM)
    return pl.pallas_call(
        functools.partial(_sw_kernel, layer=layer),
        grid=(batch, nt),
        in_specs=[
            smem, smem,
            pl.BlockSpec((SW_TQ, SW_QW), lambda b, i: (b * nt + i, cq)),
            pl.BlockSpec((BLK, kvw), lambda b, i: (prev(b, i), ck)),
            pl.BlockSpec((SW_TQ, kvw), lambda b, i: (b * nt + i, ck)),
            pl.BlockSpec((BLK, kvw), lambda b, i: (prev(b, i), cv)),
            pl.BlockSpec((SW_TQ, kvw), lambda b, i: (b * nt + i, cv)),
        ],
        out_specs=pl.BlockSpec((SW_TQ, SW_QW), lambda b, i: (b * nt + i, 0)),
        out_shape=jax.ShapeDtypeStruct((batch * seq, SW_QW), BF16),
        compiler_params=pltpu.CompilerParams(
            dimension_semantics=("arbitrary", "arbitrary"), vmem_limit_bytes=VMEM_LIMIT),
        name="sw_attention",
    )(sinks, slopes, pbf, pbf, pbf, pbf, pbf)


GDN_CHUNK = 128


def _split3(x):
    x1 = x.astype(BF16)
    r1 = x - x1.astype(F32)
    x2 = r1.astype(BF16)
    x3 = (r1 - x2.astype(F32)).astype(BF16)
    return x1, x2, x3


def _gdn_kernel(q_ref, k_ref, v_ref, z_ref, ab_ref, conv_ref, small_ref, norm_ref, o_ref,
                carry_ref, state_ref, yq_ref, yk_ref, yv_ref, *, rows):
    n = GDN_CHUNK
    nchunks = rows // n
    step = pl.program_id(1)

    @pl.when(step == 0)
    def _():
        carry_ref[...] = jnp.zeros_like(carry_ref)
        state_ref[...] = jnp.zeros_like(state_ref)

    for idx, (src, dst) in enumerate(((q_ref, yq_ref), (k_ref, yk_ref), (v_ref, yv_ref))):
        cols = slice(idx * GDN_W, (idx + 1) * GDN_W)
        w = conv_ref[:, cols]
        for ci in range(nchunks):
            x = src[ci * n:(ci + 1) * n, :]
            prev8 = carry_ref[:, cols] if ci == 0 else src[ci * n - 8:ci * n, :]
            y = w[GDN_CONV - 1:GDN_CONV, :] * x
            for s in range(1, GDN_CONV):
                y = y + w[GDN_CONV - 1 - s:GDN_CONV - s, :] * _shift_rows(x, prev8, s)
            y = y * jax.nn.sigmoid(y)
            for h in range(GDN_HEADS):
                yh = y[:, h * GDN_HEAD_DIM:(h + 1) * GDN_HEAD_DIM]
                if idx < 2:
                    inv = lax.rsqrt(jnp.sum(yh * yh, axis=-1, keepdims=True) + NORM_EPS)
                    yh = yh * (inv * (GDN_HEAD_DIM ** -0.5) if idx == 0 else inv)
                dst[ci * n:(ci + 1) * n, h * GDN_HEAD_DIM:(h + 1) * GDN_HEAD_DIM] = yh
        carry_ref[:, cols] = src[rows - 8:rows, :]

    r = _iota((n, n), 0)
    c = _iota((n, n), 1)
    causal = r >= c
    strict = r > c
    lower_ones = jnp.where(causal, 1.0, 0.0).astype(BF16)
    head_lane = _iota((n, LANES), 1) < GDN_HEADS

    def sub_blocks(ls):
        return ((r >> (ls + 1)) == (c >> (ls + 1))) & ((r >> ls) != (c >> ls))

    probs = []
    for ci in range(nchunks):
        rs = slice(ci * n, (ci + 1) * n)
        ab = ab_ref[rs, :]
        zab = ab + small_ref[1:2, :]
        softplus = jnp.maximum(zab, 0.0) + jnp.log(1.0 + jnp.exp(-jnp.abs(zab)))
        gbeta = jnp.where(head_lane, -jnp.exp(small_ref[0:1, :]) * softplus, jax.nn.sigmoid(ab))
        gc_all = sum(_dot(lower_ones, part) for part in _split3(gbeta))
        gc_t = gc_all.T
        for h in range(GDN_HEADS):
            hs = slice(h * GDN_HEAD_DIM, (h + 1) * GDN_HEAD_DIM)
            q, k, v = yq_ref[rs, hs], yk_ref[rs, hs], yv_ref[rs, hs]
            gc = jnp.broadcast_to(gc_all[:, h:h + 1], (n, n))
            gc_row = jnp.broadcast_to(gc_t[h:h + 1, :], (n, n))
            g_tot = jnp.broadcast_to(gc[n - 1:n, :], (n, n))
            beta_b = jnp.broadcast_to(gbeta[:, GDN_HEADS + h:GDN_HEADS + h + 1], (n, n))
            decay = jnp.exp(gc - gc_row)
            k_beta = k * beta_b
            k16 = k.astype(BF16)
            e_gc = jnp.exp(gc)
            probs.append(dict(
                rs=rs, hs=hs, h=h,
                lower=jnp.where(strict, _dot_nt(k_beta.astype(BF16), k16) * decay, 0.0),
                a=jnp.where(causal, _dot_nt(q.astype(BF16), k16) * decay, 0.0).astype(BF16),
                qd=(q * e_gc).astype(BF16),
                kdt=(k * jnp.exp(g_tot - gc)).T.astype(BF16),
                gl=jnp.exp(g_tot),
                rhs=jnp.concatenate([v * beta_b, k_beta * e_gc], axis=1)))
    eye = jnp.where(r == c, 1.0, 0.0).astype(F32)
    zero16 = jnp.zeros((n, n), BF16)
    lower16 = [p["lower"].astype(BF16) for p in probs]
    xs = [eye - jnp.where((r >> 1) == (c >> 1), p["lower"], 0.0) for p in probs]
    for ls in range(1, n.bit_length() - 1):
        x16 = [x.astype(BF16) for x in xs]
        ys = [_dot(xb, jnp.where(sub_blocks(ls), l16, zero16)).astype(BF16) for xb, l16 in zip(x16, lower16)]
        xs = [x - _dot(y, xb) for x, y, xb in zip(xs, ys, x16)]
    for p, x in zip(probs, xs):
        uw = _dot(x.astype(BF16), p["rhs"].astype(BF16))
        p["u"] = uw[:, :n]
        p["wq"] = jnp.concatenate([uw[:, n:].astype(BF16), p["qd"]], axis=0)
        p["ak"] = jnp.concatenate([p["a"], p["kdt"]], axis=0)

    gain = norm_ref[...]
    states = [state_ref[h] for h in range(GDN_HEADS)]
    for p in probs:
        h = p["h"]
        sr = _dot(p["wq"], states[h].astype(BF16))
        vn16 = (p["u"] - sr[:n]).astype(BF16)
        vr = _dot(p["ak"], vn16)
        o = sr[n:] + vr[:n]
        states[h] = states[h] * p["gl"] + vr[n:]
        zg = z_ref[p["rs"], p["hs"]]
        o_ref[p["rs"], p["hs"]] = (_rms(o, gain) * (zg * jax.nn.sigmoid(zg))).astype(o_ref.dtype)
    for h in range(GDN_HEADS):
        state_ref[h] = states[h]


def _gdn(pfp, pab, conv_w, small, norm, layer, batch, seq, rows):
    steps = seq // rows
    w = GDN_W
    cq = FP_CQKV // w

    def blk(col):
        return pl.BlockSpec((rows, w), lambda b, i: (b * steps + i, col))

    return pl.pallas_call(
        functools.partial(_gdn_kernel, rows=rows),
        grid=(batch, steps),
        in_specs=[
            blk(cq), blk(cq + 1), blk(cq + 2), blk(FP_CZ // w),
            pl.BlockSpec((rows, LANES), lambda b, i: (b * steps + i, 0)),
            pl.BlockSpec((None, GDN_CONV, 3 * w), lambda b, i: (layer, 0, 0)),
            pl.BlockSpec((None, 8, LANES), lambda b, i: (layer, 0, 0)),
            pl.BlockSpec((None, 1, GDN_HEAD_DIM), lambda b, i: (layer, 0, 0)),
        ],
        out_specs=pl.BlockSpec((rows, w), lambda b, i: (b * steps + i, 0)),
        out_shape=jax.ShapeDtypeStruct((batch * seq, w), BF16),
        scratch_shapes=[
            pltpu.VMEM((8, 3 * w), F32),
            pltpu.VMEM((GDN_HEADS, GDN_HEAD_DIM, GDN_HEAD_DIM), F32),
            pltpu.VMEM((rows, w), F32), pltpu.VMEM((rows, w), F32), pltpu.VMEM((rows, w), F32),
        ],
        compiler_params=pltpu.CompilerParams(
            dimension_semantics=("arbitrary", "arbitrary"), vmem_limit_bytes=VMEM_LIMIT),
        name="gdn",
    )(pfp, pfp, pfp, pfp, pab, conv_w, small, norm)


def _merge_kernel(ya_ref, yb_ref, yc_ref, ga_ref, gb_ref, gc_ref, x_ref,
                  wa_ref, wb_ref, wc_ref, wo_ref, gain_ref, o_ref):
    merged = (jax.nn.sigmoid(ga_ref[...].astype(F32)) * _dot(ya_ref[...], wa_ref[...])
              + jax.nn.sigmoid(gb_ref[...].astype(F32)) * _dot(yb_ref[...], wb_ref[...])
              + jax.nn.sigmoid(gc_ref[...].astype(F32)) * _dot(yc_ref[...], wc_ref[...]))
    r = _dot(merged.astype(BF16), wo_ref[...])
    o_ref[...] = x_ref[...] + _rms(r, gain_ref[...])


def _merge(ya, yb, yc, pbf, x, wa, wb, wc, wo, gain, layer, tm):
    t, d = x.shape

    def rows(width, col=0):
        return pl.BlockSpec((tm, width), lambda i: (i, col))

    def whole(a):
        return pl.BlockSpec((None,) + a.shape[1:], lambda i: (layer, 0, 0))

    return pl.pallas_call(
        _merge_kernel,
        grid=(t // tm,),
        in_specs=[rows(SB_W), rows(SW_QW), rows(GDN_W),
                  rows(d, BF_GA // d), rows(d, BF_GB // d), rows(d, BF_GC // d), rows(d),
                  whole(wa), whole(wb), whole(wc), whole(wo), whole(gain)],
        out_specs=rows(d),
        out_shape=jax.ShapeDtypeStruct((t, d), F32),
        compiler_params=pltpu.CompilerParams(
            dimension_semantics=("arbitrary",), vmem_limit_bytes=VMEM_LIMIT),
        name="merge",
    )(ya, yb, yc, pbf, pbf, pbf, x, wa, wb, wc, wo, gain)


FFN_CK = 256
SQRT_2_OVER_PI = 0.7978845608028654


def _gelu_tanh(x):
    inner = x * (SQRT_2_OVER_PI + (SQRT_2_OVER_PI * 0.044715) * (x * x))
    return (0.5 * x) * (1.0 + jnp.tanh(inner))


def _ffn_kernel(x_ref, gpre_ref, wup_ref, conv_ref, wdn_ref, gpost_ref, o_ref, carry_ref, f_ref, *, tm):
    @pl.when(pl.program_id(1) == 0)
    def _():
        carry_ref[...] = jnp.zeros_like(carry_ref)

    x = x_ref[...]
    hn = _rms(x, gpre_ref[...]).astype(BF16)

    def conv(col):
        cs = slice(col, col + FFN_CK)
        hid = _dot(hn, wup_ref[:, cs])
        prev8 = carry_ref[:, cs]
        w = conv_ref[:, cs]
        y = w[FFN_CONV - 1:FFN_CONV, :] * hid
        for s in range(1, FFN_CONV):
            y = y + w[FFN_CONV - 1 - s:FFN_CONV - s, :] * _shift_rows(hid, prev8, s)
        carry_ref[:, cs] = hid[tm - 8:tm, :]
        return y

    for ci in range(D_FF // FFN_CK):
        f_gate = conv(ci * FFN_CK)
        f_up = conv(D_FF + ci * FFN_CK)
        f_ref[:, ci * FFN_CK:(ci + 1) * FFN_CK] = (_gelu_tanh(f_gate) * f_up).astype(BF16)
    o_ref[...] = x + _rms(_dot(f_ref[...], wdn_ref[...]), gpost_ref[...])


def _ffn(x, gpre, wup, conv_w, wdn, gpost, layer, batch, seq, tm):
    steps = seq // tm
    d = x.shape[1]

    def whole(a):
        return pl.BlockSpec((None,) + a.shape[1:], lambda b, i: (layer, 0, 0))

    return pl.pallas_call(
        functools.partial(_ffn_kernel, tm=tm),
        grid=(batch, steps),
        in_specs=[pl.BlockSpec((tm, d), lambda b, i: (b * steps + i, 0)),
                  whole(gpre), whole(wup), whole(conv_w), whole(wdn), whole(gpost)],
        out_specs=pl.BlockSpec((tm, d), lambda b, i: (b * steps + i, 0)),
        out_shape=jax.ShapeDtypeStruct(x.shape, F32),
        scratch_shapes=[pltpu.VMEM((8, 2 * D_FF), F32), pltpu.VMEM((tm, D_FF), BF16)],
        compiler_params=pltpu.CompilerParams(
            dimension_semantics=("arbitrary", "arbitrary"), vmem_limit_bytes=VMEM_LIMIT),
        name="ffn",
    )(x, gpre, wup, conv_w, wdn, gpost)


def _pack_w_in(w):
    w = jnp.swapaxes(w, 1, 2).astype(BF16)
    o = 0
    parts = {}
    for name, width in (("aq", SB_W), ("ak", SB_W), ("av", SB_W), ("bq", SW_QW), ("bk", SW_KVW),
                        ("bv", SW_KVW), ("cqkv", 3 * GDN_W), ("cz", GDN_W), ("ca", GDN_HEADS),
                        ("cb", GDN_HEADS), ("ga", D_MODEL), ("gb", D_MODEL), ("gc", D_MODEL)):
        parts[name] = w[:, o:o + width, :]
        o += width

    def dup(t):
        return jnp.concatenate([t[:, h * HALF:(h + 1) * HALF, :] for h in range(SW_KV_HEADS) for _ in range(2)], axis=1)

    cols = [parts["ga"], parts["gb"], parts["gc"],
            parts["aq"] * (SB_HEAD_DIM ** -0.5), parts["ak"], parts["av"],
            parts["bq"] * (SW_HEAD_DIM ** -0.5), dup(parts["bk"]), dup(parts["bv"]),
            parts["cqkv"], parts["cz"]]
    pad = jnp.zeros((w.shape[0], LANES - 2 * GDN_HEADS, w.shape[2]), w.dtype)
    wab = jnp.concatenate([parts["ca"], parts["cb"], pad], axis=1)
    return jnp.concatenate(cols, axis=1).astype(BF16), wab


def _layer(x, batch, seq, layer, p):
    pbf, pfp, pab = _inproj(x, p["ln_mix_pre"], p["w_in"], p["w_ab"], layer, tm=512)
    ya = _sb_attention(pbf, batch, seq)
    yb = _sw_attention(pbf, p["sinks"], p["slopes"], layer, batch, seq)
    yc = _gdn(pfp, pab, p["gdn_conv"], p["gdn_small"], p["gdn_norm"], layer, batch, seq, rows=1024)
    x = _merge(ya, yb, yc, pbf, x, p["wa"], p["wb"], p["wc"], p["wo"], p["ln_mix_post"], layer, tm=1024)
    return _ffn(x, p["ln_ffn_pre"], p["w_up"], p["ffn_conv"], p["w_down"], p["ln_ffn_post"],
                layer, batch, seq, tm=512)


def kernel(x, ln_mix_pre, w_in, sw_sinks, gdn_conv, gdn_a_log, gdn_dt_bias, gdn_norm, w_branch_a,
           w_branch_b, w_branch_c, w_out, ln_mix_post, ln_ffn_pre, w_up, ffn_conv, w_down, ln_ffn_post):
    batch, seq, d = x.shape
    depth = w_in.shape[0]
    small = jnp.zeros((depth, 8, LANES), F32)
    small = small.at[:, 0, :GDN_HEADS].set(gdn_a_log).at[:, 1, :GDN_HEADS].set(gdn_dt_bias)
    w_main, w_ab = _pack_w_in(w_in)
    p = {
        "ln_mix_pre": ln_mix_pre[:, None, :], "w_in": w_main, "w_ab": w_ab,
        "sinks": sw_sinks, "slopes": jnp.exp2(-8.0 * jnp.arange(1, SW_Q_HEADS + 1, dtype=F32) / SW_Q_HEADS),
        "gdn_conv": gdn_conv, "gdn_small": small, "gdn_norm": gdn_norm[:, None, :],
        "wa": w_branch_a.astype(BF16), "wb": w_branch_b.astype(BF16),
        "wc": w_branch_c.astype(BF16), "wo": w_out.astype(BF16),
        "ln_mix_post": ln_mix_post[:, None, :], "ln_ffn_pre": ln_ffn_pre[:, None, :],
        "w_up": w_up.astype(BF16), "ffn_conv": ffn_conv, "w_down": w_down.astype(BF16),
        "ln_ffn_post": ln_ffn_post[:, None, :],
    }
    h = x.reshape(batch * seq, d)
    for layer in range(depth):
        h = _layer(h, batch, seq, layer, p)
    return h.reshape(batch, seq, d)
```

```python
import functools

import jax
import jax.numpy as jnp
from jax import lax
from jax.experimental import pallas as pl
from jax.experimental.pallas import tpu as pltpu

F32 = jnp.float32
BF16 = jnp.bfloat16
NORM_EPS = 1e-6

D_MODEL = 1024
SB_HEADS, SB_HEAD_DIM = 8, 64
SW_Q_HEADS, SW_KV_HEADS, SW_HEAD_DIM = 8, 2, 64
GDN_HEADS, GDN_HEAD_DIM, GDN_CONV = 4, 128, 4
D_FF, FFN_CONV = 2816, 3
SB_W = SB_HEADS * SB_HEAD_DIM
SW_QW = SW_Q_HEADS * SW_HEAD_DIM
SW_KVW = SW_KV_HEADS * SW_HEAD_DIM
GDN_W = GDN_HEADS * GDN_HEAD_DIM

BLK = 128
LANES = 128
HALF = 64
VMEM_LIMIT = 56 * 1024 * 1024

BF_GA, BF_GB, BF_GC, BF_SBQ, BF_SBK, BF_SBV, BF_SWQ, BF_SWK, BF_SWV, BF_COLS = (
    0, 1024, 2048, 3072, 3584, 4096, 4608, 5120, 5376, 5632)
FP_CQKV, FP_CZ, FP_COLS = 0, 1536, 2048
PROJ_TN = 512
LOG2E = 1.4426950408889634
F32_EXP2_ZERO = -104.0 * LOG2E


def _dot(a, b):
    return jnp.dot(a, b, preferred_element_type=F32)


def _dot_nt(a, b):
    return lax.dot_general(a, b, (((1,), (1,)), ((), ())), preferred_element_type=F32)


def _iota(shape, dim):
    return lax.broadcasted_iota(jnp.int32, shape, dim)


def _rms(t, gain):
    return t * lax.rsqrt(jnp.mean(t * t, axis=-1, keepdims=True) + NORM_EPS) * gain


def _shift_rows(h, prev8, s):
    r = pltpu.roll(h, s, axis=0)
    row = _iota(h.shape, 0)
    for t in range(s):
        r = jnp.where(row == t, prev8[8 - s + t:8 - s + t + 1, :], r)
    return r


def _inproj_kernel(x_ref, g_ref, w_ref, wab_ref, obf_ref, of_ref, oab_ref):
    xn = _rms(x_ref[...], g_ref[...]).astype(BF16)
    oab_ref[...] = _dot_nt(xn, wab_ref[...])
    tn = PROJ_TN
    for j in range(BF_COLS // tn):
        obf_ref[:, j * tn:(j + 1) * tn] = _dot_nt(xn, w_ref[j * tn:(j + 1) * tn, :]).astype(BF16)
    for j in range(FP_COLS // tn):
        of_ref[:, j * tn:(j + 1) * tn] = _dot_nt(xn, w_ref[BF_COLS + j * tn:BF_COLS + (j + 1) * tn, :])


def _inproj(x, gain, w, wab, layer, tm):
    t, d = x.shape

    def whole(a):
        return pl.BlockSpec((None,) + a.shape[1:], lambda i: (layer, 0, 0), pipeline_mode=pl.Buffered(1))

    return pl.pallas_call(
        _inproj_kernel,
        grid=(t // tm,),
        in_specs=[pl.BlockSpec((tm, d), lambda i: (i, 0)), whole(gain), whole(w), whole(wab)],
        out_specs=[
            pl.BlockSpec((tm, BF_COLS), lambda i: (i, 0)),
            pl.BlockSpec((tm, FP_COLS), lambda i: (i, 0)),
            pl.BlockSpec((tm, LANES), lambda i: (i, 0)),
        ],
        out_shape=[jax.ShapeDtypeStruct((t, BF_COLS), BF16), jax.ShapeDtypeStruct((t, FP_COLS), F32),
                   jax.ShapeDtypeStruct((t, LANES), F32)],
        compiler_params=pltpu.CompilerParams(
            dimension_semantics=("arbitrary",), vmem_limit_bytes=VMEM_LIMIT),
        name="inproj",
    )(x, gain, w, wab)


SB_TQ = 2 * BLK
SB_PAIRS = 2


def _sb_kernel(q_ref, k_ref, v_ref, o_ref, acc_ref, lr_ref):
    ti = pl.program_id(2)
    lane = _iota((BLK, LANES), 1)
    row = _iota((BLK, LANES), 0)
    first = lane < HALF
    tri = lane < row
    full = lane >= 0
    zero_tile = jnp.zeros((BLK, LANES), F32)
    r2 = _iota((BLK, 2 * LANES), 0)
    c2 = _iota((BLK, 2 * LANES), 1)
    suffix_ones = jnp.where((r2 > c2) | (c2 >= LANES), 1.0, 0.0).astype(BF16)
    suffix_ones = jnp.concatenate([suffix_ones, suffix_ones], axis=0)

    def split_heads(q):
        sel = jnp.concatenate([first] * (q.shape[0] // BLK), axis=0)
        zero = jnp.zeros_like(q)
        return jnp.where(sel, q, zero), jnp.where(sel, zero, q)

    def log_parts(z):
        zs = z * LOG2E
        log_stay = -(jnp.maximum(zs, 0.0) + jnp.log2(1.0 + jnp.exp2(-jnp.abs(zs))))
        return log_stay, log_stay + zs

    def suffix_sums(parts):
        hi = [x.astype(BF16) for x in parts]
        lo = [(x - h.astype(F32)).astype(BF16) for x, h in zip(parts, hi)]
        hilo = jnp.concatenate([jnp.concatenate(hi, axis=0), jnp.concatenate(lo, axis=0)], axis=1)
        sums = _dot(hilo, suffix_ones)
        out, o = [], 0
        for x in parts:
            out.append((sums[o:o + x.shape[0], :LANES], sums[o:o + x.shape[0], LANES:]))
            o += x.shape[0]
        return out

    def generic_block(pp, s, j, valid):
        ps = slice(pp * LANES, (pp + 1) * LANES)
        q0, q1 = split_heads(q_ref[s * BLK:(s + 1) * BLK, ps])
        off = pl.multiple_of(j * BLK, BLK)
        k, v = k_ref[pl.ds(off, BLK), ps], v_ref[pl.ds(off, BLK), ps]
        log_stay, log_take = log_parts(_dot_nt(jnp.concatenate([q0, q1], axis=0), k))
        if valid is not None:
            vm = jnp.concatenate([valid, valid], axis=0)
            log_stay = jnp.where(vm, log_stay, 0.0)
        (later, total), = suffix_sums([log_stay])
        w = jnp.exp2(log_take + later + lr_ref[pp, s])
        if valid is not None:
            w = jnp.where(vm, w, 0.0)
        pv = _dot(w.astype(BF16), v)
        acc_ref[pp, s] += jnp.where(first, pv[:BLK], pv[BLK:])
        lr_ref[pp, s] += total

    def tail(pp, s, j0):
        def cond(c):
            j, live = c
            return jnp.logical_and(j >= 0, live)

        def body(c):
            j, _ = c
            generic_block(pp, s, j, None)
            return j - 1, jnp.max(lr_ref[pp, s]) > F32_EXP2_ZERO

        lax.while_loop(cond, body, (j0, jnp.max(lr_ref[pp, s]) > F32_EXP2_ZERO))

    @pl.when(ti == 0)
    def _():
        for pp in range(SB_PAIRS):
            for s in range(2):
                acc_ref[pp, s] = jnp.zeros((BLK, LANES), F32)
                lr_ref[pp, s] = jnp.zeros((2 * BLK, LANES), F32)
                generic_block(pp, s, 2 * ti + s, tri)
                tail(pp, s, 2 * ti + s - 1)

    @pl.when(ti > 0)
    def _():
        base = pl.multiple_of((2 * ti - 2) * BLK, BLK)
        m2 = jnp.concatenate([tri, full, tri, full], axis=0)
        m3 = jnp.concatenate([tri, tri], axis=0)
        P = range(SB_PAIRS)
        ps = [slice(pp * LANES, (pp + 1) * LANES) for pp in P]
        qs = [split_heads(q_ref[:, ps[pp]]) for pp in P]
        kw = [k_ref[pl.ds(base, 4 * BLK), ps[pp]] for pp in P]
        vw = [v_ref[pl.ds(base, 4 * BLK), ps[pp]] for pp in P]
        l_ab = [jnp.concatenate([q0, q1], axis=0) for q0, q1 in qs]
        l_a = [jnp.concatenate([q0[:BLK], q1[:BLK]], axis=0) for q0, q1 in qs]
        l_b = [jnp.concatenate([q0[BLK:], q1[BLK:]], axis=0) for q0, q1 in qs]
        z3 = [_dot_nt(l_b[pp], kw[pp][3 * BLK:]) for pp in P]
        z2 = [_dot_nt(l_ab[pp], kw[pp][2 * BLK:3 * BLK]) for pp in P]
        z1 = [_dot_nt(l_ab[pp], kw[pp][BLK:2 * BLK]) for pp in P]
        z0 = [_dot_nt(l_a[pp], kw[pp][:BLK]) for pp in P]
        lp3 = [log_parts(z) for z in z3]
        s3 = [suffix_sums([jnp.where(m3, ls, 0.0)])[0] for ls, _ in lp3]
        lp2 = [log_parts(z) for z in z2]
        s2 = [suffix_sums([jnp.where(m2, ls, 0.0)])[0] for ls, _ in lp2]
        lp1 = [log_parts(z) for z in z1]
        s1 = [suffix_sums([ls])[0] for ls, _ in lp1]
        lp0 = [log_parts(z) for z in z0]
        s0 = [suffix_sums([ls])[0] for ls, _ in lp0]
        for pp in P:
            (lat3, tot3), (lat2, tot2), (lat1, tot1), (lat0, tot0) = s3[pp], s2[pp], s1[pp], s0[pp]
            lt3, lt2, lt1, lt0 = lp3[pp][1], lp2[pp][1], lp1[pp][1], lp0[pp][1]
            lrb2 = jnp.concatenate([zero_tile, tot3[:BLK], zero_tile, tot3[BLK:]], axis=0)
            lrb1 = lrb2 + tot2
            lre1 = lrb1 + tot1
            lrb0 = jnp.concatenate([lre1[:BLK], lre1[2 * BLK:3 * BLK]], axis=0)
            w3 = jnp.where(m3, jnp.exp2(lt3 + lat3), 0.0)
            w2 = jnp.where(m2, jnp.exp2(lt2 + lat2 + lrb2), 0.0)
            w1 = jnp.exp2(lt1 + lat1 + lrb1)
            w0 = jnp.exp2(lt0 + lat0 + lrb0)
            w12 = jnp.concatenate([w1.astype(BF16), w2.astype(BF16)], axis=1)
            pv12 = _dot(w12, vw[pp][BLK:3 * BLK])
            pv0 = _dot(w0.astype(BF16), vw[pp][:BLK])
            pv3 = _dot(w3.astype(BF16), vw[pp][3 * BLK:])
            acc_ref[pp, 0] = jnp.where(first, pv12[:BLK] + pv0[:BLK], pv12[2 * BLK:3 * BLK] + pv0[BLK:])
            acc_ref[pp, 1] = jnp.where(first, pv12[BLK:2 * BLK] + pv3[:BLK], pv12[3 * BLK:] + pv3[BLK:])
            lr_ref[pp, 0] = lrb0 + tot0
            lr_ref[pp, 1] = jnp.concatenate([lre1[BLK:2 * BLK], lre1[3 * BLK:]], axis=0)

        @pl.when(jnp.max(lr_ref[...]) > F32_EXP2_ZERO)
        def _():
            for pp in range(SB_PAIRS):
                tail(pp, 0, 2 * ti - 3)
                tail(pp, 1, 2 * ti - 2)

    for pp in range(SB_PAIRS):
        for s in range(2):
            o_ref[s * BLK:(s + 1) * BLK, pp * LANES:(pp + 1) * LANES] = acc_ref[pp, s].astype(o_ref.dtype)


def _sb_attention(pbf, batch, seq):
    nt = seq // SB_TQ
    w = SB_PAIRS * LANES
    cq, ck, cv = BF_SBQ // w, BF_SBK // w, BF_SBV // w
    return pl.pallas_call(
        _sb_kernel,
        grid=(batch, SB_W // w, nt),
        in_specs=[
            pl.BlockSpec((SB_TQ, w), lambda b, p, i: (b * nt + i, cq + p)),
            pl.BlockSpec((seq, w), lambda b, p, i: (b, ck + p)),
            pl.BlockSpec((seq, w), lambda b, p, i: (b, cv + p)),
        ],
        out_specs=pl.BlockSpec((SB_TQ, w), lambda b, p, i: (b * nt + i, p)),
        out_shape=jax.ShapeDtypeStruct((batch * seq, SB_W), BF16),
        scratch_shapes=[pltpu.VMEM((SB_PAIRS, 2, BLK, LANES), F32),
                        pltpu.VMEM((SB_PAIRS, 2, 2 * BLK, LANES), F32)],
        compiler_params=pltpu.CompilerParams(
            dimension_semantics=("arbitrary", "arbitrary", "arbitrary"), vmem_limit_bytes=VMEM_LIMIT),
        name="sb_attention",
    )(pbf, pbf, pbf)


SW_TQ = 4 * BLK


def _sw_kernel(sinks_ref, slopes_ref, q_ref, kp_ref, kc_ref, vp_ref, vc_ref, o_ref, *, layer):
    ti = pl.program_id(1)
    lane = _iota((BLK, LANES), 1)
    row = _iota((BLK, LANES), 0)
    first = lane < HALF
    cur = lane <= row
    dist = jnp.where(cur, row - lane, row - lane + BLK).astype(F32)
    group = SW_Q_HEADS // SW_KV_HEADS
    units = []
    for s in range(SW_TQ // BLK):
        rs = slice(s * BLK, (s + 1) * BLK)
        for hk in range(SW_KV_HEADS):
            ks = slice(hk * LANES, (hk + 1) * LANES)
            kc, vc = kc_ref[rs, ks], vc_ref[rs, ks]
            if s == 0:
                kp, vp = kp_ref[:, ks], vp_ref[:, ks]
            else:
                kp, vp = kc_ref[(s - 1) * BLK:s * BLK, ks], vc_ref[(s - 1) * BLK:s * BLK, ks]
            pairs = range(hk * group // 2, (hk + 1) * group // 2)
            lhs = []
            for p in pairs:
                qp = q_ref[rs, p * LANES:(p + 1) * LANES]
                zero = jnp.zeros_like(qp)
                lhs += [jnp.where(first, qp, zero), jnp.where(first, zero, qp)]
            lhs = jnp.concatenate(lhs, axis=0)
            units.append(dict(s=s, rs=rs, hk=hk, pairs=pairs, vc=vc, vp=vp,
                              s_cur=_dot_nt(lhs, kc), s_prev=_dot_nt(lhs, kp)))
    for u in units:
        scores = []
        for g in range(group):
            head = u["hk"] * group + g
            gs = slice(g * BLK, (g + 1) * BLK)
            sc = jnp.where(cur, u["s_cur"][gs], u["s_prev"][gs]) - slopes_ref[head] * dist
            if u["s"] == 0:
                sc = jnp.where(jnp.logical_or(cur, ti > 0), sc, -jnp.inf)
            scores.append(sc)
        u["scores"] = scores
        u["m"] = [jnp.maximum(jnp.max(sc, axis=-1, keepdims=True), sinks_ref[layer, u["hk"] * group + g])
                  for g, sc in enumerate(scores)]
    for u in units:
        u["e"] = [jnp.exp(sc - m) for sc, m in zip(u["scores"], u["m"])]
        u["denom"] = [jnp.sum(e, axis=-1, keepdims=True) + jnp.exp(sinks_ref[layer, u["hk"] * group + g] - m)
                      for g, (e, m) in enumerate(zip(u["e"], u["m"]))]
    for u in units:
        prob = [(e / d).astype(BF16) for e, d in zip(u["e"], u["denom"])]
        pz = jnp.zeros_like(prob[0])
        prob_c = jnp.concatenate([jnp.where(cur, p, pz) for p in prob], axis=0)
        prob_p = jnp.concatenate([jnp.where(cur, pz, p) for p in prob], axis=0)
        pv = _dot(prob_c, u["vc"]) + _dot(prob_p, u["vp"])
        for t, p in enumerate(u["pairs"]):
            o_ref[u["rs"], p * LANES:(p + 1) * LANES] = jnp.where(
                first, pv[2 * t * BLK:(2 * t + 1) * BLK], pv[(2 * t + 1) * BLK:(2 * t + 2) * BLK]
            ).astype(o_ref.dtype)


def _sw_attention(pbf, sinks, slopes, layer, batch, seq):
    nt = seq // SW_TQ
    kvw = 2 * SW_KVW
    cq, ck, cv = BF_SWQ // SW_QW, BF_SWK // kvw, BF_SWV // kvw
    per = SW_TQ // BLK

    def prev(b, i):
        return b * nt * per + jnp.maximum(i * per - 1, 0)

    smem = pl.BlockSpec(memory_space=pltpu.SMEM)
    return pl.pallas_call(
        functools.partial(_sw_kernel, layer=layer),
        grid=(batch, nt),
        in_specs=[
            smem, smem,
            pl.BlockSpec((SW_TQ, SW_QW), lambda b, i: (b * nt + i, cq)),
            pl.BlockSpec((BLK, kvw), lambda b, i: (prev(b, i), ck)),
            pl.BlockSpec((SW_TQ, kvw), lambda b, i: (b * nt + i, ck)),
            pl.BlockSpec((BLK, kvw), lambda b, i: (prev(b, i), cv)),
            pl.BlockSpec((SW_TQ, kvw), lambda b, i: (b * nt + i, cv)),
        ],
        out_specs=pl.BlockSpec((SW_TQ, SW_QW), lambda b, i: (b * nt + i, 0)),
        out_shape=jax.ShapeDtypeStruct((batch * seq, SW_QW), BF16),
        compiler_params=pltpu.CompilerParams(
            dimension_semantics=("arbitrary", "arbitrary"), vmem_limit_bytes=VMEM_LIMIT),
        name="sw_attention",
    )(sinks, slopes, pbf, pbf, pbf, pbf, pbf)


GDN_CHUNK = 128


def _split3(x):
    x1 = x.astype(BF16)
    r1 = x - x1.astype(F32)
    x2 = r1.astype(BF16)
    x3 = (r1 - x2.astype(F32)).astype(BF16)
    return x1, x2, x3


def _gdn_kernel(q_ref, k_ref, v_ref, z_ref, ab_ref, conv_ref, small_ref, norm_ref, o_ref,
                carry_ref, state_ref, yq_ref, yk_ref, yv_ref, *, rows):
    n = GDN_CHUNK
    nchunks = rows // n
    step = pl.program_id(1)

    @pl.when(step == 0)
    def _():
        carry_ref[...] = jnp.zeros_like(carry_ref)
        state_ref[...] = jnp.zeros_like(state_ref)

    for idx, (src, dst) in enumerate(((q_ref, yq_ref), (k_ref, yk_ref), (v_ref, yv_ref))):
        cols = slice(idx * GDN_W, (idx + 1) * GDN_W)
        w = conv_ref[:, cols]
        for ci in range(nchunks):
            x = src[ci * n:(ci + 1) * n, :]
            prev8 = carry_ref[:, cols] if ci == 0 else src[ci * n - 8:ci * n, :]
            y = w[GDN_CONV - 1:GDN_CONV, :] * x
            for s in range(1, GDN_CONV):
                y = y + w[GDN_CONV - 1 - s:GDN_CONV - s, :] * _shift_rows(x, prev8, s)
            y = y * jax.nn.sigmoid(y)
            for h in range(GDN_HEADS):
                yh = y[:, h * GDN_HEAD_DIM:(h + 1) * GDN_HEAD_DIM]
                if idx < 2:
                    inv = lax.rsqrt(jnp.sum(yh * yh, axis=-1, keepdims=True) + NORM_EPS)
                    yh = yh * (inv * (GDN_HEAD_DIM ** -0.5) if idx == 0 else inv)
                dst[ci * n:(ci + 1) * n, h * GDN_HEAD_DIM:(h + 1) * GDN_HEAD_DIM] = yh
        carry_ref[:, cols] = src[rows - 8:rows, :]

    r = _iota((n, n), 0)
    c = _iota((n, n), 1)
    causal = r >= c
    strict = r > c
    lower_ones = jnp.where(causal, 1.0, 0.0).astype(BF16)
    head_lane = _iota((n, LANES), 1) < GDN_HEADS

    def sub_blocks(ls):
        return ((r >> (ls + 1)) == (c >> (ls + 1))) & ((r >> ls) != (c >> ls))

    probs = []
    for ci in range(nchunks):
        rs = slice(ci * n, (ci + 1) * n)
        ab = ab_ref[rs, :]
        zab = ab + small_ref[1:2, :]
        softplus = jnp.maximum(zab, 0.0) + jnp.log(1.0 + jnp.exp(-jnp.abs(zab)))
        gbeta = jnp.where(head_lane, -jnp.exp(small_ref[0:1, :]) * softplus, jax.nn.sigmoid(ab))
        gc_all = sum(_dot(lower_ones, part) for part in _split3(gbeta))
        gc_t = gc_all.T
        for h in range(GDN_HEADS):
            hs = slice(h * GDN_HEAD_DIM, (h + 1) * GDN_HEAD_DIM)
            q, k, v = yq_ref[rs, hs], yk_ref[rs, hs], yv_ref[rs, hs]
            gc = jnp.broadcast_to(gc_all[:, h:h + 1], (n, n))
            gc_row = jnp.broadcast_to(gc_t[h:h + 1, :], (n, n))
            g_tot = jnp.broadcast_to(gc[n - 1:n, :], (n, n))
            beta_b = jnp.broadcast_to(gbeta[:, GDN_HEADS + h:GDN_HEADS + h + 1], (n, n))
            decay = jnp.exp(gc - gc_row)
            k_beta = k * beta_b
            k16 = k.astype(BF16)
            e_gc = jnp.exp(gc)
            probs.append(dict(
                rs=rs, hs=hs, h=h,
                lower=jnp.where(strict, _dot_nt(k_beta.astype(BF16), k16) * decay, 0.0),
                a=jnp.where(causal, _dot_nt(q.astype(BF16), k16) * decay, 0.0).astype(BF16),
                qd=(q * e_gc).astype(BF16),
                kdt=(k * jnp.exp(g_tot - gc)).T.astype(BF16),
                gl=jnp.exp(g_tot),
                rhs=jnp.concatenate([v * beta_b, k_beta * e_gc], axis=1)))
    eye = jnp.where(r == c, 1.0, 0.0).astype(F32)
    zero16 = jnp.zeros((n, n), BF16)
    lower16 = [p["lower"].astype(BF16) for p in probs]
    xs = [eye - jnp.where((r >> 1) == (c >> 1), p["lower"], 0.0) for p in probs]
    for ls in range(1, n.bit_length() - 1):
        x16 = [x.astype(BF16) for x in xs]
        ys = [_dot(xb, jnp.where(sub_blocks(ls), l16, zero16)).astype(BF16) for xb, l16 in zip(x16, lower16)]
        xs = [x - _dot(y, xb) for x, y, xb in zip(xs, ys, x16)]
    for p, x in zip(probs, xs):
        uw = _dot(x.astype(BF16), p["rhs"].astype(BF16))
        p["u"] = uw[:, :n]
        p["wq"] = jnp.concatenate([uw[:, n:].astype(BF16), p["qd"]], axis=0)
        p["ak"] = jnp.concatenate([p["a"], p["kdt"]], axis=0)

    gain = norm_ref[...]
    states = [state_ref[h] for h in range(GDN_HEADS)]
    for p in probs:
        h = p["h"]
        sr = _dot(p["wq"], states[h].astype(BF16))
        vn16 = (p["u"] - sr[:n]).astype(BF16)
        vr = _dot(p["ak"], vn16)
        o = sr[n:] + vr[:n]
        states[h] = states[h] * p["gl"] + vr[n:]
        zg = z_ref[p["rs"], p["hs"]]
        o_ref[p["rs"], p["hs"]] = (_rms(o, gain) * (zg * jax.nn.sigmoid(zg))).astype(o_ref.dtype)
    for h in range(GDN_HEADS):
        state_ref[h] = states[h]


def _gdn(pfp, pab, conv_w, small, norm, layer, batch, seq, rows):
    steps = seq // rows
    w = GDN_W
    cq = FP_CQKV // w

    def blk(col):
        return pl.BlockSpec((rows, w), lambda b, i: (b * steps + i, col))

    return pl.pallas_call(
        functools.partial(_gdn_kernel, rows=rows),
        grid=(batch, steps),
        in_specs=[
            blk(cq), blk(cq + 1), blk(cq + 2), blk(FP_CZ // w),
            pl.BlockSpec((rows, LANES), lambda b, i: (b * steps + i, 0)),
            pl.BlockSpec((None, GDN_CONV, 3 * w), lambda b, i: (layer, 0, 0)),
            pl.BlockSpec((None, 8, LANES), lambda b, i: (layer, 0, 0)),
            pl.BlockSpec((None, 1, GDN_HEAD_DIM), lambda b, i: (layer, 0, 0)),
        ],
        out_specs=pl.BlockSpec((rows, w), lambda b, i: (b * steps + i, 0)),
        out_shape=jax.ShapeDtypeStruct((batch * seq, w), BF16),
        scratch_shapes=[
            pltpu.VMEM((8, 3 * w), F32),
            pltpu.VMEM((GDN_HEADS, GDN_HEAD_DIM, GDN_HEAD_DIM), F32),
            pltpu.VMEM((rows, w), F32), pltpu.VMEM((rows, w), F32), pltpu.VMEM((rows, w), F32),
        ],
        compiler_params=pltpu.CompilerParams(
            dimension_semantics=("arbitrary", "arbitrary"), vmem_limit_bytes=VMEM_LIMIT),
        name="gdn",
    )(pfp, pfp, pfp, pfp, pab, conv_w, small, norm)


MERGE_CK = 256


def _merge_kernel(ya_ref, yb_ref, yc_ref, ga_ref, gb_ref, gc_ref, x_ref,
                  wa_ref, wb_ref, wc_ref, wo_ref, gain_ref, o_ref, m_ref):
    ya, yb, yc = ya_ref[...], yb_ref[...], yc_ref[...]
    for c0 in range(0, D_MODEL, MERGE_CK):
        cs = slice(c0, c0 + MERGE_CK)
        m_ref[:, cs] = (jax.nn.sigmoid(ga_ref[:, cs].astype(F32)) * _dot(ya, wa_ref[:, cs])
                        + jax.nn.sigmoid(gb_ref[:, cs].astype(F32)) * _dot(yb, wb_ref[:, cs])
                        + jax.nn.sigmoid(gc_ref[:, cs].astype(F32)) * _dot(yc, wc_ref[:, cs])).astype(BF16)
    o_ref[...] = x_ref[...] + _rms(_dot(m_ref[...], wo_ref[...]), gain_ref[...])


def _merge(ya, yb, yc, pbf, x, wa, wb, wc, wo, gain, layer, tm):
    t, d = x.shape

    def rows(width, col=0):
        return pl.BlockSpec((tm, width), lambda i: (i, col))

    def whole(a):
        return pl.BlockSpec((None,) + a.shape[1:], lambda i: (layer, 0, 0))

    return pl.pallas_call(
        _merge_kernel,
        grid=(t // tm,),
        in_specs=[rows(SB_W), rows(SW_QW), rows(GDN_W),
                  rows(d, BF_GA // d), rows(d, BF_GB // d), rows(d, BF_GC // d), rows(d),
                  whole(wa), whole(wb), whole(wc), whole(wo), whole(gain)],
        out_specs=rows(d),
        out_shape=jax.ShapeDtypeStruct((t, d), F32),
        scratch_shapes=[pltpu.VMEM((tm, d), BF16)],
        compiler_params=pltpu.CompilerParams(
            dimension_semantics=("arbitrary",), vmem_limit_bytes=VMEM_LIMIT),
        name="merge",
    )(ya, yb, yc, pbf, pbf, pbf, x, wa, wb, wc, wo, gain)


FFN_CK = 768
SQRT_2_OVER_PI = 0.7978845608028654


def _gelu_tanh(x):
    inner = x * (SQRT_2_OVER_PI + (SQRT_2_OVER_PI * 0.044715) * (x * x))
    return (0.5 * x) * (1.0 + jnp.tanh(inner))


def _ffn_kernel(x_ref, gpre_ref, wup_ref, conv_ref, wdn_ref, gpost_ref, o_ref, carry_ref, f_ref, *, tm):
    @pl.when(pl.program_id(1) == 0)
    def _():
        carry_ref[...] = jnp.zeros_like(carry_ref)

    x = x_ref[...]
    hn = _rms(x, gpre_ref[...]).astype(BF16)

    def conv(col, width):
        cs = slice(col, col + width)
        hid = _dot(hn, wup_ref[:, cs])
        prev8 = carry_ref[:, cs]
        w = conv_ref[:, cs]
        y = w[FFN_CONV - 1:FFN_CONV, :] * hid
        for s in range(1, FFN_CONV):
            y = y + w[FFN_CONV - 1 - s:FFN_CONV - s, :] * _shift_rows(hid, prev8, s)
        carry_ref[:, cs] = hid[tm - 8:tm, :]
        return y

    col = 0
    while col < D_FF:
        width = min(FFN_CK, D_FF - col)
        f_gate = conv(col, width)
        f_up = conv(D_FF + col, width)
        f_ref[:, col:col + width] = (_gelu_tanh(f_gate) * f_up).astype(BF16)
        col += width
    o_ref[...] = x + _rms(_dot(f_ref[...], wdn_ref[...]), gpost_ref[...])


def _ffn(x, gpre, wup, conv_w, wdn, gpost, layer, batch, seq, tm):
    steps = seq // tm
    d = x.shape[1]

    def whole(a):
        return pl.BlockSpec((None,) + a.shape[1:], lambda b, i: (layer, 0, 0))

    return pl.pallas_call(
        functools.partial(_ffn_kernel, tm=tm),
        grid=(batch, steps),
        in_specs=[pl.BlockSpec((tm, d), lambda b, i: (b * steps + i, 0)),
                  whole(gpre), whole(wup), whole(conv_w), whole(wdn), whole(gpost)],
        out_specs=pl.BlockSpec((tm, d), lambda b, i: (b * steps + i, 0)),
        out_shape=jax.ShapeDtypeStruct(x.shape, F32),
        scratch_shapes=[pltpu.VMEM((8, 2 * D_FF), F32), pltpu.VMEM((tm, D_FF), BF16)],
        compiler_params=pltpu.CompilerParams(
            dimension_semantics=("arbitrary", "arbitrary"), vmem_limit_bytes=VMEM_LIMIT),
        name="ffn",
    )(x, gpre, wup, conv_w, wdn, gpost)


def _pack_w_in(w):
    w = jnp.swapaxes(w, 1, 2).astype(BF16)
    o = 0
    parts = {}
    for name, width in (("aq", SB_W), ("ak", SB_W), ("av", SB_W), ("bq", SW_QW), ("bk", SW_KVW),
                        ("bv", SW_KVW), ("cqkv", 3 * GDN_W), ("cz", GDN_W), ("ca", GDN_HEADS),
                        ("cb", GDN_HEADS), ("ga", D_MODEL), ("gb", D_MODEL), ("gc", D_MODEL)):
        parts[name] = w[:, o:o + width, :]
        o += width

    def dup(t):
        return jnp.concatenate([t[:, h * HALF:(h + 1) * HALF, :] for h in range(SW_KV_HEADS) for _ in range(2)], axis=1)

    cols = [parts["ga"], parts["gb"], parts["gc"],
            parts["aq"] * (SB_HEAD_DIM ** -0.5), parts["ak"], parts["av"],
            parts["bq"] * (SW_HEAD_DIM ** -0.5), dup(parts["bk"]), dup(parts["bv"]),
            parts["cqkv"], parts["cz"]]
    pad = jnp.zeros((w.shape[0], LANES - 2 * GDN_HEADS, w.shape[2]), w.dtype)
    wab = jnp.concatenate([parts["ca"], parts["cb"], pad], axis=1)
    return jnp.concatenate(cols, axis=1).astype(BF16), wab


def _layer(x, batch, seq, layer, p):
    pbf, pfp, pab = _inproj(x, p["ln_mix_pre"], p["w_in"], p["w_ab"], layer, tm=512)
    ya = _sb_attention(pbf, batch, seq)
    yb = _sw_attention(pbf, p["sinks"], p["slopes"], layer, batch, seq)
    yc = _gdn(pfp, pab, p["gdn_conv"], p["gdn_small"], p["gdn_norm"], layer, batch, seq, rows=512)
    x = _merge(ya, yb, yc, pbf, x, p["wa"], p["wb"], p["wc"], p["wo"], p["ln_mix_post"], layer, tm=1024)
    return _ffn(x, p["ln_ffn_pre"], p["w_up"], p["ffn_conv"], p["w_down"], p["ln_ffn_post"],
                layer, batch, seq, tm=512)


def kernel(x, ln_mix_pre, w_in, sw_sinks, gdn_conv, gdn_a_log, gdn_dt_bias, gdn_norm, w_branch_a,
           w_branch_b, w_branch_c, w_out, ln_mix_post, ln_ffn_pre, w_up, ffn_conv, w_down, ln_ffn_post):
    batch, seq, d = x.shape
    depth = w_in.shape[0]
    small = jnp.zeros((depth, 8, LANES), F32)
    small = small.at[:, 0, :GDN_HEADS].set(gdn_a_log).at[:, 1, :GDN_HEADS].set(gdn_dt_bias)
    w_main, w_ab = _pack_w_in(w_in)
    p = {
        "ln_mix_pre": ln_mix_pre[:, None, :], "w_in": w_main, "w_ab": w_ab,
        "sinks": sw_sinks, "slopes": jnp.exp2(-8.0 * jnp.arange(1, SW_Q_HEADS + 1, dtype=F32) / SW_Q_HEADS),
        "gdn_conv": gdn_conv, "gdn_small": small, "gdn_norm": gdn_norm[:, None, :],
        "wa": w_branch_a.astype(BF16), "wb": w_branch_b.astype(BF16),
        "wc": w_branch_c.astype(BF16), "wo": w_out.astype(BF16),
        "ln_mix_post": ln_mix_post[:, None, :], "ln_ffn_pre": ln_ffn_pre[:, None, :],
        "w_up": w_up.astype(BF16), "ffn_conv": ffn_conv, "w_down": w_down.astype(BF16),
        "ln_ffn_post": ln_ffn_post[:, None, :],
    }
    h = x.reshape(batch * seq, d)
    for layer in range(depth):
        h = _layer(h, batch, seq, layer, p)
    return h.reshape(batch, seq, d)
```

```python
import functools

import jax
import jax.numpy as jnp
from jax import lax
from jax.experimental import pallas as pl
from jax.experimental.pallas import tpu as pltpu

F32 = jnp.float32
BF16 = jnp.bfloat16
NORM_EPS = 1e-6

D_MODEL = 1024
SB_HEADS, SB_HEAD_DIM = 8, 64
SW_Q_HEADS, SW_KV_HEADS, SW_HEAD_DIM = 8, 2, 64
GDN_HEADS, GDN_HEAD_DIM, GDN_CONV = 4, 128, 4
D_FF, FFN_CONV = 2816, 3
SB_W = SB_HEADS * SB_HEAD_DIM
SW_QW = SW_Q_HEADS * SW_HEAD_DIM
SW_KVW = SW_KV_HEADS * SW_HEAD_DIM
GDN_W = GDN_HEADS * GDN_HEAD_DIM

BLK = 128
LANES = 128
HALF = 64
VMEM_LIMIT = 56 * 1024 * 1024

BF_GA, BF_GB, BF_GC, BF_SBQ, BF_SBK, BF_SBV, BF_SWQ, BF_SWK, BF_SWV, BF_COLS = (
    0, 1024, 2048, 3072, 3584, 4096, 4608, 5120, 5376, 5632)
FP_CQKV, FP_CZ, FP_COLS = 0, 1536, 2048
PROJ_TN = 512
LOG2E = 1.4426950408889634
F32_EXP2_ZERO = -104.0 * LOG2E


def _dot(a, b):
    return jnp.dot(a, b, preferred_element_type=F32)


def _dot_nt(a, b):
    return lax.dot_general(a, b, (((1,), (1,)), ((), ())), preferred_element_type=F32)


def _iota(shape, dim):
    return lax.broadcasted_iota(jnp.int32, shape, dim)


def _rms(t, gain):
    return t * lax.rsqrt(jnp.mean(t * t, axis=-1, keepdims=True) + NORM_EPS) * gain


def _shift_rows(h, prev8, s):
    r = pltpu.roll(h, s, axis=0)
    row = _iota(h.shape, 0)
    for t in range(s):
        r = jnp.where(row == t, prev8[8 - s + t:8 - s + t + 1, :], r)
    return r


def _inproj_kernel(x_ref, g_ref, w_ref, wab_ref, obf_ref, of_ref, oab_ref):
    xn = _rms(x_ref[...], g_ref[...]).astype(BF16)
    oab_ref[...] = _dot_nt(xn, wab_ref[...])
    tn = PROJ_TN
    for j in range(BF_COLS // tn):
        obf_ref[:, j * tn:(j + 1) * tn] = _dot_nt(xn, w_ref[j * tn:(j + 1) * tn, :]).astype(BF16)
    for j in range(FP_COLS // tn):
        of_ref[:, j * tn:(j + 1) * tn] = _dot_nt(xn, w_ref[BF_COLS + j * tn:BF_COLS + (j + 1) * tn, :])


def _inproj(x, gain, w, wab, layer, tm):
    t, d = x.shape

    def whole(a):
        return pl.BlockSpec((None,) + a.shape[1:], lambda i: (layer, 0, 0), pipeline_mode=pl.Buffered(1))

    return pl.pallas_call(
        _inproj_kernel,
        grid=(t // tm,),
        in_specs=[pl.BlockSpec((tm, d), lambda i: (i, 0)), whole(gain), whole(w), whole(wab)],
        out_specs=[
            pl.BlockSpec((tm, BF_COLS), lambda i: (i, 0)),
            pl.BlockSpec((tm, FP_COLS), lambda i: (i, 0)),
            pl.BlockSpec((tm, LANES), lambda i: (i, 0)),
        ],
        out_shape=[jax.ShapeDtypeStruct((t, BF_COLS), BF16), jax.ShapeDtypeStruct((t, FP_COLS), F32),
                   jax.ShapeDtypeStruct((t, LANES), F32)],
        compiler_params=pltpu.CompilerParams(
            dimension_semantics=("arbitrary",), vmem_limit_bytes=VMEM_LIMIT),
        name="inproj",
    )(x, gain, w, wab)


SB_TQ = 2 * BLK
SB_PAIRS = 4


def _sb_kernel(q_ref, k_ref, v_ref, o_ref, acc_ref, lr_ref):
    ti = pl.program_id(2)
    lane = _iota((BLK, LANES), 1)
    row = _iota((BLK, LANES), 0)
    first = lane < HALF
    tri = lane < row
    full = lane >= 0
    zero_tile = jnp.zeros((BLK, LANES), F32)
    r2 = _iota((BLK, 2 * LANES), 0)
    c2 = _iota((BLK, 2 * LANES), 1)
    suffix_ones = jnp.where((r2 > c2) | (c2 >= LANES), 1.0, 0.0).astype(BF16)
    suffix_ones = jnp.concatenate([suffix_ones, suffix_ones], axis=0)

    def split_heads(q):
        sel = jnp.concatenate([first] * (q.shape[0] // BLK), axis=0)
        zero = jnp.zeros_like(q)
        return jnp.where(sel, q, zero), jnp.where(sel, zero, q)

    def log_parts(z):
        zs = z * LOG2E
        log_stay = -(jnp.maximum(zs, 0.0) + jnp.log2(1.0 + jnp.exp2(-jnp.abs(zs))))
        return log_stay, log_stay + zs

    def suffix_sums(parts):
        hi = [x.astype(BF16) for x in parts]
        lo = [(x - h.astype(F32)).astype(BF16) for x, h in zip(parts, hi)]
        hilo = jnp.concatenate([jnp.concatenate(hi, axis=0), jnp.concatenate(lo, axis=0)], axis=1)
        sums = _dot(hilo, suffix_ones)
        out, o = [], 0
        for x in parts:
            out.append((sums[o:o + x.shape[0], :LANES], sums[o:o + x.shape[0], LANES:]))
            o += x.shape[0]
        return out

    def generic_block(pp, s, j, valid):
        ps = slice(pp * LANES, (pp + 1) * LANES)
        q0, q1 = split_heads(q_ref[s * BLK:(s + 1) * BLK, ps])
        off = pl.multiple_of(j * BLK, BLK)
        k, v = k_ref[pl.ds(off, BLK), ps], v_ref[pl.ds(off, BLK), ps]
        log_stay, log_take = log_parts(_dot_nt(jnp.concatenate([q0, q1], axis=0), k))
        if valid is not None:
            vm = jnp.concatenate([valid, valid], axis=0)
            log_stay = jnp.where(vm, log_stay, 0.0)
        (later, total), = suffix_sums([log_stay])
        w = jnp.exp2(log_take + later + lr_ref[pp, s])
        if valid is not None:
            w = jnp.where(vm, w, 0.0)
        pv = _dot(w.astype(BF16), v)
        acc_ref[pp, s] += jnp.where(first, pv[:BLK], pv[BLK:])
        lr_ref[pp, s] += total

    def tail(pp, s, j0):
        def cond(c):
            j, live = c
            return jnp.logical_and(j >= 0, live)

        def body(c):
            j, _ = c
            generic_block(pp, s, j, None)
            return j - 1, jnp.max(lr_ref[pp, s]) > F32_EXP2_ZERO

        lax.while_loop(cond, body, (j0, jnp.max(lr_ref[pp, s]) > F32_EXP2_ZERO))

    @pl.when(ti == 0)
    def _():
        for pp in range(SB_PAIRS):
            for s in range(2):
                acc_ref[pp, s] = jnp.zeros((BLK, LANES), F32)
                lr_ref[pp, s] = jnp.zeros((2 * BLK, LANES), F32)
                generic_block(pp, s, 2 * ti + s, tri)
                tail(pp, s, 2 * ti + s - 1)

    @pl.when(ti > 0)
    def _():
        base = pl.multiple_of((2 * ti - 2) * BLK, BLK)
        m2 = jnp.concatenate([tri, full, tri, full], axis=0)
        m3 = jnp.concatenate([tri, tri], axis=0)
        P = range(SB_PAIRS)
        ps = [slice(pp * LANES, (pp + 1) * LANES) for pp in P]
        qs = [split_heads(q_ref[:, ps[pp]]) for pp in P]
        kw = [k_ref[pl.ds(base, 4 * BLK), ps[pp]] for pp in P]
        vw = [v_ref[pl.ds(base, 4 * BLK), ps[pp]] for pp in P]
        l_ab = [jnp.concatenate([q0, q1], axis=0) for q0, q1 in qs]
        l_a = [jnp.concatenate([q0[:BLK], q1[:BLK]], axis=0) for q0, q1 in qs]
        l_b = [jnp.concatenate([q0[BLK:], q1[BLK:]], axis=0) for q0, q1 in qs]
        z3 = [_dot_nt(l_b[pp], kw[pp][3 * BLK:]) for pp in P]
        z2 = [_dot_nt(l_ab[pp], kw[pp][2 * BLK:3 * BLK]) for pp in P]
        z1 = [_dot_nt(l_ab[pp], kw[pp][BLK:2 * BLK]) for pp in P]
        z0 = [_dot_nt(l_a[pp], kw[pp][:BLK]) for pp in P]
        lp3 = [log_parts(z) for z in z3]
        s3 = [suffix_sums([jnp.where(m3, ls, 0.0)])[0] for ls, _ in lp3]
        lp2 = [log_parts(z) for z in z2]
        s2 = [suffix_sums([jnp.where(m2, ls, 0.0)])[0] for ls, _ in lp2]
        lp1 = [log_parts(z) for z in z1]
        s1 = [suffix_sums([ls])[0] for ls, _ in lp1]
        lp0 = [log_parts(z) for z in z0]
        s0 = [suffix_sums([ls])[0] for ls, _ in lp0]
        for pp in P:
            (lat3, tot3), (lat2, tot2), (lat1, tot1), (lat0, tot0) = s3[pp], s2[pp], s1[pp], s0[pp]
            lt3, lt2, lt1, lt0 = lp3[pp][1], lp2[pp][1], lp1[pp][1], lp0[pp][1]
            lrb2 = jnp.concatenate([zero_tile, tot3[:BLK], zero_tile, tot3[BLK:]], axis=0)
            lrb1 = lrb2 + tot2
            lre1 = lrb1 + tot1
            lrb0 = jnp.concatenate([lre1[:BLK], lre1[2 * BLK:3 * BLK]], axis=0)
            w3 = jnp.where(m3, jnp.exp2(lt3 + lat3), 0.0)
            w2 = jnp.where(m2, jnp.exp2(lt2 + lat2 + lrb2), 0.0)
            w1 = jnp.exp2(lt1 + lat1 + lrb1)
            w0 = jnp.exp2(lt0 + lat0 + lrb0)
            w12 = jnp.concatenate([w1.astype(BF16), w2.astype(BF16)], axis=1)
            pv12 = _dot(w12, vw[pp][BLK:3 * BLK])
            pv0 = _dot(w0.astype(BF16), vw[pp][:BLK])
            pv3 = _dot(w3.astype(BF16), vw[pp][3 * BLK:])
            acc_ref[pp, 0] = jnp.where(first, pv12[:BLK] + pv0[:BLK], pv12[2 * BLK:3 * BLK] + pv0[BLK:])
            acc_ref[pp, 1] = jnp.where(first, pv12[BLK:2 * BLK] + pv3[:BLK], pv12[3 * BLK:] + pv3[BLK:])
            lr_ref[pp, 0] = lrb0 + tot0
            lr_ref[pp, 1] = jnp.concatenate([lre1[BLK:2 * BLK], lre1[3 * BLK:]], axis=0)

        @pl.when(jnp.max(lr_ref[...]) > F32_EXP2_ZERO)
        def _():
            for pp in range(SB_PAIRS):
                tail(pp, 0, 2 * ti - 3)
                tail(pp, 1, 2 * ti - 2)

    for pp in range(SB_PAIRS):
        for s in range(2):
            o_ref[s * BLK:(s + 1) * BLK, pp * LANES:(pp + 1) * LANES] = acc_ref[pp, s].astype(o_ref.dtype)


def _sb_attention(pbf, batch, seq):
    nt = seq // SB_TQ
    w = SB_PAIRS * LANES
    cq, ck, cv = BF_SBQ // w, BF_SBK // w, BF_SBV // w
    return pl.pallas_call(
        _sb_kernel,
        grid=(batch, SB_W // w, nt),
        in_specs=[
            pl.BlockSpec((SB_TQ, w), lambda b, p, i: (b * nt + i, cq + p)),
            pl.BlockSpec((seq, w), lambda b, p, i: (b, ck + p), pipeline_mode=pl.Buffered(1)),
            pl.BlockSpec((seq, w), lambda b, p, i: (b, cv + p), pipeline_mode=pl.Buffered(1)),
        ],
        out_specs=pl.BlockSpec((SB_TQ, w), lambda b, p, i: (b * nt + i, p)),
        out_shape=jax.ShapeDtypeStruct((batch * seq, SB_W), BF16),
        scratch_shapes=[pltpu.VMEM((SB_PAIRS, 2, BLK, LANES), F32),
                        pltpu.VMEM((SB_PAIRS, 2, 2 * BLK, LANES), F32)],
        compiler_params=pltpu.CompilerParams(
            dimension_semantics=("arbitrary", "arbitrary", "arbitrary"), vmem_limit_bytes=VMEM_LIMIT),
        name="sb_attention",
    )(pbf, pbf, pbf)


SW_TQ = 4 * BLK


def _sw_kernel(sinks_ref, slopes_ref, q_ref, kp_ref, kc_ref, vp_ref, vc_ref, o_ref, *, layer):
    ti = pl.program_id(1)
    lane = _iota((BLK, LANES), 1)
    row = _iota((BLK, LANES), 0)
    first = lane < HALF
    cur = lane <= row
    dist = jnp.where(cur, row - lane, row - lane + BLK).astype(F32)
    group = SW_Q_HEADS // SW_KV_HEADS
    units = []
    for s in range(SW_TQ // BLK):
        rs = slice(s * BLK, (s + 1) * BLK)
        for hk in range(SW_KV_HEADS):
            ks = slice(hk * LANES, (hk + 1) * LANES)
            kc, vc = kc_ref[rs, ks], vc_ref[rs, ks]
            if s == 0:
                kp, vp = kp_ref[:, ks], vp_ref[:, ks]
            else:
                kp, vp = kc_ref[(s - 1) * BLK:s * BLK, ks], vc_ref[(s - 1) * BLK:s * BLK, ks]
            pairs = range(hk * group // 2, (hk + 1) * group // 2)
            lhs = []
            for p in pairs:
                qp = q_ref[rs, p * LANES:(p + 1) * LANES]
                zero = jnp.zeros_like(qp)
                lhs += [jnp.where(first, qp, zero), jnp.where(first, zero, qp)]
            lhs = jnp.concatenate(lhs, axis=0)
            units.append(dict(s=s, rs=rs, hk=hk, pairs=pairs, vc=vc, vp=vp,
                              s_cur=_dot_nt(lhs, kc), s_prev=_dot_nt(lhs, kp)))
    for u in units:
        scores = []
        for g in range(group):
            head = u["hk"] * group + g
            gs = slice(g * BLK, (g + 1) * BLK)
            sc = jnp.where(cur, u["s_cur"][gs], u["s_prev"][gs]) - slopes_ref[head] * dist
            if u["s"] == 0:
                sc = jnp.where(jnp.logical_or(cur, ti > 0), sc, -jnp.inf)
            scores.append(sc)
        u["scores"] = scores
        u["m"] = [jnp.maximum(jnp.max(sc, axis=-1, keepdims=True), sinks_ref[layer, u["hk"] * group + g])
                  for g, sc in enumerate(scores)]
    for u in units:
        u["e"] = [jnp.exp(sc - m) for sc, m in zip(u["scores"], u["m"])]
        u["denom"] = [jnp.sum(e, axis=-1, keepdims=True) + jnp.exp(sinks_ref[layer, u["hk"] * group + g] - m)
                      for g, (e, m) in enumerate(zip(u["e"], u["m"]))]
    for u in units:
        prob = [(e / d).astype(BF16) for e, d in zip(u["e"], u["denom"])]
        pz = jnp.zeros_like(prob[0])
        prob_c = jnp.concatenate([jnp.where(cur, p, pz) for p in prob], axis=0)
        prob_p = jnp.concatenate([jnp.where(cur, pz, p) for p in prob], axis=0)
        pv = _dot(prob_c, u["vc"]) + _dot(prob_p, u["vp"])
        for t, p in enumerate(u["pairs"]):
            o_ref[u["rs"], p * LANES:(p + 1) * LANES] = jnp.where(
                first, pv[2 * t * BLK:(2 * t + 1) * BLK], pv[(2 * t + 1) * BLK:(2 * t + 2) * BLK]
            ).astype(o_ref.dtype)


def _sw_attention(pbf, sinks, slopes, layer, batch, seq):
    nt = seq // SW_TQ
    kvw = 2 * SW_KVW
    cq, ck, cv = BF_SWQ // SW_QW, BF_SWK // kvw, BF_SWV // kvw
    per = SW_TQ // BLK

    def prev(b, i):
        return b * nt * per + jnp.maximum(i * per - 1, 0)

    smem = pl.BlockSpec(memory_space=pltpu.SMEM)
    return pl.pallas_call(
        functools.partial(_sw_kernel, layer=layer),
        grid=(batch, nt),
        in_specs=[
            smem, smem,
            pl.BlockSpec((SW_TQ, SW_QW), lambda b, i: (b * nt + i, cq)),
            pl.BlockSpec((BLK, kvw), lambda b, i: (prev(b, i), ck)),
            pl.BlockSpec((SW_TQ, kvw), lambda b, i: (b * nt + i, ck)),
            pl.BlockSpec((BLK, kvw), lambda b, i: (prev(b, i), cv)),
            pl.BlockSpec((SW_TQ, kvw), lambda b, i: (b * nt + i, cv)),
        ],
        out_specs=pl.BlockSpec((SW_TQ, SW_QW), lambda b, i: (b * nt + i, 0)),
        out_shape=jax.ShapeDtypeStruct((batch * seq, SW_QW), BF16),
        compiler_params=pltpu.CompilerParams(
            dimension_semantics=("arbitrary", "arbitrary"), vmem_limit_bytes=VMEM_LIMIT),
        name="sw_attention",
    )(sinks, slopes, pbf, pbf, pbf, pbf, pbf)


GDN_CHUNK = 128


def _split3(x):
    x1 = x.astype(BF16)
    r1 = x - x1.astype(F32)
    x2 = r1.astype(BF16)
    x3 = (r1 - x2.astype(F32)).astype(BF16)
    return x1, x2, x3


def _gdn_kernel(q_ref, k_ref, v_ref, z_ref, ab_ref, conv_ref, small_ref, norm_ref, o_ref,
                carry_ref, state_ref, yq_ref, yk_ref, yv_ref, *, rows):
    n = GDN_CHUNK
    nchunks = rows // n
    step = pl.program_id(1)

    @pl.when(step == 0)
    def _():
        carry_ref[...] = jnp.zeros_like(carry_ref)
        state_ref[...] = jnp.zeros_like(state_ref)

    for idx, (src, dst) in enumerate(((q_ref, yq_ref), (k_ref, yk_ref), (v_ref, yv_ref))):
        cols = slice(idx * GDN_W, (idx + 1) * GDN_W)
        w = conv_ref[:, cols]
        for ci in range(nchunks):
            x = src[ci * n:(ci + 1) * n, :]
            prev8 = carry_ref[:, cols] if ci == 0 else src[ci * n - 8:ci * n, :]
            y = w[GDN_CONV - 1:GDN_CONV, :] * x
            for s in range(1, GDN_CONV):
                y = y + w[GDN_CONV - 1 - s:GDN_CONV - s, :] * _shift_rows(x, prev8, s)
            y = y * jax.nn.sigmoid(y)
            for h in range(GDN_HEADS):
                yh = y[:, h * GDN_HEAD_DIM:(h + 1) * GDN_HEAD_DIM]
                if idx < 2:
                    inv = lax.rsqrt(jnp.sum(yh * yh, axis=-1, keepdims=True) + NORM_EPS)
                    yh = yh * (inv * (GDN_HEAD_DIM ** -0.5) if idx == 0 else inv)
                dst[ci * n:(ci + 1) * n, h * GDN_HEAD_DIM:(h + 1) * GDN_HEAD_DIM] = yh
        carry_ref[:, cols] = src[rows - 8:rows, :]

    r = _iota((n, n), 0)
    c = _iota((n, n), 1)
    causal = r >= c
    strict = r > c
    lower_ones = jnp.where(causal, 1.0, 0.0).astype(BF16)
    head_lane = _iota((n, LANES), 1) < GDN_HEADS

    def sub_blocks(ls):
        return ((r >> (ls + 1)) == (c >> (ls + 1))) & ((r >> ls) != (c >> ls))

    probs = []
    for ci in range(nchunks):
        rs = slice(ci * n, (ci + 1) * n)
        ab = ab_ref[rs, :]
        zab = ab + small_ref[1:2, :]
        softplus = jnp.maximum(zab, 0.0) + jnp.log(1.0 + jnp.exp(-jnp.abs(zab)))
        gbeta = jnp.where(head_lane, -jnp.exp(small_ref[0:1, :]) * softplus, jax.nn.sigmoid(ab))
        gc_all = sum(_dot(lower_ones, part) for part in _split3(gbeta))
        gc_t = gc_all.T
        for h in range(GDN_HEADS):
            hs = slice(h * GDN_HEAD_DIM, (h + 1) * GDN_HEAD_DIM)
            q, k, v = yq_ref[rs, hs], yk_ref[rs, hs], yv_ref[rs, hs]
            gc = jnp.broadcast_to(gc_all[:, h:h + 1], (n, n))
            gc_row = jnp.broadcast_to(gc_t[h:h + 1, :], (n, n))
            g_tot = jnp.broadcast_to(gc[n - 1:n, :], (n, n))
            beta_b = jnp.broadcast_to(gbeta[:, GDN_HEADS + h:GDN_HEADS + h + 1], (n, n))
            decay = jnp.exp(gc - gc_row)
            k_beta = k * beta_b
            k16 = k.astype(BF16)
            e_gc = jnp.exp(gc)
            probs.append(dict(
                rs=rs, hs=hs, h=h,
                lower=jnp.where(strict, _dot_nt(k_beta.astype(BF16), k16) * decay, 0.0),
                a=jnp.where(causal, _dot_nt(q.astype(BF16), k16) * decay, 0.0).astype(BF16),
                qd=(q * e_gc).astype(BF16),
                kdt=(k * jnp.exp(g_tot - gc)).T.astype(BF16),
                gl=jnp.exp(g_tot),
                rhs=jnp.concatenate([v * beta_b, k_beta * e_gc], axis=1)))
    eye = jnp.where(r == c, 1.0, 0.0).astype(F32)
    zero16 = jnp.zeros((n, n), BF16)
    lower16 = [p["lower"].astype(BF16) for p in probs]
    xs = [eye - jnp.where((r >> 1) == (c >> 1), p["lower"], 0.0) for p in probs]
    for ls in range(1, n.bit_length() - 1):
        x16 = [x.astype(BF16) for x in xs]
        ys = [_dot(xb, jnp.where(sub_blocks(ls), l16, zero16)).astype(BF16) for xb, l16 in zip(x16, lower16)]
        xs = [x - _dot(y, xb) for x, y, xb in zip(xs, ys, x16)]
    for p, x in zip(probs, xs):
        uw = _dot(x.astype(BF16), p["rhs"].astype(BF16))
        p["u"] = uw[:, :n]
        p["wq"] = jnp.concatenate([uw[:, n:].astype(BF16), p["qd"]], axis=0)
        p["ak"] = jnp.concatenate([p["a"], p["kdt"]], axis=0)

    gain = norm_ref[...]
    states = [state_ref[h] for h in range(GDN_HEADS)]
    for p in probs:
        h = p["h"]
        sr = _dot(p["wq"], states[h].astype(BF16))
        vn16 = (p["u"] - sr[:n]).astype(BF16)
        vr = _dot(p["ak"], vn16)
        o = sr[n:] + vr[:n]
        states[h] = states[h] * p["gl"] + vr[n:]
        zg = z_ref[p["rs"], p["hs"]]
        o_ref[p["rs"], p["hs"]] = (_rms(o, gain) * (zg * jax.nn.sigmoid(zg))).astype(o_ref.dtype)
    for h in range(GDN_HEADS):
        state_ref[h] = states[h]


def _gdn(pfp, pab, conv_w, small, norm, layer, batch, seq, rows):
    steps = seq // rows
    w = GDN_W
    cq = FP_CQKV // w

    def blk(col):
        return pl.BlockSpec((rows, w), lambda b, i: (b * steps + i, col))

    return pl.pallas_call(
        functools.partial(_gdn_kernel, rows=rows),
        grid=(batch, steps),
        in_specs=[
            blk(cq), blk(cq + 1), blk(cq + 2), blk(FP_CZ // w),
            pl.BlockSpec((rows, LANES), lambda b, i: (b * steps + i, 0)),
            pl.BlockSpec((None, GDN_CONV, 3 * w), lambda b, i: (layer, 0, 0)),
            pl.BlockSpec((None, 8, LANES), lambda b, i: (layer, 0, 0)),
            pl.BlockSpec((None, 1, GDN_HEAD_DIM), lambda b, i: (layer, 0, 0)),
        ],
        out_specs=pl.BlockSpec((rows, w), lambda b, i: (b * steps + i, 0)),
        out_shape=jax.ShapeDtypeStruct((batch * seq, w), BF16),
        scratch_shapes=[
            pltpu.VMEM((8, 3 * w), F32),
            pltpu.VMEM((GDN_HEADS, GDN_HEAD_DIM, GDN_HEAD_DIM), F32),
            pltpu.VMEM((rows, w), F32), pltpu.VMEM((rows, w), F32), pltpu.VMEM((rows, w), F32),
        ],
        compiler_params=pltpu.CompilerParams(
            dimension_semantics=("arbitrary", "arbitrary"), vmem_limit_bytes=VMEM_LIMIT),
        name="gdn",
    )(pfp, pfp, pfp, pfp, pab, conv_w, small, norm)


MERGE_CK = 256


def _merge_kernel(ya_ref, yb_ref, yc_ref, ga_ref, gb_ref, gc_ref, x_ref,
                  wa_ref, wb_ref, wc_ref, wo_ref, gain_ref, o_ref, m_ref):
    ya, yb, yc = ya_ref[...], yb_ref[...], yc_ref[...]
    for c0 in range(0, D_MODEL, MERGE_CK):
        cs = slice(c0, c0 + MERGE_CK)
        m_ref[:, cs] = (jax.nn.sigmoid(ga_ref[:, cs].astype(F32)) * _dot(ya, wa_ref[:, cs])
                        + jax.nn.sigmoid(gb_ref[:, cs].astype(F32)) * _dot(yb, wb_ref[:, cs])
                        + jax.nn.sigmoid(gc_ref[:, cs].astype(F32)) * _dot(yc, wc_ref[:, cs])).astype(BF16)
    o_ref[...] = x_ref[...] + _rms(_dot(m_ref[...], wo_ref[...]), gain_ref[...])


def _merge(ya, yb, yc, pbf, x, wa, wb, wc, wo, gain, layer, tm):
    t, d = x.shape

    def rows(width, col=0):
        return pl.BlockSpec((tm, width), lambda i: (i, col))

    def whole(a):
        return pl.BlockSpec((None,) + a.shape[1:], lambda i: (layer, 0, 0))

    return pl.pallas_call(
        _merge_kernel,
        grid=(t // tm,),
        in_specs=[rows(SB_W), rows(SW_QW), rows(GDN_W),
                  rows(d, BF_GA // d), rows(d, BF_GB // d), rows(d, BF_GC // d), rows(d),
                  whole(wa), whole(wb), whole(wc), whole(wo), whole(gain)],
        out_specs=rows(d),
        out_shape=jax.ShapeDtypeStruct((t, d), F32),
        scratch_shapes=[pltpu.VMEM((tm, d), BF16)],
        compiler_params=pltpu.CompilerParams(
            dimension_semantics=("arbitrary",), vmem_limit_bytes=VMEM_LIMIT),
        name="merge",
    )(ya, yb, yc, pbf, pbf, pbf, x, wa, wb, wc, wo, gain)


FFN_CK = 768
SQRT_2_OVER_PI = 0.7978845608028654


def _gelu_tanh(x):
    inner = x * (SQRT_2_OVER_PI + (SQRT_2_OVER_PI * 0.044715) * (x * x))
    return (0.5 * x) * (1.0 + jnp.tanh(inner))


def _ffn_kernel(x_ref, gpre_ref, wup_ref, conv_ref, wdn_ref, gpost_ref, o_ref, carry_ref, f_ref, *, tm):
    @pl.when(pl.program_id(1) == 0)
    def _():
        carry_ref[...] = jnp.zeros_like(carry_ref)

    x = x_ref[...]
    hn = _rms(x, gpre_ref[...]).astype(BF16)

    def conv(col, width):
        cs = slice(col, col + width)
        hid = _dot(hn, wup_ref[:, cs])
        prev8 = carry_ref[:, cs]
        w = conv_ref[:, cs]
        y = w[FFN_CONV - 1:FFN_CONV, :] * hid
        for s in range(1, FFN_CONV):
            y = y + w[FFN_CONV - 1 - s:FFN_CONV - s, :] * _shift_rows(hid, prev8, s)
        carry_ref[:, cs] = hid[tm - 8:tm, :]
        return y

    col = 0
    while col < D_FF:
        width = min(FFN_CK, D_FF - col)
        f_gate = conv(col, width)
        f_up = conv(D_FF + col, width)
        f_ref[:, col:col + width] = (_gelu_tanh(f_gate) * f_up).astype(BF16)
        col += width
    o_ref[...] = x + _rms(_dot(f_ref[...], wdn_ref[...]), gpost_ref[...])


def _ffn(x, gpre, wup, conv_w, wdn, gpost, layer, batch, seq, tm):
    steps = seq // tm
    d = x.shape[1]

    def whole(a):
        return pl.BlockSpec((None,) + a.shape[1:], lambda b, i: (layer, 0, 0))

    return pl.pallas_call(
        functools.partial(_ffn_kernel, tm=tm),
        grid=(batch, steps),
        in_specs=[pl.BlockSpec((tm, d), lambda b, i: (b * steps + i, 0)),
                  whole(gpre), whole(wup), whole(conv_w), whole(wdn), whole(gpost)],
        out_specs=pl.BlockSpec((tm, d), lambda b, i: (b * steps + i, 0)),
        out_shape=jax.ShapeDtypeStruct(x.shape, F32),
        scratch_shapes=[pltpu.VMEM((8, 2 * D_FF), F32), pltpu.VMEM((tm, D_FF), BF16)],
        compiler_params=pltpu.CompilerParams(
            dimension_semantics=("arbitrary", "arbitrary"), vmem_limit_bytes=VMEM_LIMIT),
        name="ffn",
    )(x, gpre, wup, conv_w, wdn, gpost)


def _pack_w_in(w):
    w = jnp.swapaxes(w, 1, 2).astype(BF16)
    o = 0
    parts = {}
    for name, width in (("aq", SB_W), ("ak", SB_W), ("av", SB_W), ("bq", SW_QW), ("bk", SW_KVW),
                        ("bv", SW_KVW), ("cqkv", 3 * GDN_W), ("cz", GDN_W), ("ca", GDN_HEADS),
                        ("cb", GDN_HEADS), ("ga", D_MODEL), ("gb", D_MODEL), ("gc", D_MODEL)):
        parts[name] = w[:, o:o + width, :]
        o += width

    def dup(t):
        return jnp.concatenate([t[:, h * HALF:(h + 1) * HALF, :] for h in range(SW_KV_HEADS) for _ in range(2)], axis=1)

    cols = [parts["ga"], parts["gb"], parts["gc"],
            parts["aq"] * (SB_HEAD_DIM ** -0.5), parts["ak"], parts["av"],
            parts["bq"] * (SW_HEAD_DIM ** -0.5), dup(parts["bk"]), dup(parts["bv"]),
            parts["cqkv"], parts["cz"]]
    pad = jnp.zeros((w.shape[0], LANES - 2 * GDN_HEADS, w.shape[2]), w.dtype)
    wab = jnp.concatenate([parts["ca"], parts["cb"], pad], axis=1)
    return jnp.concatenate(cols, axis=1).astype(BF16), wab


def _layer(x, batch, seq, layer, p):
    pbf, pfp, pab = _inproj(x, p["ln_mix_pre"], p["w_in"], p["w_ab"], layer, tm=512)
    ya = _sb_attention(pbf, batch, seq)
    yb = _sw_attention(pbf, p["sinks"], p["slopes"], layer, batch, seq)
    yc = _gdn(pfp, pab, p["gdn_conv"], p["gdn_small"], p["gdn_norm"], layer, batch, seq, rows=512)
    x = _merge(ya, yb, yc, pbf, x, p["wa"], p["wb"], p["wc"], p["wo"], p["ln_mix_post"], layer, tm=1024)
    return _ffn(x, p["ln_ffn_pre"], p["w_up"], p["ffn_conv"], p["w_down"], p["ln_ffn_post"],
                layer, batch, seq, tm=512)


def kernel(x, ln_mix_pre, w_in, sw_sinks, gdn_conv, gdn_a_log, gdn_dt_bias, gdn_norm, w_branch_a,
           w_branch_b, w_branch_c, w_out, ln_mix_post, ln_ffn_pre, w_up, ffn_conv, w_down, ln_ffn_post):
    batch, seq, d = x.shape
    depth = w_in.shape[0]
    small = jnp.zeros((depth, 8, LANES), F32)
    small = small.at[:, 0, :GDN_HEADS].set(gdn_a_log).at[:, 1, :GDN_HEADS].set(gdn_dt_bias)
    w_main, w_ab = _pack_w_in(w_in)
    p = {
        "ln_mix_pre": ln_mix_pre[:, None, :], "w_in": w_main, "w_ab": w_ab,
        "sinks": sw_sinks, "slopes": jnp.exp2(-8.0 * jnp.arange(1, SW_Q_HEADS + 1, dtype=F32) / SW_Q_HEADS),
        "gdn_conv": gdn_conv, "gdn_small": small, "gdn_norm": gdn_norm[:, None, :],
        "wa": w_branch_a.astype(BF16), "wb": w_branch_b.astype(BF16),
        "wc": w_branch_c.astype(BF16), "wo": w_out.astype(BF16),
        "ln_mix_post": ln_mix_post[:, None, :], "ln_ffn_pre": ln_ffn_pre[:, None, :],
        "w_up": w_up.astype(BF16), "ffn_conv": ffn_conv, "w_down": w_down.astype(BF16),
        "ln_ffn_post": ln_ffn_post[:, None, :],
    }
    h = x.reshape(batch * seq, d)
    for layer in range(depth):
        h = _layer(h, batch, seq, layer, p)
    return h.reshape(batch, seq, d)
```

```python
import functools

import jax
import jax.numpy as jnp
from jax import lax
from jax.experimental import pallas as pl
from jax.experimental.pallas import tpu as pltpu

F32 = jnp.float32
BF16 = jnp.bfloat16
NORM_EPS = 1e-6

D_MODEL = 1024
SB_HEADS, SB_HEAD_DIM = 8, 64
SW_Q_HEADS, SW_KV_HEADS, SW_HEAD_DIM = 8, 2, 64
GDN_HEADS, GDN_HEAD_DIM, GDN_CONV = 4, 128, 4
D_FF, FFN_CONV = 2816, 3
SB_W = SB_HEADS * SB_HEAD_DIM
SW_QW = SW_Q_HEADS * SW_HEAD_DIM
SW_KVW = SW_KV_HEADS * SW_HEAD_DIM
GDN_W = GDN_HEADS * GDN_HEAD_DIM

BLK = 128
LANES = 128
HALF = 64
VMEM_LIMIT = 56 * 1024 * 1024

BF_GA, BF_GB, BF_GC, BF_SBQ, BF_SBK, BF_SBV, BF_SWQ, BF_SWK, BF_SWV, BF_COLS = (
    0, 1024, 2048, 3072, 3584, 4096, 4608, 5120, 5376, 5632)
FP_CQKV, FP_CZ, FP_COLS = 0, 1536, 2048
PROJ_TN = 512
LOG2E = 1.4426950408889634
F32_EXP2_ZERO = -104.0 * LOG2E


def _dot(a, b):
    return jnp.dot(a, b, preferred_element_type=F32)


def _dot_nt(a, b):
    return lax.dot_general(a, b, (((1,), (1,)), ((), ())), preferred_element_type=F32)


def _iota(shape, dim):
    return lax.broadcasted_iota(jnp.int32, shape, dim)


def _rms(t, gain):
    return t * lax.rsqrt(jnp.mean(t * t, axis=-1, keepdims=True) + NORM_EPS) * gain


def _shift_rows(h, prev8, s):
    r = pltpu.roll(h, s, axis=0)
    row = _iota(h.shape, 0)
    for t in range(s):
        r = jnp.where(row == t, prev8[8 - s + t:8 - s + t + 1, :], r)
    return r


def _inproj_kernel(x_ref, g_ref, w_ref, wab_ref, obf_ref, of_ref, oab_ref):
    xn = _rms(x_ref[...], g_ref[...]).astype(BF16)
    oab_ref[...] = _dot_nt(xn, wab_ref[...])
    tn = PROJ_TN
    for j in range(BF_COLS // tn):
        obf_ref[:, j * tn:(j + 1) * tn] = _dot_nt(xn, w_ref[j * tn:(j + 1) * tn, :]).astype(BF16)
    for j in range(FP_COLS // tn):
        of_ref[:, j * tn:(j + 1) * tn] = _dot_nt(xn, w_ref[BF_COLS + j * tn:BF_COLS + (j + 1) * tn, :])


def _inproj(x, gain, w, wab, layer, tm):
    t, d = x.shape

    def whole(a):
        return pl.BlockSpec((None,) + a.shape[1:], lambda i: (layer, 0, 0), pipeline_mode=pl.Buffered(1))

    return pl.pallas_call(
        _inproj_kernel,
        grid=(t // tm,),
        in_specs=[pl.BlockSpec((tm, d), lambda i: (i, 0)), whole(gain), whole(w), whole(wab)],
        out_specs=[
            pl.BlockSpec((tm, BF_COLS), lambda i: (i, 0)),
            pl.BlockSpec((tm, FP_COLS), lambda i: (i, 0)),
            pl.BlockSpec((tm, LANES), lambda i: (i, 0)),
        ],
        out_shape=[jax.ShapeDtypeStruct((t, BF_COLS), BF16), jax.ShapeDtypeStruct((t, FP_COLS), F32),
                   jax.ShapeDtypeStruct((t, LANES), F32)],
        compiler_params=pltpu.CompilerParams(
            dimension_semantics=("arbitrary",), vmem_limit_bytes=VMEM_LIMIT),
        name="inproj",
    )(x, gain, w, wab)


SB_TQ = 2 * BLK
SB_PAIRS = 4


def _sb_kernel(q_ref, k_ref, v_ref, o_ref, acc_ref, lr_ref):
    ti = pl.program_id(2)
    lane = _iota((BLK, LANES), 1)
    row = _iota((BLK, LANES), 0)
    first = lane < HALF
    tri = lane < row
    full = lane >= 0
    zero_tile = jnp.zeros((BLK, LANES), F32)
    r2 = _iota((BLK, 2 * LANES), 0)
    c2 = _iota((BLK, 2 * LANES), 1)
    suffix_ones = jnp.where((r2 > c2) | (c2 >= LANES), 1.0, 0.0).astype(BF16)
    suffix_ones = jnp.concatenate([suffix_ones, suffix_ones], axis=0)

    def split_heads(q):
        sel = jnp.concatenate([first] * (q.shape[0] // BLK), axis=0)
        zero = jnp.zeros_like(q)
        return jnp.where(sel, q, zero), jnp.where(sel, zero, q)

    def log_parts(z):
        zs = z * LOG2E
        log_stay = -(jnp.maximum(zs, 0.0) + jnp.log2(1.0 + jnp.exp2(-jnp.abs(zs))))
        return log_stay, log_stay + zs

    def suffix_sums(parts):
        hi = [x.astype(BF16) for x in parts]
        lo = [(x - h.astype(F32)).astype(BF16) for x, h in zip(parts, hi)]
        hilo = jnp.concatenate([jnp.concatenate(hi, axis=0), jnp.concatenate(lo, axis=0)], axis=1)
        sums = _dot(hilo, suffix_ones)
        out, o = [], 0
        for x in parts:
            out.append((sums[o:o + x.shape[0], :LANES], sums[o:o + x.shape[0], LANES:]))
            o += x.shape[0]
        return out

    def generic_block(pp, s, j, valid):
        ps = slice(pp * LANES, (pp + 1) * LANES)
        q0, q1 = split_heads(q_ref[s * BLK:(s + 1) * BLK, ps])
        off = pl.multiple_of(j * BLK, BLK)
        k, v = k_ref[pl.ds(off, BLK), ps], v_ref[pl.ds(off, BLK), ps]
        log_stay, log_take = log_parts(_dot_nt(jnp.concatenate([q0, q1], axis=0), k))
        if valid is not None:
            vm = jnp.concatenate([valid, valid], axis=0)
            log_stay = jnp.where(vm, log_stay, 0.0)
        (later, total), = suffix_sums([log_stay])
        w = jnp.exp2(log_take + later + lr_ref[pp, s])
        if valid is not None:
            w = jnp.where(vm, w, 0.0)
        pv = _dot(w.astype(BF16), v)
        acc_ref[pp, s] += jnp.where(first, pv[:BLK], pv[BLK:])
        lr_ref[pp, s] += total

    def tail(pp, s, j0):
        def cond(c):
            j, live = c
            return jnp.logical_and(j >= 0, live)

        def body(c):
            j, _ = c
            generic_block(pp, s, j, None)
            return j - 1, jnp.max(lr_ref[pp, s]) > F32_EXP2_ZERO

        lax.while_loop(cond, body, (j0, jnp.max(lr_ref[pp, s]) > F32_EXP2_ZERO))

    @pl.when(ti == 0)
    def _():
        for pp in range(SB_PAIRS):
            for s in range(2):
                acc_ref[pp, s] = jnp.zeros((BLK, LANES), F32)
                lr_ref[pp, s] = jnp.zeros((2 * BLK, LANES), F32)
                generic_block(pp, s, 2 * ti + s, tri)
                tail(pp, s, 2 * ti + s - 1)

    @pl.when(ti > 0)
    def _():
        base = pl.multiple_of((2 * ti - 2) * BLK, BLK)
        m2 = jnp.concatenate([tri, full, tri, full], axis=0)
        m3 = jnp.concatenate([tri, tri], axis=0)
        P = range(SB_PAIRS)
        ps = [slice(pp * LANES, (pp + 1) * LANES) for pp in P]
        qs = [split_heads(q_ref[:, ps[pp]]) for pp in P]
        kw = [k_ref[pl.ds(base, 4 * BLK), ps[pp]] for pp in P]
        vw = [v_ref[pl.ds(base, 4 * BLK), ps[pp]] for pp in P]
        l_ab = [jnp.concatenate([q0, q1], axis=0) for q0, q1 in qs]
        l_a = [jnp.concatenate([q0[:BLK], q1[:BLK]], axis=0) for q0, q1 in qs]
        l_b = [jnp.concatenate([q0[BLK:], q1[BLK:]], axis=0) for q0, q1 in qs]
        z3 = [_dot_nt(l_b[pp], kw[pp][3 * BLK:]) for pp in P]
        z2 = [_dot_nt(l_ab[pp], kw[pp][2 * BLK:3 * BLK]) for pp in P]
        z1 = [_dot_nt(l_ab[pp], kw[pp][BLK:2 * BLK]) for pp in P]
        z0 = [_dot_nt(l_a[pp], kw[pp][:BLK]) for pp in P]
        lp3 = [log_parts(z) for z in z3]
        s3 = [suffix_sums([jnp.where(m3, ls, 0.0)])[0] for ls, _ in lp3]
        lp2 = [log_parts(z) for z in z2]
        s2 = [suffix_sums([jnp.where(m2, ls, 0.0)])[0] for ls, _ in lp2]
        lp1 = [log_parts(z) for z in z1]
        s1 = [suffix_sums([ls])[0] for ls, _ in lp1]
        lp0 = [log_parts(z) for z in z0]
        s0 = [suffix_sums([ls])[0] for ls, _ in lp0]
        for pp in P:
            (lat3, tot3), (lat2, tot2), (lat1, tot1), (lat0, tot0) = s3[pp], s2[pp], s1[pp], s0[pp]
            lt3, lt2, lt1, lt0 = lp3[pp][1], lp2[pp][1], lp1[pp][1], lp0[pp][1]
            lrb2 = jnp.concatenate([zero_tile, tot3[:BLK], zero_tile, tot3[BLK:]], axis=0)
            lrb1 = lrb2 + tot2
            lre1 = lrb1 + tot1
            lrb0 = jnp.concatenate([lre1[:BLK], lre1[2 * BLK:3 * BLK]], axis=0)
            w3 = jnp.where(m3, jnp.exp2(lt3 + lat3), 0.0)
            w2 = jnp.where(m2, jnp.exp2(lt2 + lat2 + lrb2), 0.0)
            w1 = jnp.exp2(lt1 + lat1 + lrb1)
            w0 = jnp.exp2(lt0 + lat0 + lrb0)
            w12 = jnp.concatenate([w1.astype(BF16), w2.astype(BF16)], axis=1)
            pv12 = _dot(w12, vw[pp][BLK:3 * BLK])
            pv0 = _dot(w0.astype(BF16), vw[pp][:BLK])
            pv3 = _dot(w3.astype(BF16), vw[pp][3 * BLK:])
            acc_ref[pp, 0] = jnp.where(first, pv12[:BLK] + pv0[:BLK], pv12[2 * BLK:3 * BLK] + pv0[BLK:])
            acc_ref[pp, 1] = jnp.where(first, pv12[BLK:2 * BLK] + pv3[:BLK], pv12[3 * BLK:] + pv3[BLK:])
            lr_ref[pp, 0] = lrb0 + tot0
            lr_ref[pp, 1] = jnp.concatenate([lre1[BLK:2 * BLK], lre1[3 * BLK:]], axis=0)

        @pl.when(jnp.max(lr_ref[...]) > F32_EXP2_ZERO)
        def _():
            for pp in range(SB_PAIRS):
                tail(pp, 0, 2 * ti - 3)
                tail(pp, 1, 2 * ti - 2)

    for pp in range(SB_PAIRS):
        for s in range(2):
            o_ref[s * BLK:(s + 1) * BLK, pp * LANES:(pp + 1) * LANES] = acc_ref[pp, s].astype(o_ref.dtype)


def _sb_attention(pbf, batch, seq):
    nt = seq // SB_TQ
    w = SB_PAIRS * LANES
    cq, ck, cv = BF_SBQ // w, BF_SBK // w, BF_SBV // w
    return pl.pallas_call(
        _sb_kernel,
        grid=(batch, SB_W // w, nt),
        in_specs=[
            pl.BlockSpec((SB_TQ, w), lambda b, p, i: (b * nt + i, cq + p)),
            pl.BlockSpec((seq, w), lambda b, p, i: (b, ck + p), pipeline_mode=pl.Buffered(1)),
            pl.BlockSpec((seq, w), lambda b, p, i: (b, cv + p), pipeline_mode=pl.Buffered(1)),
        ],
        out_specs=pl.BlockSpec((SB_TQ, w), lambda b, p, i: (b * nt + i, p)),
        out_shape=jax.ShapeDtypeStruct((batch * seq, SB_W), BF16),
        scratch_shapes=[pltpu.VMEM((SB_PAIRS, 2, BLK, LANES), F32),
                        pltpu.VMEM((SB_PAIRS, 2, 2 * BLK, LANES), F32)],
        compiler_params=pltpu.CompilerParams(
            dimension_semantics=("arbitrary", "arbitrary", "arbitrary"), vmem_limit_bytes=VMEM_LIMIT),
        name="sb_attention",
    )(pbf, pbf, pbf)


SW_TQ = 4 * BLK


def _sw_kernel(sinks_ref, slopes_ref, q_ref, kp_ref, kc_ref, vp_ref, vc_ref, o_ref, *, layer):
    ti = pl.program_id(1)
    lane = _iota((BLK, LANES), 1)
    row = _iota((BLK, LANES), 0)
    first = lane < HALF
    cur = lane <= row
    dist = jnp.where(cur, row - lane, row - lane + BLK).astype(F32)
    group = SW_Q_HEADS // SW_KV_HEADS
    units = []
    for s in range(SW_TQ // BLK):
        rs = slice(s * BLK, (s + 1) * BLK)
        for hk in range(SW_KV_HEADS):
            ks = slice(hk * LANES, (hk + 1) * LANES)
            kc, vc = kc_ref[rs, ks], vc_ref[rs, ks]
            if s == 0:
                kp, vp = kp_ref[:, ks], vp_ref[:, ks]
            else:
                kp, vp = kc_ref[(s - 1) * BLK:s * BLK, ks], vc_ref[(s - 1) * BLK:s * BLK, ks]
            pairs = range(hk * group // 2, (hk + 1) * group // 2)
            lhs = []
            for p in pairs:
                qp = q_ref[rs, p * LANES:(p + 1) * LANES]
                zero = jnp.zeros_like(qp)
                lhs += [jnp.where(first, qp, zero), jnp.where(first, zero, qp)]
            lhs = jnp.concatenate(lhs, axis=0)
            units.append(dict(s=s, rs=rs, hk=hk, pairs=pairs, vc=vc, vp=vp,
                              s_cur=_dot_nt(lhs, kc), s_prev=_dot_nt(lhs, kp)))
    for u in units:
        scores = []
        for g in range(group):
            head = u["hk"] * group + g
            gs = slice(g * BLK, (g + 1) * BLK)
            sc = jnp.where(cur, u["s_cur"][gs], u["s_prev"][gs]) - slopes_ref[head] * dist
            if u["s"] == 0:
                sc = jnp.where(jnp.logical_or(cur, ti > 0), sc, -jnp.inf)
            scores.append(sc)
        u["scores"] = scores
        u["m"] = [jnp.maximum(jnp.max(sc, axis=-1, keepdims=True), sinks_ref[layer, u["hk"] * group + g])
                  for g, sc in enumerate(scores)]
    for u in units:
        u["e"] = [jnp.exp(sc - m) for sc, m in zip(u["scores"], u["m"])]
        u["denom"] = [jnp.sum(e, axis=-1, keepdims=True) + jnp.exp(sinks_ref[layer, u["hk"] * group + g] - m)
                      for g, (e, m) in enumerate(zip(u["e"], u["m"]))]
    for u in units:
        prob = [(e / d).astype(BF16) for e, d in zip(u["e"], u["denom"])]
        pz = jnp.zeros_like(prob[0])
        prob_c = jnp.concatenate([jnp.where(cur, p, pz) for p in prob], axis=0)
        prob_p = jnp.concatenate([jnp.where(cur, pz, p) for p in prob], axis=0)
        pv = _dot(prob_c, u["vc"]) + _dot(prob_p, u["vp"])
        for t, p in enumerate(u["pairs"]):
            o_ref[u["rs"], p * LANES:(p + 1) * LANES] = jnp.where(
                first, pv[2 * t * BLK:(2 * t + 1) * BLK], pv[(2 * t + 1) * BLK:(2 * t + 2) * BLK]
            ).astype(o_ref.dtype)


def _sw_attention(pbf, sinks, slopes, layer, batch, seq):
    nt = seq // SW_TQ
    kvw = 2 * SW_KVW
    cq, ck, cv = BF_SWQ // SW_QW, BF_SWK // kvw, BF_SWV // kvw
    per = SW_TQ // BLK

    def prev(b, i):
        return b * nt * per + jnp.maximum(i * per - 1, 0)

    smem = pl.BlockSpec(memory_space=pltpu.SMEM)
    return pl.pallas_call(
        functools.partial(_sw_kernel, layer=layer),
        grid=(batch, nt),
        in_specs=[
            smem, smem,
            pl.BlockSpec((SW_TQ, SW_QW), lambda b, i: (b * nt + i, cq)),
            pl.BlockSpec((BLK, kvw), lambda b, i: (prev(b, i), ck)),
            pl.BlockSpec((SW_TQ, kvw), lambda b, i: (b * nt + i, ck)),
            pl.BlockSpec((BLK, kvw), lambda b, i: (prev(b, i), cv)),
            pl.BlockSpec((SW_TQ, kvw), lambda b, i: (b * nt + i, cv)),
        ],
        out_specs=pl.BlockSpec((SW_TQ, SW_QW), lambda b, i: (b * nt + i, 0)),
        out_shape=jax.ShapeDtypeStruct((batch * seq, SW_QW), BF16),
        compiler_params=pltpu.CompilerParams(
            dimension_semantics=("arbitrary", "arbitrary"), vmem_limit_bytes=VMEM_LIMIT),
        name="sw_attention",
    )(sinks, slopes, pbf, pbf, pbf, pbf, pbf)


GDN_CHUNK = 128


def _split3(x):
    x1 = x.astype(BF16)
    r1 = x - x1.astype(F32)
    x2 = r1.astype(BF16)
    x3 = (r1 - x2.astype(F32)).astype(BF16)
    return x1, x2, x3


def _gdn_kernel(q_ref, k_ref, v_ref, z_ref, ab_ref, conv_ref, small_ref, norm_ref, o_ref,
                carry_ref, state_ref, yq_ref, yk_ref, yv_ref, *, rows):
    n = GDN_CHUNK
    nchunks = rows // n
    step = pl.program_id(1)

    @pl.when(step == 0)
    def _():
        carry_ref[...] = jnp.zeros_like(carry_ref)
        state_ref[...] = jnp.zeros_like(state_ref)

    for idx, (src, dst) in enumerate(((q_ref, yq_ref), (k_ref, yk_ref), (v_ref, yv_ref))):
        cols = slice(idx * GDN_W, (idx + 1) * GDN_W)
        w = conv_ref[:, cols]
        for ci in range(nchunks):
            x = src[ci * n:(ci + 1) * n, :]
            prev8 = carry_ref[:, cols] if ci == 0 else src[ci * n - 8:ci * n, :]
            y = w[GDN_CONV - 1:GDN_CONV, :] * x
            for s in range(1, GDN_CONV):
                y = y + w[GDN_CONV - 1 - s:GDN_CONV - s, :] * _shift_rows(x, prev8, s)
            y = y * jax.nn.sigmoid(y)
            for h in range(GDN_HEADS):
                yh = y[:, h * GDN_HEAD_DIM:(h + 1) * GDN_HEAD_DIM]
                if idx < 2:
                    inv = lax.rsqrt(jnp.sum(yh * yh, axis=-1, keepdims=True) + NORM_EPS)
                    yh = yh * (inv * (GDN_HEAD_DIM ** -0.5) if idx == 0 else inv)
                dst[ci * n:(ci + 1) * n, h * GDN_HEAD_DIM:(h + 1) * GDN_HEAD_DIM] = yh
        carry_ref[:, cols] = src[rows - 8:rows, :]

    r = _iota((n, n), 0)
    c = _iota((n, n), 1)
    causal = r >= c
    strict = r > c
    lower_ones = jnp.where(causal, 1.0, 0.0).astype(BF16)
    head_lane = _iota((n, LANES), 1) < GDN_HEADS

    def sub_blocks(ls):
        return ((r >> (ls + 1)) == (c >> (ls + 1))) & ((r >> ls) != (c >> ls))

    probs = []
    for ci in range(nchunks):
        rs = slice(ci * n, (ci + 1) * n)
        ab = ab_ref[rs, :]
        zab = ab + small_ref[1:2, :]
        softplus = jnp.maximum(zab, 0.0) + jnp.log(1.0 + jnp.exp(-jnp.abs(zab)))
        gbeta = jnp.where(head_lane, -jnp.exp(small_ref[0:1, :]) * softplus, jax.nn.sigmoid(ab))
        gc_all = sum(_dot(lower_ones, part) for part in _split3(gbeta))
        gc_t = gc_all.T
        for h in range(GDN_HEADS):
            hs = slice(h * GDN_HEAD_DIM, (h + 1) * GDN_HEAD_DIM)
            q, k, v = yq_ref[rs, hs], yk_ref[rs, hs], yv_ref[rs, hs]
            gc = jnp.broadcast_to(gc_all[:, h:h + 1], (n, n))
            gc_row = jnp.broadcast_to(gc_t[h:h + 1, :], (n, n))
            g_tot = jnp.broadcast_to(gc[n - 1:n, :], (n, n))
            beta_b = jnp.broadcast_to(gbeta[:, GDN_HEADS + h:GDN_HEADS + h + 1], (n, n))
            decay = jnp.exp(gc - gc_row)
            k_beta = k * beta_b
            k16 = k.astype(BF16)
            e_gc = jnp.exp(gc)
            probs.append(dict(
                rs=rs, hs=hs, h=h,
                lower=jnp.where(strict, _dot_nt(k_beta.astype(BF16), k16) * decay, 0.0),
                a=jnp.where(causal, _dot_nt(q.astype(BF16), k16) * decay, 0.0).astype(BF16),
                qd=(q * e_gc).astype(BF16),
                kdt=(k * jnp.exp(g_tot - gc)).T.astype(BF16),
                gl=jnp.exp(g_tot),
                rhs=jnp.concatenate([v * beta_b, k_beta * e_gc], axis=1)))
    eye = jnp.where(r == c, 1.0, 0.0).astype(F32)
    zero16 = jnp.zeros((n, n), BF16)
    lower16 = [p["lower"].astype(BF16) for p in probs]
    xs = [eye - jnp.where((r >> 1) == (c >> 1), p["lower"], 0.0) for p in probs]
    for ls in range(1, n.bit_length() - 1):
        x16 = [x.astype(BF16) for x in xs]
        ys = [_dot(xb, jnp.where(sub_blocks(ls), l16, zero16)).astype(BF16) for xb, l16 in zip(x16, lower16)]
        xs = [x - _dot(y, xb) for x, y, xb in zip(xs, ys, x16)]
    for p, x in zip(probs, xs):
        uw = _dot(x.astype(BF16), p["rhs"].astype(BF16))
        p["u"] = uw[:, :n]
        p["wq"] = jnp.concatenate([uw[:, n:].astype(BF16), p["qd"]], axis=0)
        p["ak"] = jnp.concatenate([p["a"], p["kdt"]], axis=0)

    gain = norm_ref[...]
    states = [state_ref[h] for h in range(GDN_HEADS)]
    for p in probs:
        h = p["h"]
        sr = _dot(p["wq"], states[h].astype(BF16))
        vn16 = (p["u"] - sr[:n]).astype(BF16)
        vr = _dot(p["ak"], vn16)
        o = sr[n:] + vr[:n]
        states[h] = states[h] * p["gl"] + vr[n:]
        zg = z_ref[p["rs"], p["hs"]]
        o_ref[p["rs"], p["hs"]] = (_rms(o, gain) * (zg * jax.nn.sigmoid(zg))).astype(o_ref.dtype)
    for h in range(GDN_HEADS):
        state_ref[h] = states[h]


def _gdn(pfp, pab, conv_w, small, norm, layer, batch, seq, rows):
    steps = seq // rows
    w = GDN_W
    cq = FP_CQKV // w

    def blk(col):
        return pl.BlockSpec((rows, w), lambda b, i: (b * steps + i, col))

    return pl.pallas_call(
        functools.partial(_gdn_kernel, rows=rows),
        grid=(batch, steps),
        in_specs=[
            blk(cq), blk(cq + 1), blk(cq + 2), blk(FP_CZ // w),
            pl.BlockSpec((rows, LANES), lambda b, i: (b * steps + i, 0)),
            pl.BlockSpec((None, GDN_CONV, 3 * w), lambda b, i: (layer, 0, 0)),
            pl.BlockSpec((None, 8, LANES), lambda b, i: (layer, 0, 0)),
            pl.BlockSpec((None, 1, GDN_HEAD_DIM), lambda b, i: (layer, 0, 0)),
        ],
        out_specs=pl.BlockSpec((rows, w), lambda b, i: (b * steps + i, 0)),
        out_shape=jax.ShapeDtypeStruct((batch * seq, w), BF16),
        scratch_shapes=[
            pltpu.VMEM((8, 3 * w), F32),
            pltpu.VMEM((GDN_HEADS, GDN_HEAD_DIM, GDN_HEAD_DIM), F32),
            pltpu.VMEM((rows, w), F32), pltpu.VMEM((rows, w), F32), pltpu.VMEM((rows, w), F32),
        ],
        compiler_params=pltpu.CompilerParams(
            dimension_semantics=("arbitrary", "arbitrary"), vmem_limit_bytes=VMEM_LIMIT),
        name="gdn",
    )(pfp, pfp, pfp, pfp, pab, conv_w, small, norm)


MERGE_CK = 256


def _merge_kernel(ya_ref, yb_ref, yc_ref, ga_ref, gb_ref, gc_ref, x_ref,
                  wa_ref, wb_ref, wc_ref, wo_ref, gain_ref, o_ref, m_ref):
    ya, yb, yc = ya_ref[...], yb_ref[...], yc_ref[...]
    for c0 in range(0, D_MODEL, MERGE_CK):
        cs = slice(c0, c0 + MERGE_CK)
        m_ref[:, cs] = (jax.nn.sigmoid(ga_ref[:, cs].astype(F32)) * _dot(ya, wa_ref[:, cs])
                        + jax.nn.sigmoid(gb_ref[:, cs].astype(F32)) * _dot(yb, wb_ref[:, cs])
                        + jax.nn.sigmoid(gc_ref[:, cs].astype(F32)) * _dot(yc, wc_ref[:, cs])).astype(BF16)
    o_ref[...] = x_ref[...] + _rms(_dot(m_ref[...], wo_ref[...]), gain_ref[...])


def _merge(ya, yb, yc, pbf, x, wa, wb, wc, wo, gain, layer, tm):
    t, d = x.shape

    def rows(width, col=0):
        return pl.BlockSpec((tm, width), lambda i: (i, col))

    def whole(a):
        return pl.BlockSpec((None,) + a.shape[1:], lambda i: (layer, 0, 0))

    return pl.pallas_call(
        _merge_kernel,
        grid=(t // tm,),
        in_specs=[rows(SB_W), rows(SW_QW), rows(GDN_W),
                  rows(d, BF_GA // d), rows(d, BF_GB // d), rows(d, BF_GC // d), rows(d),
                  whole(wa), whole(wb), whole(wc), whole(wo), whole(gain)],
        out_specs=rows(d),
        out_shape=jax.ShapeDtypeStruct((t, d), F32),
        scratch_shapes=[pltpu.VMEM((tm, d), BF16)],
        compiler_params=pltpu.CompilerParams(
            dimension_semantics=("arbitrary",), vmem_limit_bytes=VMEM_LIMIT),
        name="merge",
    )(ya, yb, yc, pbf, pbf, pbf, x, wa, wb, wc, wo, gain)


FFN_CK = 768
SQRT_2_OVER_PI = 0.7978845608028654


def _gelu_tanh(x):
    inner = x * (SQRT_2_OVER_PI + (SQRT_2_OVER_PI * 0.044715) * (x * x))
    return (0.5 * x) * (1.0 + jnp.tanh(inner))


def _ffn_kernel(x_ref, gpre_ref, wup_ref, conv_ref, wdn_ref, gpost_ref, o_ref, carry_ref, f_ref, *, tm):
    @pl.when(pl.program_id(1) == 0)
    def _():
        carry_ref[...] = jnp.zeros_like(carry_ref)

    x = x_ref[...]
    hn = _rms(x, gpre_ref[...]).astype(BF16)

    def conv(col, width):
        cs = slice(col, col + width)
        hid = _dot(hn, wup_ref[:, cs])
        prev8 = carry_ref[:, cs]
        w = conv_ref[:, cs]
        y = w[FFN_CONV - 1:FFN_CONV, :] * hid
        for s in range(1, FFN_CONV):
            y = y + w[FFN_CONV - 1 - s:FFN_CONV - s, :] * _shift_rows(hid, prev8, s)
        carry_ref[:, cs] = hid[tm - 8:tm, :]
        return y

    col = 0
    while col < D_FF:
        width = min(FFN_CK, D_FF - col)
        f_gate = conv(col, width)
        f_up = conv(D_FF + col, width)
        f_ref[:, col:col + width] = (_gelu_tanh(f_gate) * f_up).astype(BF16)
        col += width
    o_ref[...] = x + _rms(_dot(f_ref[...], wdn_ref[...]), gpost_ref[...])


def _ffn(x, gpre, wup, conv_w, wdn, gpost, layer, batch, seq, tm):
    steps = seq // tm
    d = x.shape[1]

    def whole(a):
        return pl.BlockSpec((None,) + a.shape[1:], lambda b, i: (layer, 0, 0), pipeline_mode=pl.Buffered(1))

    return pl.pallas_call(
        functools.partial(_ffn_kernel, tm=tm),
        grid=(batch, steps),
        in_specs=[pl.BlockSpec((tm, d), lambda b, i: (b * steps + i, 0)),
                  whole(gpre), whole(wup), whole(conv_w), whole(wdn), whole(gpost)],
        out_specs=pl.BlockSpec((tm, d), lambda b, i: (b * steps + i, 0)),
        out_shape=jax.ShapeDtypeStruct(x.shape, F32),
        scratch_shapes=[pltpu.VMEM((8, 2 * D_FF), F32), pltpu.VMEM((tm, D_FF), BF16)],
        compiler_params=pltpu.CompilerParams(
            dimension_semantics=("arbitrary", "arbitrary"), vmem_limit_bytes=VMEM_LIMIT),
        name="ffn",
    )(x, gpre, wup, conv_w, wdn, gpost)


def _pack_w_in(w):
    w = jnp.swapaxes(w, 1, 2).astype(BF16)
    o = 0
    parts = {}
    for name, width in (("aq", SB_W), ("ak", SB_W), ("av", SB_W), ("bq", SW_QW), ("bk", SW_KVW),
                        ("bv", SW_KVW), ("cqkv", 3 * GDN_W), ("cz", GDN_W), ("ca", GDN_HEADS),
                        ("cb", GDN_HEADS), ("ga", D_MODEL), ("gb", D_MODEL), ("gc", D_MODEL)):
        parts[name] = w[:, o:o + width, :]
        o += width

    def dup(t):
        return jnp.concatenate([t[:, h * HALF:(h + 1) * HALF, :] for h in range(SW_KV_HEADS) for _ in range(2)], axis=1)

    cols = [parts["ga"], parts["gb"], parts["gc"],
            parts["aq"] * (SB_HEAD_DIM ** -0.5), parts["ak"], parts["av"],
            parts["bq"] * (SW_HEAD_DIM ** -0.5), dup(parts["bk"]), dup(parts["bv"]),
            parts["cqkv"], parts["cz"]]
    pad = jnp.zeros((w.shape[0], LANES - 2 * GDN_HEADS, w.shape[2]), w.dtype)
    wab = jnp.concatenate([parts["ca"], parts["cb"], pad], axis=1)
    return jnp.concatenate(cols, axis=1).astype(BF16), wab


def _layer(x, batch, seq, layer, p):
    pbf, pfp, pab = _inproj(x, p["ln_mix_pre"], p["w_in"], p["w_ab"], layer, tm=512)
    ya = _sb_attention(pbf, batch, seq)
    yb = _sw_attention(pbf, p["sinks"], p["slopes"], layer, batch, seq)
    yc = _gdn(pfp, pab, p["gdn_conv"], p["gdn_small"], p["gdn_norm"], layer, batch, seq, rows=512)
    x = _merge(ya, yb, yc, pbf, x, p["wa"], p["wb"], p["wc"], p["wo"], p["ln_mix_post"], layer, tm=1024)
    return _ffn(x, p["ln_ffn_pre"], p["w_up"], p["ffn_conv"], p["w_down"], p["ln_ffn_post"],
                layer, batch, seq, tm=1024)


def kernel(x, ln_mix_pre, w_in, sw_sinks, gdn_conv, gdn_a_log, gdn_dt_bias, gdn_norm, w_branch_a,
           w_branch_b, w_branch_c, w_out, ln_mix_post, ln_ffn_pre, w_up, ffn_conv, w_down, ln_ffn_post):
    batch, seq, d = x.shape
    depth = w_in.shape[0]
    small = jnp.zeros((depth, 8, LANES), F32)
    small = small.at[:, 0, :GDN_HEADS].set(gdn_a_log).at[:, 1, :GDN_HEADS].set(gdn_dt_bias)
    w_main, w_ab = _pack_w_in(w_in)
    p = {
        "ln_mix_pre": ln_mix_pre[:, None, :], "w_in": w_main, "w_ab": w_ab,
        "sinks": sw_sinks, "slopes": jnp.exp2(-8.0 * jnp.arange(1, SW_Q_HEADS + 1, dtype=F32) / SW_Q_HEADS),
        "gdn_conv": gdn_conv, "gdn_small": small, "gdn_norm": gdn_norm[:, None, :],
        "wa": w_branch_a.astype(BF16), "wb": w_branch_b.astype(BF16),
        "wc": w_branch_c.astype(BF16), "wo": w_out.astype(BF16),
        "ln_mix_post": ln_mix_post[:, None, :], "ln_ffn_pre": ln_ffn_pre[:, None, :],
        "w_up": w_up.astype(BF16), "ffn_conv": ffn_conv, "w_down": w_down.astype(BF16),
        "ln_ffn_post": ln_ffn_post[:, None, :],
    }
    h = x.reshape(batch * seq, d)
    for layer in range(depth):
        h = _layer(h, batch, seq, layer, p)
    return h.reshape(batch, seq, d)
```

```python
import functools

import jax
import jax.numpy as jnp
from jax import lax
from jax.experimental import pallas as pl
from jax.experimental.pallas import tpu as pltpu

F32 = jnp.float32
BF16 = jnp.bfloat16
NORM_EPS = 1e-6

D_MODEL = 1024
SB_HEADS, SB_HEAD_DIM = 8, 64
SW_Q_HEADS, SW_KV_HEADS, SW_HEAD_DIM = 8, 2, 64
GDN_HEADS, GDN_HEAD_DIM, GDN_CONV = 4, 128, 4
D_FF, FFN_CONV = 2816, 3
SB_W = SB_HEADS * SB_HEAD_DIM
SW_QW = SW_Q_HEADS * SW_HEAD_DIM
SW_KVW = SW_KV_HEADS * SW_HEAD_DIM
GDN_W = GDN_HEADS * GDN_HEAD_DIM

BLK = 128
LANES = 128
HALF = 64
VMEM_LIMIT = 56 * 1024 * 1024

BF_GA, BF_GB, BF_GC, BF_SBQ, BF_SBK, BF_SBV, BF_SWQ, BF_SWK, BF_SWV, BF_COLS = (
    0, 1024, 2048, 3072, 3584, 4096, 4608, 5120, 5376, 5632)
FP_CQKV, FP_CZ, FP_COLS = 0, 1536, 2048
PROJ_TN = 512
LOG2E = 1.4426950408889634
F32_EXP2_ZERO = -104.0 * LOG2E


def _dot(a, b):
    return jnp.dot(a, b, preferred_element_type=F32)


def _dot_nt(a, b):
    return lax.dot_general(a, b, (((1,), (1,)), ((), ())), preferred_element_type=F32)


def _iota(shape, dim):
    return lax.broadcasted_iota(jnp.int32, shape, dim)


def _rms(t, gain):
    return t * lax.rsqrt(jnp.mean(t * t, axis=-1, keepdims=True) + NORM_EPS) * gain


def _shift_rows(h, prev8, s):
    r = pltpu.roll(h, s, axis=0)
    row = _iota(h.shape, 0)
    for t in range(s):
        r = jnp.where(row == t, prev8[8 - s + t:8 - s + t + 1, :], r)
    return r


def _inproj_kernel(x_ref, g_ref, w_ref, wab_ref, conv_ref, small_ref, obf_ref, of_ref, oab_ref, carry_ref,
                   *, tm, tiles_per_seq):
    @pl.when(pl.program_id(0) % tiles_per_seq == 0)
    def _():
        carry_ref[...] = jnp.zeros_like(carry_ref)

    xn = _rms(x_ref[...], g_ref[...]).astype(BF16)
    n = GDN_CHUNK
    ab = _dot_nt(xn, wab_ref[...])
    zab = ab + small_ref[1:2, :]
    softplus = jnp.maximum(zab, 0.0) + jnp.log(1.0 + jnp.exp(-jnp.abs(zab)))
    oab_ref[...] = jnp.where(_iota(ab.shape, 1) < GDN_HEADS,
                             -jnp.exp(small_ref[0:1, :]) * softplus, jax.nn.sigmoid(ab))
    for idx in range(3):
        cols = slice(FP_CQKV + idx * GDN_W, FP_CQKV + (idx + 1) * GDN_W)
        cw = slice(idx * GDN_W, (idx + 1) * GDN_W)
        raw = _dot_nt(xn, w_ref[BF_COLS + cols.start:BF_COLS + cols.stop, :])
        w = conv_ref[:, cw]
        for ci in range(tm // n):
            x = raw[ci * n:(ci + 1) * n, :]
            prev8 = carry_ref[:, cw] if ci == 0 else raw[ci * n - 8:ci * n, :]
            y = w[GDN_CONV - 1:GDN_CONV, :] * x
            for s in range(1, GDN_CONV):
                y = y + w[GDN_CONV - 1 - s:GDN_CONV - s, :] * _shift_rows(x, prev8, s)
            y = y * jax.nn.sigmoid(y)
            for h in range(GDN_HEADS):
                yh = y[:, h * GDN_HEAD_DIM:(h + 1) * GDN_HEAD_DIM]
                if idx < 2:
                    inv = lax.rsqrt(jnp.sum(yh * yh, axis=-1, keepdims=True) + NORM_EPS)
                    yh = yh * (inv * (GDN_HEAD_DIM ** -0.5) if idx == 0 else inv)
                of_ref[ci * n:(ci + 1) * n, cols.start + h * GDN_HEAD_DIM:cols.start + (h + 1) * GDN_HEAD_DIM] = yh
        carry_ref[:, cw] = raw[tm - 8:tm, :]
    z = _dot_nt(xn, w_ref[BF_COLS + FP_CZ:BF_COLS + FP_CZ + GDN_W, :])
    of_ref[:, FP_CZ:FP_CZ + GDN_W] = z * jax.nn.sigmoid(z)
    for c0 in range(0, BF_COLS, PROJ_TN):
        obf_ref[:, c0:c0 + PROJ_TN] = _dot_nt(xn, w_ref[c0:c0 + PROJ_TN, :]).astype(BF16)


def _inproj(x, gain, w, wab, conv_w, small, layer, seq, tm):
    t, d = x.shape

    def whole(a):
        return pl.BlockSpec((None,) + a.shape[1:], lambda i: (layer, 0, 0), pipeline_mode=pl.Buffered(1))

    return pl.pallas_call(
        functools.partial(_inproj_kernel, tm=tm, tiles_per_seq=seq // tm),
        grid=(t // tm,),
        in_specs=[pl.BlockSpec((tm, d), lambda i: (i, 0)), whole(gain), whole(w), whole(wab),
                  whole(conv_w), whole(small)],
        out_specs=[
            pl.BlockSpec((tm, BF_COLS), lambda i: (i, 0)),
            pl.BlockSpec((tm, FP_COLS), lambda i: (i, 0)),
            pl.BlockSpec((tm, LANES), lambda i: (i, 0)),
        ],
        out_shape=[jax.ShapeDtypeStruct((t, BF_COLS), BF16), jax.ShapeDtypeStruct((t, FP_COLS), F32),
                   jax.ShapeDtypeStruct((t, LANES), F32)],
        scratch_shapes=[pltpu.VMEM((8, 3 * GDN_W), F32)],
        compiler_params=pltpu.CompilerParams(
            dimension_semantics=("arbitrary",), vmem_limit_bytes=VMEM_LIMIT),
        name="inproj",
    )(x, gain, w, wab, conv_w, small)


SB_TQ = 2 * BLK
SB_PAIRS = 4


def _sb_kernel(q_ref, k_ref, v_ref, o_ref, acc_ref, lr_ref):
    ti = pl.program_id(2)
    lane = _iota((BLK, LANES), 1)
    row = _iota((BLK, LANES), 0)
    first = lane < HALF
    tri = lane < row
    full = lane >= 0
    zero_tile = jnp.zeros((BLK, LANES), F32)
    r2 = _iota((BLK, 2 * LANES), 0)
    c2 = _iota((BLK, 2 * LANES), 1)
    suffix_ones = jnp.where((r2 > c2) | (c2 >= LANES), 1.0, 0.0).astype(BF16)
    suffix_ones = jnp.concatenate([suffix_ones, suffix_ones], axis=0)

    def split_heads(q):
        sel = jnp.concatenate([first] * (q.shape[0] // BLK), axis=0)
        zero = jnp.zeros_like(q)
        return jnp.where(sel, q, zero), jnp.where(sel, zero, q)

    def log_parts(z):
        zs = z * LOG2E
        log_stay = -(jnp.maximum(zs, 0.0) + jnp.log2(1.0 + jnp.exp2(-jnp.abs(zs))))
        return log_stay, log_stay + zs

    def suffix_sums(parts):
        hi = [x.astype(BF16) for x in parts]
        lo = [(x - h.astype(F32)).astype(BF16) for x, h in zip(parts, hi)]
        hilo = jnp.concatenate([jnp.concatenate(hi, axis=0), jnp.concatenate(lo, axis=0)], axis=1)
        sums = _dot(hilo, suffix_ones)
        out, o = [], 0
        for x in parts:
            out.append((sums[o:o + x.shape[0], :LANES], sums[o:o + x.shape[0], LANES:]))
            o += x.shape[0]
        return out

    def generic_block(pp, s, j, valid):
        ps = slice(pp * LANES, (pp + 1) * LANES)
        q0, q1 = split_heads(q_ref[s * BLK:(s + 1) * BLK, ps])
        off = pl.multiple_of(j * BLK, BLK)
        k, v = k_ref[pl.ds(off, BLK), ps], v_ref[pl.ds(off, BLK), ps]
        log_stay, log_take = log_parts(_dot_nt(jnp.concatenate([q0, q1], axis=0), k))
        if valid is not None:
            vm = jnp.concatenate([valid, valid], axis=0)
            log_stay = jnp.where(vm, log_stay, 0.0)
        (later, total), = suffix_sums([log_stay])
        w = jnp.exp2(log_take + later + lr_ref[pp, s])
        if valid is not None:
            w = jnp.where(vm, w, 0.0)
        pv = _dot(w.astype(BF16), v)
        acc_ref[pp, s] += jnp.where(first, pv[:BLK], pv[BLK:])
        lr_ref[pp, s] += total

    def tail(pp, s, j0):
        def cond(c):
            j, live = c
            return jnp.logical_and(j >= 0, live)

        def body(c):
            j, _ = c
            generic_block(pp, s, j, None)
            return j - 1, jnp.max(lr_ref[pp, s]) > F32_EXP2_ZERO

        lax.while_loop(cond, body, (j0, jnp.max(lr_ref[pp, s]) > F32_EXP2_ZERO))

    @pl.when(ti == 0)
    def _():
        for pp in range(SB_PAIRS):
            for s in range(2):
                acc_ref[pp, s] = jnp.zeros((BLK, LANES), F32)
                lr_ref[pp, s] = jnp.zeros((2 * BLK, LANES), F32)
                generic_block(pp, s, 2 * ti + s, tri)
                tail(pp, s, 2 * ti + s - 1)

    @pl.when(ti > 0)
    def _():
        base = pl.multiple_of((2 * ti - 2) * BLK, BLK)
        m2 = jnp.concatenate([tri, full, tri, full], axis=0)
        m3 = jnp.concatenate([tri, tri], axis=0)
        P = range(SB_PAIRS)
        ps = [slice(pp * LANES, (pp + 1) * LANES) for pp in P]
        qs = [split_heads(q_ref[:, ps[pp]]) for pp in P]
        kw = [k_ref[pl.ds(base, 4 * BLK), ps[pp]] for pp in P]
        vw = [v_ref[pl.ds(base, 4 * BLK), ps[pp]] for pp in P]
        l_ab = [jnp.concatenate([q0, q1], axis=0) for q0, q1 in qs]
        l_a = [jnp.concatenate([q0[:BLK], q1[:BLK]], axis=0) for q0, q1 in qs]
        l_b = [jnp.concatenate([q0[BLK:], q1[BLK:]], axis=0) for q0, q1 in qs]
        z3 = [_dot_nt(l_b[pp], kw[pp][3 * BLK:]) for pp in P]
        z2 = [_dot_nt(l_ab[pp], kw[pp][2 * BLK:3 * BLK]) for pp in P]
        z1 = [_dot_nt(l_ab[pp], kw[pp][BLK:2 * BLK]) for pp in P]
        z0 = [_dot_nt(l_a[pp], kw[pp][:BLK]) for pp in P]
        lp3 = [log_parts(z) for z in z3]
        s3 = [suffix_sums([jnp.where(m3, ls, 0.0)])[0] for ls, _ in lp3]
        lp2 = [log_parts(z) for z in z2]
        s2 = [suffix_sums([jnp.where(m2, ls, 0.0)])[0] for ls, _ in lp2]
        lp1 = [log_parts(z) for z in z1]
        s1 = [suffix_sums([ls])[0] for ls, _ in lp1]
        lp0 = [log_parts(z) for z in z0]
        s0 = [suffix_sums([ls])[0] for ls, _ in lp0]
        for pp in P:
            (lat3, tot3), (lat2, tot2), (lat1, tot1), (lat0, tot0) = s3[pp], s2[pp], s1[pp], s0[pp]
            lt3, lt2, lt1, lt0 = lp3[pp][1], lp2[pp][1], lp1[pp][1], lp0[pp][1]
            lrb2 = jnp.concatenate([zero_tile, tot3[:BLK], zero_tile, tot3[BLK:]], axis=0)
            lrb1 = lrb2 + tot2
            lre1 = lrb1 + tot1
            lrb0 = jnp.concatenate([lre1[:BLK], lre1[2 * BLK:3 * BLK]], axis=0)
            w3 = jnp.where(m3, jnp.exp2(lt3 + lat3), 0.0)
            w2 = jnp.where(m2, jnp.exp2(lt2 + lat2 + lrb2), 0.0)
            w1 = jnp.exp2(lt1 + lat1 + lrb1)
            w0 = jnp.exp2(lt0 + lat0 + lrb0)
            w12 = jnp.concatenate([w1.astype(BF16), w2.astype(BF16)], axis=1)
            pv12 = _dot(w12, vw[pp][BLK:3 * BLK])
            pv0 = _dot(w0.astype(BF16), vw[pp][:BLK])
            pv3 = _dot(w3.astype(BF16), vw[pp][3 * BLK:])
            acc_ref[pp, 0] = jnp.where(first, pv12[:BLK] + pv0[:BLK], pv12[2 * BLK:3 * BLK] + pv0[BLK:])
            acc_ref[pp, 1] = jnp.where(first, pv12[BLK:2 * BLK] + pv3[:BLK], pv12[3 * BLK:] + pv3[BLK:])
            lr_ref[pp, 0] = lrb0 + tot0
            lr_ref[pp, 1] = jnp.concatenate([lre1[BLK:2 * BLK], lre1[3 * BLK:]], axis=0)

        @pl.when(jnp.max(lr_ref[...]) > F32_EXP2_ZERO)
        def _():
            for pp in range(SB_PAIRS):
                tail(pp, 0, 2 * ti - 3)
                tail(pp, 1, 2 * ti - 2)

    for pp in range(SB_PAIRS):
        for s in range(2):
            o_ref[s * BLK:(s + 1) * BLK, pp * LANES:(pp + 1) * LANES] = acc_ref[pp, s].astype(o_ref.dtype)


def _sb_attention(pbf, batch, seq):
    nt = seq // SB_TQ
    w = SB_PAIRS * LANES
    cq, ck, cv = BF_SBQ // w, BF_SBK // w, BF_SBV // w
    return pl.pallas_call(
        _sb_kernel,
        grid=(batch, SB_W // w, nt),
        in_specs=[
            pl.BlockSpec((SB_TQ, w), lambda b, p, i: (b * nt + i, cq + p)),
            pl.BlockSpec((seq, w), lambda b, p, i: (b, ck + p), pipeline_mode=pl.Buffered(1)),
            pl.BlockSpec((seq, w), lambda b, p, i: (b, cv + p), pipeline_mode=pl.Buffered(1)),
        ],
        out_specs=pl.BlockSpec((SB_TQ, w), lambda b, p, i: (b * nt + i, p)),
        out_shape=jax.ShapeDtypeStruct((batch * seq, SB_W), BF16),
        scratch_shapes=[pltpu.VMEM((SB_PAIRS, 2, BLK, LANES), F32),
                        pltpu.VMEM((SB_PAIRS, 2, 2 * BLK, LANES), F32)],
        compiler_params=pltpu.CompilerParams(
            dimension_semantics=("arbitrary", "arbitrary", "arbitrary"), vmem_limit_bytes=VMEM_LIMIT),
        name="sb_attention",
    )(pbf, pbf, pbf)


SW_TQ = 4 * BLK


def _sw_kernel(sinks_ref, slopes_ref, q_ref, kp_ref, kc_ref, vp_ref, vc_ref, o_ref, *, layer):
    ti = pl.program_id(1)
    lane = _iota((BLK, LANES), 1)
    row = _iota((BLK, LANES), 0)
    first = lane < HALF
    cur = lane <= row
    dist = jnp.where(cur, row - lane, row - lane + BLK).astype(F32)
    group = SW_Q_HEADS // SW_KV_HEADS
    units = []
    for s in range(SW_TQ // BLK):
        rs = slice(s * BLK, (s + 1) * BLK)
        for hk in range(SW_KV_HEADS):
            ks = slice(hk * LANES, (hk + 1) * LANES)
            kc, vc = kc_ref[rs, ks], vc_ref[rs, ks]
            if s == 0:
                kp, vp = kp_ref[:, ks], vp_ref[:, ks]
            else:
                kp, vp = kc_ref[(s - 1) * BLK:s * BLK, ks], vc_ref[(s - 1) * BLK:s * BLK, ks]
            pairs = range(hk * group // 2, (hk + 1) * group // 2)
            lhs = []
            for p in pairs:
                qp = q_ref[rs, p * LANES:(p + 1) * LANES]
                zero = jnp.zeros_like(qp)
                lhs += [jnp.where(first, qp, zero), jnp.where(first, zero, qp)]
            lhs = jnp.concatenate(lhs, axis=0)
            units.append(dict(s=s, rs=rs, hk=hk, pairs=pairs, vc=vc, vp=vp,
                              s_cur=_dot_nt(lhs, kc), s_prev=_dot_nt(lhs, kp)))
    for u in units:
        scores = []
        for g in range(group):
            head = u["hk"] * group + g
            gs = slice(g * BLK, (g + 1) * BLK)
            sc = jnp.where(cur, u["s_cur"][gs], u["s_prev"][gs]) - slopes_ref[head] * dist
            if u["s"] == 0:
                sc = jnp.where(jnp.logical_or(cur, ti > 0), sc, -jnp.inf)
            scores.append(sc)
        u["scores"] = scores
        u["m"] = [jnp.maximum(jnp.max(sc, axis=-1, keepdims=True), sinks_ref[layer, u["hk"] * group + g])
                  for g, sc in enumerate(scores)]
    for u in units:
        u["e"] = [jnp.exp(sc - m) for sc, m in zip(u["scores"], u["m"])]
        u["denom"] = [jnp.sum(e, axis=-1, keepdims=True) + jnp.exp(sinks_ref[layer, u["hk"] * group + g] - m)
                      for g, (e, m) in enumerate(zip(u["e"], u["m"]))]
    for u in units:
        prob = [(e / d).astype(BF16) for e, d in zip(u["e"], u["denom"])]
        pz = jnp.zeros_like(prob[0])
        prob_c = jnp.concatenate([jnp.where(cur, p, pz) for p in prob], axis=0)
        prob_p = jnp.concatenate([jnp.where(cur, pz, p) for p in prob], axis=0)
        pv = _dot(prob_c, u["vc"]) + _dot(prob_p, u["vp"])
        for t, p in enumerate(u["pairs"]):
            o_ref[u["rs"], p * LANES:(p + 1) * LANES] = jnp.where(
                first, pv[2 * t * BLK:(2 * t + 1) * BLK], pv[(2 * t + 1) * BLK:(2 * t + 2) * BLK]
            ).astype(o_ref.dtype)


def _sw_attention(pbf, sinks, slopes, layer, batch, seq):
    nt = seq // SW_TQ
    kvw = 2 * SW_KVW
    cq, ck, cv = BF_SWQ // SW_QW, BF_SWK // kvw, BF_SWV // kvw
    per = SW_TQ // BLK

    def prev(b, i):
        return b * nt * per + jnp.maximum(i * per - 1, 0)

    smem = pl.BlockSpec(memory_space=pltpu.SMEM)
    return pl.pallas_call(
        functools.partial(_sw_kernel, layer=layer),
        grid=(batch, nt),
        in_specs=[
            smem, smem,
            pl.BlockSpec((SW_TQ, SW_QW), lambda b, i: (b * nt + i, cq)),
            pl.BlockSpec((BLK, kvw), lambda b, i: (prev(b, i), ck)),
            pl.BlockSpec((SW_TQ, kvw), lambda b, i: (b * nt + i, ck)),
            pl.BlockSpec((BLK, kvw), lambda b, i: (prev(b, i), cv)),
            pl.BlockSpec((SW_TQ, kvw), lambda b, i: (b * nt + i, cv)),
        ],
        out_specs=pl.BlockSpec((SW_TQ, SW_QW), lambda b, i: (b * nt + i, 0)),
        out_shape=jax.ShapeDtypeStruct((batch * seq, SW_QW), BF16),
        compiler_params=pltpu.CompilerParams(
            dimension_semantics=("arbitrary", "arbitrary"), vmem_limit_bytes=VMEM_LIMIT),
        name="sw_attention",
    )(sinks, slopes, pbf, pbf, pbf, pbf, pbf)


GDN_CHUNK = 128


def _split3(x):
    x1 = x.astype(BF16)
    r1 = x - x1.astype(F32)
    x2 = r1.astype(BF16)
    x3 = (r1 - x2.astype(F32)).astype(BF16)
    return x1, x2, x3


def _gdn_kernel(q_ref, k_ref, v_ref, z_ref, gb_ref, norm_ref, o_ref, state_ref, *, rows):
    n = GDN_CHUNK
    nchunks = rows // n

    @pl.when(pl.program_id(1) == 0)
    def _():
        state_ref[...] = jnp.zeros_like(state_ref)

    r = _iota((n, n), 0)
    c = _iota((n, n), 1)
    causal = r >= c
    strict = r > c
    lower_ones = jnp.where(causal, 1.0, 0.0).astype(BF16)

    def sub_blocks(ls):
        return ((r >> (ls + 1)) == (c >> (ls + 1))) & ((r >> ls) != (c >> ls))

    probs = []
    for ci in range(nchunks):
        rs = slice(ci * n, (ci + 1) * n)
        gbeta = gb_ref[rs, :]
        gc_all = sum(_dot(lower_ones, part) for part in _split3(gbeta))
        gc_t = gc_all.T
        for h in range(GDN_HEADS):
            hs = slice(h * GDN_HEAD_DIM, (h + 1) * GDN_HEAD_DIM)
            q, k, v = q_ref[rs, hs], k_ref[rs, hs], v_ref[rs, hs]
            gc = jnp.broadcast_to(gc_all[:, h:h + 1], (n, n))
            gc_row = jnp.broadcast_to(gc_t[h:h + 1, :], (n, n))
            g_tot = jnp.broadcast_to(gc[n - 1:n, :], (n, n))
            beta_b = jnp.broadcast_to(gbeta[:, GDN_HEADS + h:GDN_HEADS + h + 1], (n, n))
            decay = jnp.exp(gc - gc_row)
            k_beta = k * beta_b
            k16 = k.astype(BF16)
            e_gc = jnp.exp(gc)
            probs.append(dict(
                rs=rs, hs=hs, h=h,
                lower=jnp.where(strict, _dot_nt(k_beta.astype(BF16), k16) * decay, 0.0),
                a=jnp.where(causal, _dot_nt(q.astype(BF16), k16) * decay, 0.0).astype(BF16),
                qd=(q * e_gc).astype(BF16),
                kdt=(k * jnp.exp(g_tot - gc)).T.astype(BF16),
                gl=jnp.exp(g_tot),
                rhs=jnp.concatenate([v * beta_b, k_beta * e_gc], axis=1)))
    eye = jnp.where(r == c, 1.0, 0.0).astype(F32)
    zero16 = jnp.zeros((n, n), BF16)
    lower16 = [p["lower"].astype(BF16) for p in probs]
    xs = [eye - jnp.where((r >> 1) == (c >> 1), p["lower"], 0.0) for p in probs]
    for ls in range(1, n.bit_length() - 1):
        x16 = [x.astype(BF16) for x in xs]
        ys = [_dot(xb, jnp.where(sub_blocks(ls), l16, zero16)).astype(BF16) for xb, l16 in zip(x16, lower16)]
        xs = [x - _dot(y, xb) for x, y, xb in zip(xs, ys, x16)]
    for p, x in zip(probs, xs):
        uw = _dot(x.astype(BF16), p["rhs"].astype(BF16))
        p["u"] = uw[:, :n]
        p["wq"] = jnp.concatenate([uw[:, n:].astype(BF16), p["qd"]], axis=0)
        p["ak"] = jnp.concatenate([p["a"], p["kdt"]], axis=0)

    gain = norm_ref[...]
    states = [state_ref[h] for h in range(GDN_HEADS)]
    for p in probs:
        h = p["h"]
        sr = _dot(p["wq"], states[h].astype(BF16))
        vn16 = (p["u"] - sr[:n]).astype(BF16)
        vr = _dot(p["ak"], vn16)
        o = sr[n:] + vr[:n]
        states[h] = states[h] * p["gl"] + vr[n:]
        o_ref[p["rs"], p["hs"]] = (_rms(o, gain) * z_ref[p["rs"], p["hs"]]).astype(o_ref.dtype)
    for h in range(GDN_HEADS):
        state_ref[h] = states[h]


def _gdn(pfp, pgb, norm, layer, batch, seq, rows):
    steps = seq // rows
    w = GDN_W
    cq = FP_CQKV // w

    def blk(col):
        return pl.BlockSpec((rows, w), lambda b, i: (b * steps + i, col))

    return pl.pallas_call(
        functools.partial(_gdn_kernel, rows=rows),
        grid=(batch, steps),
        in_specs=[
            blk(cq), blk(cq + 1), blk(cq + 2), blk(FP_CZ // w),
            pl.BlockSpec((rows, LANES), lambda b, i: (b * steps + i, 0)),
            pl.BlockSpec((None, 1, GDN_HEAD_DIM), lambda b, i: (layer, 0, 0)),
        ],
        out_specs=pl.BlockSpec((rows, w), lambda b, i: (b * steps + i, 0)),
        out_shape=jax.ShapeDtypeStruct((batch * seq, w), BF16),
        scratch_shapes=[pltpu.VMEM((GDN_HEADS, GDN_HEAD_DIM, GDN_HEAD_DIM), F32)],
        compiler_params=pltpu.CompilerParams(
            dimension_semantics=("arbitrary", "arbitrary"), vmem_limit_bytes=VMEM_LIMIT),
        name="gdn",
    )(pfp, pfp, pfp, pfp, pgb, norm)


MERGE_CK = 256


def _merge_kernel(ya_ref, yb_ref, yc_ref, ga_ref, gb_ref, gc_ref, x_ref,
                  wa_ref, wb_ref, wc_ref, wo_ref, gain_ref, o_ref, m_ref):
    ya, yb, yc = ya_ref[...], yb_ref[...], yc_ref[...]
    for c0 in range(0, D_MODEL, MERGE_CK):
        cs = slice(c0, c0 + MERGE_CK)
        m_ref[:, cs] = (jax.nn.sigmoid(ga_ref[:, cs].astype(F32)) * _dot(ya, wa_ref[:, cs])
                        + jax.nn.sigmoid(gb_ref[:, cs].astype(F32)) * _dot(yb, wb_ref[:, cs])
                        + jax.nn.sigmoid(gc_ref[:, cs].astype(F32)) * _dot(yc, wc_ref[:, cs])).astype(BF16)
    o_ref[...] = x_ref[...] + _rms(_dot(m_ref[...], wo_ref[...]), gain_ref[...])


def _merge(ya, yb, yc, pbf, x, wa, wb, wc, wo, gain, layer, tm):
    t, d = x.shape

    def rows(width, col=0):
        return pl.BlockSpec((tm, width), lambda i: (i, col))

    def whole(a):
        return pl.BlockSpec((None,) + a.shape[1:], lambda i: (layer, 0, 0))

    return pl.pallas_call(
        _merge_kernel,
        grid=(t // tm,),
        in_specs=[rows(SB_W), rows(SW_QW), rows(GDN_W),
                  rows(d, BF_GA // d), rows(d, BF_GB // d), rows(d, BF_GC // d), rows(d),
                  whole(wa), whole(wb), whole(wc), whole(wo), whole(gain)],
        out_specs=rows(d),
        out_shape=jax.ShapeDtypeStruct((t, d), F32),
        scratch_shapes=[pltpu.VMEM((tm, d), BF16)],
        compiler_params=pltpu.CompilerParams(
            dimension_semantics=("arbitrary",), vmem_limit_bytes=VMEM_LIMIT),
        name="merge",
    )(ya, yb, yc, pbf, pbf, pbf, x, wa, wb, wc, wo, gain)


FFN_CK = 768
SQRT_2_OVER_PI = 0.7978845608028654


def _gelu_tanh(x):
    inner = x * (SQRT_2_OVER_PI + (SQRT_2_OVER_PI * 0.044715) * (x * x))
    return (0.5 * x) * (1.0 + jnp.tanh(inner))


def _ffn_kernel(x_ref, gpre_ref, wup_ref, conv_ref, wdn_ref, gpost_ref, o_ref, carry_ref, f_ref, *, tm):
    @pl.when(pl.program_id(1) == 0)
    def _():
        carry_ref[...] = jnp.zeros_like(carry_ref)

    x = x_ref[...]
    hn = _rms(x, gpre_ref[...]).astype(BF16)

    def conv(col, width):
        cs = slice(col, col + width)
        hid = _dot(hn, wup_ref[:, cs])
        prev8 = carry_ref[:, cs]
        w = conv_ref[:, cs]
        y = w[FFN_CONV - 1:FFN_CONV, :] * hid
        for s in range(1, FFN_CONV):
            y = y + w[FFN_CONV - 1 - s:FFN_CONV - s, :] * _shift_rows(hid, prev8, s)
        carry_ref[:, cs] = hid[tm - 8:tm, :]
        return y

    col = 0
    while col < D_FF:
        width = min(FFN_CK, D_FF - col)
        f_gate = conv(col, width)
        f_up = conv(D_FF + col, width)
        f_ref[:, col:col + width] = (_gelu_tanh(f_gate) * f_up).astype(BF16)
        col += width
    o_ref[...] = x + _rms(_dot(f_ref[...], wdn_ref[...]), gpost_ref[...])


def _ffn(x, gpre, wup, conv_w, wdn, gpost, layer, batch, seq, tm):
    steps = seq // tm
    d = x.shape[1]

    def whole(a):
        return pl.BlockSpec((None,) + a.shape[1:], lambda b, i: (layer, 0, 0), pipeline_mode=pl.Buffered(1))

    return pl.pallas_call(
        functools.partial(_ffn_kernel, tm=tm),
        grid=(batch, steps),
        in_specs=[pl.BlockSpec((tm, d), lambda b, i: (b * steps + i, 0)),
                  whole(gpre), whole(wup), whole(conv_w), whole(wdn), whole(gpost)],
        out_specs=pl.BlockSpec((tm, d), lambda b, i: (b * steps + i, 0)),
        out_shape=jax.ShapeDtypeStruct(x.shape, F32),
        scratch_shapes=[pltpu.VMEM((8, 2 * D_FF), F32), pltpu.VMEM((tm, D_FF), BF16)],
        compiler_params=pltpu.CompilerParams(
            dimension_semantics=("arbitrary", "arbitrary"), vmem_limit_bytes=VMEM_LIMIT),
        name="ffn",
    )(x, gpre, wup, conv_w, wdn, gpost)


def _pack_w_in(w):
    w = jnp.swapaxes(w, 1, 2).astype(BF16)
    o = 0
    parts = {}
    for name, width in (("aq", SB_W), ("ak", SB_W), ("av", SB_W), ("bq", SW_QW), ("bk", SW_KVW),
                        ("bv", SW_KVW), ("cqkv", 3 * GDN_W), ("cz", GDN_W), ("ca", GDN_HEADS),
                        ("cb", GDN_HEADS), ("ga", D_MODEL), ("gb", D_MODEL), ("gc", D_MODEL)):
        parts[name] = w[:, o:o + width, :]
        o += width

    def dup(t):
        return jnp.concatenate([t[:, h * HALF:(h + 1) * HALF, :] for h in range(SW_KV_HEADS) for _ in range(2)], axis=1)

    cols = [parts["ga"], parts["gb"], parts["gc"],
            parts["aq"] * (SB_HEAD_DIM ** -0.5), parts["ak"], parts["av"],
            parts["bq"] * (SW_HEAD_DIM ** -0.5), dup(parts["bk"]), dup(parts["bv"]),
            parts["cqkv"], parts["cz"]]
    pad = jnp.zeros((w.shape[0], LANES - 2 * GDN_HEADS, w.shape[2]), w.dtype)
    wab = jnp.concatenate([parts["ca"], parts["cb"], pad], axis=1)
    return jnp.concatenate(cols, axis=1).astype(BF16), wab


def _layer(x, batch, seq, layer, p):
    pbf, pfp, pgb = _inproj(x, p["ln_mix_pre"], p["w_in"], p["w_ab"], p["gdn_conv"], p["gdn_small"], layer, seq, tm=512)
    ya = _sb_attention(pbf, batch, seq)
    yb = _sw_attention(pbf, p["sinks"], p["slopes"], layer, batch, seq)
    yc = _gdn(pfp, pgb, p["gdn_norm"], layer, batch, seq, rows=512)
    x = _merge(ya, yb, yc, pbf, x, p["wa"], p["wb"], p["wc"], p["wo"], p["ln_mix_post"], layer, tm=1024)
    return _ffn(x, p["ln_ffn_pre"], p["w_up"], p["ffn_conv"], p["w_down"], p["ln_ffn_post"],
                layer, batch, seq, tm=1024)


def kernel(x, ln_mix_pre, w_in, sw_sinks, gdn_conv, gdn_a_log, gdn_dt_bias, gdn_norm, w_branch_a,
           w_branch_b, w_branch_c, w_out, ln_mix_post, ln_ffn_pre, w_up, ffn_conv, w_down, ln_ffn_post):
    batch, seq, d = x.shape
    depth = w_in.shape[0]
    small = jnp.zeros((depth, 8, LANES), F32)
    small = small.at[:, 0, :GDN_HEADS].set(gdn_a_log).at[:, 1, :GDN_HEADS].set(gdn_dt_bias)
    w_main, w_ab = _pack_w_in(w_in)
    p = {
        "ln_mix_pre": ln_mix_pre[:, None, :], "w_in": w_main, "w_ab": w_ab,
        "sinks": sw_sinks, "slopes": jnp.exp2(-8.0 * jnp.arange(1, SW_Q_HEADS + 1, dtype=F32) / SW_Q_HEADS),
        "gdn_conv": gdn_conv, "gdn_small": small, "gdn_norm": gdn_norm[:, None, :],
        "wa": w_branch_a.astype(BF16), "wb": w_branch_b.astype(BF16),
        "wc": w_branch_c.astype(BF16), "wo": w_out.astype(BF16),
        "ln_mix_post": ln_mix_post[:, None, :], "ln_ffn_pre": ln_ffn_pre[:, None, :],
        "w_up": w_up.astype(BF16), "ffn_conv": ffn_conv, "w_down": w_down.astype(BF16),
        "ln_ffn_post": ln_ffn_post[:, None, :],
    }
    h = x.reshape(batch * seq, d)
    for layer in range(depth):
        h = _layer(h, batch, seq, layer, p)
    return h.reshape(batch, seq, d)
```

```python
import functools

import jax
import jax.numpy as jnp
from jax import lax
from jax.experimental import pallas as pl
from jax.experimental.pallas import tpu as pltpu

F32 = jnp.float32
BF16 = jnp.bfloat16
NORM_EPS = 1e-6

D_MODEL = 1024
SB_HEADS, SB_HEAD_DIM = 8, 64
SW_Q_HEADS, SW_KV_HEADS, SW_HEAD_DIM = 8, 2, 64
GDN_HEADS, GDN_HEAD_DIM, GDN_CONV = 4, 128, 4
D_FF, FFN_CONV = 2816, 3
SB_W = SB_HEADS * SB_HEAD_DIM
SW_QW = SW_Q_HEADS * SW_HEAD_DIM
SW_KVW = SW_KV_HEADS * SW_HEAD_DIM
GDN_W = GDN_HEADS * GDN_HEAD_DIM

BLK = 128
LANES = 128
HALF = 64
VMEM_LIMIT = 56 * 1024 * 1024

BF_GA, BF_GB, BF_GC, BF_SBQ, BF_SBK, BF_SBV, BF_SWQ, BF_SWK, BF_SWV, BF_COLS = (
    0, 1024, 2048, 3072, 3584, 4096, 4608, 5120, 5376, 5632)
FP_CQKV, FP_CZ, FP_COLS = 0, 1536, 2048
PROJ_TN = 512
LOG2E = 1.4426950408889634
F32_EXP2_ZERO = -104.0 * LOG2E


def _dot(a, b):
    return jnp.dot(a, b, preferred_element_type=F32)


def _dot_nt(a, b):
    return lax.dot_general(a, b, (((1,), (1,)), ((), ())), preferred_element_type=F32)


def _iota(shape, dim):
    return lax.broadcasted_iota(jnp.int32, shape, dim)


def _rms(t, gain):
    return t * lax.rsqrt(jnp.mean(t * t, axis=-1, keepdims=True) + NORM_EPS) * gain


def _shift_rows(h, prev8, s):
    r = pltpu.roll(h, s, axis=0)
    row = _iota(h.shape, 0)
    for t in range(s):
        r = jnp.where(row == t, prev8[8 - s + t:8 - s + t + 1, :], r)
    return r


def _inproj_kernel(x_ref, g_ref, w_ref, wab_ref, conv_ref, small_ref, obf_ref, of_ref, oab_ref, carry_ref,
                   *, tm, tiles_per_seq):
    @pl.when(pl.program_id(0) % tiles_per_seq == 0)
    def _():
        carry_ref[...] = jnp.zeros_like(carry_ref)

    xn = _rms(x_ref[...], g_ref[...]).astype(BF16)
    n = GDN_CHUNK
    ab = _dot_nt(xn, wab_ref[...])
    zab = ab + small_ref[1:2, :]
    softplus = jnp.maximum(zab, 0.0) + jnp.log(1.0 + jnp.exp(-jnp.abs(zab)))
    oab_ref[...] = jnp.where(_iota(ab.shape, 1) < GDN_HEADS,
                             -jnp.exp(small_ref[0:1, :]) * softplus, jax.nn.sigmoid(ab))
    for idx in range(3):
        cols = slice(FP_CQKV + idx * GDN_W, FP_CQKV + (idx + 1) * GDN_W)
        cw = slice(idx * GDN_W, (idx + 1) * GDN_W)
        raw = _dot_nt(xn, w_ref[BF_COLS + cols.start:BF_COLS + cols.stop, :])
        w = conv_ref[:, cw]
        for ci in range(tm // n):
            x = raw[ci * n:(ci + 1) * n, :]
            prev8 = carry_ref[:, cw] if ci == 0 else raw[ci * n - 8:ci * n, :]
            y = w[GDN_CONV - 1:GDN_CONV, :] * x
            for s in range(1, GDN_CONV):
                y = y + w[GDN_CONV - 1 - s:GDN_CONV - s, :] * _shift_rows(x, prev8, s)
            y = y * jax.nn.sigmoid(y)
            for h in range(GDN_HEADS):
                yh = y[:, h * GDN_HEAD_DIM:(h + 1) * GDN_HEAD_DIM]
                if idx < 2:
                    inv = lax.rsqrt(jnp.sum(yh * yh, axis=-1, keepdims=True) + NORM_EPS)
                    yh = yh * (inv * (GDN_HEAD_DIM ** -0.5) if idx == 0 else inv)
                of_ref[ci * n:(ci + 1) * n, cols.start + h * GDN_HEAD_DIM:cols.start + (h + 1) * GDN_HEAD_DIM] = yh
        carry_ref[:, cw] = raw[tm - 8:tm, :]
    z = _dot_nt(xn, w_ref[BF_COLS + FP_CZ:BF_COLS + FP_CZ + GDN_W, :])
    of_ref[:, FP_CZ:FP_CZ + GDN_W] = z * jax.nn.sigmoid(z)
    for c0 in range(0, BF_COLS, PROJ_TN):
        obf_ref[:, c0:c0 + PROJ_TN] = _dot_nt(xn, w_ref[c0:c0 + PROJ_TN, :]).astype(BF16)


def _inproj(x, gain, w, wab, conv_w, small, layer, seq, tm):
    t, d = x.shape

    def whole(a):
        return pl.BlockSpec((None,) + a.shape[1:], lambda i: (layer, 0, 0), pipeline_mode=pl.Buffered(1))

    return pl.pallas_call(
        functools.partial(_inproj_kernel, tm=tm, tiles_per_seq=seq // tm),
        grid=(t // tm,),
        in_specs=[pl.BlockSpec((tm, d), lambda i: (i, 0)), whole(gain), whole(w), whole(wab),
                  whole(conv_w), whole(small)],
        out_specs=[
            pl.BlockSpec((tm, BF_COLS), lambda i: (i, 0)),
            pl.BlockSpec((tm, FP_COLS), lambda i: (i, 0)),
            pl.BlockSpec((tm, LANES), lambda i: (i, 0)),
        ],
        out_shape=[jax.ShapeDtypeStruct((t, BF_COLS), BF16), jax.ShapeDtypeStruct((t, FP_COLS), F32),
                   jax.ShapeDtypeStruct((t, LANES), F32)],
        scratch_shapes=[pltpu.VMEM((8, 3 * GDN_W), F32)],
        compiler_params=pltpu.CompilerParams(
            dimension_semantics=("arbitrary",), vmem_limit_bytes=VMEM_LIMIT),
        name="inproj",
    )(x, gain, w, wab, conv_w, small)


SB_TQ = 4 * BLK
SB_PAIRS = 4


def _sb_kernel(q_ref, k_ref, v_ref, o_ref, acc_ref, lr_ref, least_ref):
    ti = pl.program_id(2)
    lane = _iota((BLK, LANES), 1)
    row = _iota((BLK, LANES), 0)
    first = lane < HALF
    tri = lane < row
    full = lane >= 0
    zero_tile = jnp.zeros((BLK, LANES), F32)
    r2 = _iota((BLK, 2 * LANES), 0)
    c2 = _iota((BLK, 2 * LANES), 1)
    suffix_ones = jnp.where((r2 > c2) | (c2 >= LANES), 1.0, 0.0).astype(BF16)
    suffix_ones = jnp.concatenate([suffix_ones, suffix_ones], axis=0)

    def split_heads(q):
        sel = jnp.concatenate([first] * (q.shape[0] // BLK), axis=0)
        zero = jnp.zeros_like(q)
        return jnp.where(sel, q, zero), jnp.where(sel, zero, q)

    def log_parts(z):
        zs = z * LOG2E
        log_stay = -(jnp.maximum(zs, 0.0) + jnp.log2(1.0 + jnp.exp2(-jnp.abs(zs))))
        return log_stay, log_stay + zs

    def suffix_sums(parts):
        hi = [x.astype(BF16) for x in parts]
        lo = [(x - h.astype(F32)).astype(BF16) for x, h in zip(parts, hi)]
        hilo = jnp.concatenate([jnp.concatenate(hi, axis=0), jnp.concatenate(lo, axis=0)], axis=1)
        sums = _dot(hilo, suffix_ones)
        out, o = [], 0
        for x in parts:
            out.append((sums[o:o + x.shape[0], :LANES], sums[o:o + x.shape[0], LANES:]))
            o += x.shape[0]
        return out

    def generic_block(pp, s, j, valid):
        ps = slice(pp * LANES, (pp + 1) * LANES)
        q0, q1 = split_heads(q_ref[s * BLK:(s + 1) * BLK, ps])
        off = pl.multiple_of(j * BLK, BLK)
        k, v = k_ref[pl.ds(off, BLK), ps], v_ref[pl.ds(off, BLK), ps]
        log_stay, log_take = log_parts(_dot_nt(jnp.concatenate([q0, q1], axis=0), k))
        if valid is not None:
            vm = jnp.concatenate([valid, valid], axis=0)
            log_stay = jnp.where(vm, log_stay, 0.0)
        (later, total), = suffix_sums([log_stay])
        w = jnp.exp2(log_take + later + lr_ref[pp, s])
        if valid is not None:
            w = jnp.where(vm, w, 0.0)
        pv = _dot(w.astype(BF16), v)
        acc_ref[pp, s] += jnp.where(first, pv[:BLK], pv[BLK:])
        lr_ref[pp, s] += total

    def tail(pp, s, j0):
        def cond(c):
            j, live = c
            return jnp.logical_and(j >= 0, live)

        def body(c):
            j, _ = c
            generic_block(pp, s, j, None)
            return j - 1, jnp.max(lr_ref[pp, s]) > F32_EXP2_ZERO

        lax.while_loop(cond, body, (j0, jnp.max(lr_ref[pp, s]) > F32_EXP2_ZERO))

    ns = SB_TQ // BLK

    def band(off):
        base = pl.multiple_of((ns * ti - off) * BLK, BLK)
        P = range(SB_PAIRS)
        order = range(ns + off - 1, -1, -1)
        ps = [slice(pp * LANES, (pp + 1) * LANES) for pp in P]
        qs = [split_heads(q_ref[:, ps[pp]]) for pp in P]
        kw = [k_ref[pl.ds(base, (ns + off) * BLK), ps[pp]] for pp in P]
        vw = [v_ref[pl.ds(base, (ns + off) * BLK), ps[pp]] for pp in P]

        def subs(t):
            return list(range(max(t - off, 0), min(t - off + 2, ns - 1) + 1))

        def lhs(q, t):
            rs = slice(subs(t)[0] * BLK, (subs(t)[-1] + 1) * BLK)
            return jnp.concatenate([q[0][rs], q[1][rs]], axis=0)

        def mask(t):
            if t - off not in subs(t):
                return None
            return jnp.concatenate([tri if s == t - off else full for s in subs(t)] * 2, axis=0)

        z = {t: [_dot_nt(lhs(qs[pp], t), kw[pp][t * BLK:(t + 1) * BLK]) for pp in P] for t in order}
        lp, sums = {}, {}
        for t in order:
            lp[t] = [log_parts(zz) for zz in z[t]]
            m = mask(t)
            sums[t] = [suffix_sums([ls if m is None else jnp.where(m, ls, 0.0)])[0] for ls, _ in lp[t]]
        least = None
        for pp in P:
            lr = [[None, None] for _ in range(ns)]
            acc = [[None, None] for _ in range(ns)]
            for t in order:
                ss = subs(t)
                later, total = sums[t][pp]
                e = lp[t][pp][1] + later
                if any(lr[s][h] is not None for s in ss for h in range(2)):
                    e = e + jnp.concatenate([zero_tile if lr[s][h] is None else lr[s][h]
                                             for h in range(2) for s in ss], axis=0)
                w = jnp.exp2(e)
                if mask(t) is not None:
                    w = jnp.where(mask(t), w, 0.0)
                pv = _dot(w.astype(BF16), vw[pp][t * BLK:(t + 1) * BLK])
                for h in range(2):
                    for j, s in enumerate(ss):
                        rs = slice((h * len(ss) + j) * BLK, (h * len(ss) + j + 1) * BLK)
                        lr[s][h] = total[rs] if lr[s][h] is None else lr[s][h] + total[rs]
                        acc[s][h] = pv[rs] if acc[s][h] is None else acc[s][h] + pv[rs]
            for s in range(ns):
                acc_ref[pp, s] = jnp.where(first, acc[s][0], acc[s][1])
                lr_ref[pp, s] = jnp.concatenate(lr[s], axis=0)
                for h in range(2):
                    least = lr[s][h] if least is None else jnp.maximum(least, lr[s][h])
        least_ref[...] = least

    @pl.when(ti == 0)
    def _():
        band(0)

    @pl.when(ti > 0)
    def _():
        band(2)

    @pl.when(jnp.max(least_ref[...]) > F32_EXP2_ZERO)
    def _():
        for pp in range(SB_PAIRS):
            for s in range(ns):
                tail(pp, s, ns * ti + s - 3)

    for pp in range(SB_PAIRS):
        for s in range(ns):
            o_ref[s * BLK:(s + 1) * BLK, pp * LANES:(pp + 1) * LANES] = acc_ref[pp, s].astype(o_ref.dtype)


def _sb_attention(pbf, batch, seq):
    nt = seq // SB_TQ
    w = SB_PAIRS * LANES
    cq, ck, cv = BF_SBQ // w, BF_SBK // w, BF_SBV // w
    return pl.pallas_call(
        _sb_kernel,
        grid=(batch, SB_W // w, nt),
        in_specs=[
            pl.BlockSpec((SB_TQ, w), lambda b, p, i: (b * nt + i, cq + p)),
            pl.BlockSpec((seq, w), lambda b, p, i: (b, ck + p), pipeline_mode=pl.Buffered(1)),
            pl.BlockSpec((seq, w), lambda b, p, i: (b, cv + p), pipeline_mode=pl.Buffered(1)),
        ],
        out_specs=pl.BlockSpec((SB_TQ, w), lambda b, p, i: (b * nt + i, p)),
        out_shape=jax.ShapeDtypeStruct((batch * seq, SB_W), BF16),
        scratch_shapes=[pltpu.VMEM((SB_PAIRS, SB_TQ // BLK, BLK, LANES), F32),
                        pltpu.VMEM((SB_PAIRS, SB_TQ // BLK, 2 * BLK, LANES), F32),
                        pltpu.VMEM((BLK, LANES), F32)],
        compiler_params=pltpu.CompilerParams(
            dimension_semantics=("arbitrary", "arbitrary", "arbitrary"), vmem_limit_bytes=VMEM_LIMIT),
        name="sb_attention",
    )(pbf, pbf, pbf)


SW_TQ = 4 * BLK


def _sw_kernel(sinks_ref, slopes_ref, q_ref, kp_ref, kc_ref, vp_ref, vc_ref, o_ref, *, layer):
    ti = pl.program_id(1)
    lane = _iota((BLK, LANES), 1)
    row = _iota((BLK, LANES), 0)
    first = lane < HALF
    cur = lane <= row
    dist = jnp.where(cur, row - lane, row - lane + BLK).astype(F32)
    group = SW_Q_HEADS // SW_KV_HEADS
    units = []
    for s in range(SW_TQ // BLK):
        rs = slice(s * BLK, (s + 1) * BLK)
        for hk in range(SW_KV_HEADS):
            ks = slice(hk * LANES, (hk + 1) * LANES)
            kc, vc = kc_ref[rs, ks], vc_ref[rs, ks]
            if s == 0:
                kp, vp = kp_ref[:, ks], vp_ref[:, ks]
            else:
                kp, vp = kc_ref[(s - 1) * BLK:s * BLK, ks], vc_ref[(s - 1) * BLK:s * BLK, ks]
            pairs = range(hk * group // 2, (hk + 1) * group // 2)
            lhs = []
            for p in pairs:
                qp = q_ref[rs, p * LANES:(p + 1) * LANES]
                zero = jnp.zeros_like(qp)
                lhs += [jnp.where(first, qp, zero), jnp.where(first, zero, qp)]
            lhs = jnp.concatenate(lhs, axis=0)
            units.append(dict(s=s, rs=rs, hk=hk, pairs=pairs, vc=vc, vp=vp,
                              s_cur=_dot_nt(lhs, kc), s_prev=_dot_nt(lhs, kp)))
    for u in units:
        scores = []
        for g in range(group):
            head = u["hk"] * group + g
            gs = slice(g * BLK, (g + 1) * BLK)
            sc = jnp.where(cur, u["s_cur"][gs], u["s_prev"][gs]) - slopes_ref[head] * dist
            if u["s"] == 0:
                sc = jnp.where(jnp.logical_or(cur, ti > 0), sc, -jnp.inf)
            scores.append(sc)
        u["scores"] = scores
        u["m"] = [jnp.maximum(jnp.max(sc, axis=-1, keepdims=True), sinks_ref[layer, u["hk"] * group + g])
                  for g, sc in enumerate(scores)]
    for u in units:
        u["e"] = [jnp.exp(sc - m) for sc, m in zip(u["scores"], u["m"])]
        u["denom"] = [jnp.sum(e, axis=-1, keepdims=True) + jnp.exp(sinks_ref[layer, u["hk"] * group + g] - m)
                      for g, (e, m) in enumerate(zip(u["e"], u["m"]))]
    for u in units:
        prob = [(e / d).astype(BF16) for e, d in zip(u["e"], u["denom"])]
        pz = jnp.zeros_like(prob[0])
        prob_c = jnp.concatenate([jnp.where(cur, p, pz) for p in prob], axis=0)
        prob_p = jnp.concatenate([jnp.where(cur, pz, p) for p in prob], axis=0)
        pv = _dot(prob_c, u["vc"]) + _dot(prob_p, u["vp"])
        for t, p in enumerate(u["pairs"]):
            o_ref[u["rs"], p * LANES:(p + 1) * LANES] = jnp.where(
                first, pv[2 * t * BLK:(2 * t + 1) * BLK], pv[(2 * t + 1) * BLK:(2 * t + 2) * BLK]
            ).astype(o_ref.dtype)


def _sw_attention(pbf, sinks, slopes, layer, batch, seq):
    nt = seq // SW_TQ
    kvw = 2 * SW_KVW
    cq, ck, cv = BF_SWQ // SW_QW, BF_SWK // kvw, BF_SWV // kvw
    per = SW_TQ // BLK

    def prev(b, i):
        return b * nt * per + jnp.maximum(i * per - 1, 0)

    smem = pl.BlockSpec(memory_space=pltpu.SMEM)
    return pl.pallas_call(
        functools.partial(_sw_kernel, layer=layer),
        grid=(batch, nt),
        in_specs=[
            smem, smem,
            pl.BlockSpec((SW_TQ, SW_QW), lambda b, i: (b * nt + i, cq)),
            pl.BlockSpec((BLK, kvw), lambda b, i: (prev(b, i), ck)),
            pl.BlockSpec((SW_TQ, kvw), lambda b, i: (b * nt + i, ck)),
            pl.BlockSpec((BLK, kvw), lambda b, i: (prev(b, i), cv)),
            pl.BlockSpec((SW_TQ, kvw), lambda b, i: (b * nt + i, cv)),
        ],
        out_specs=pl.BlockSpec((SW_TQ, SW_QW), lambda b, i: (b * nt + i, 0)),
        out_shape=jax.ShapeDtypeStruct((batch * seq, SW_QW), BF16),
        compiler_params=pltpu.CompilerParams(
            dimension_semantics=("arbitrary", "arbitrary"), vmem_limit_bytes=VMEM_LIMIT),
        name="sw_attention",
    )(sinks, slopes, pbf, pbf, pbf, pbf, pbf)


GDN_CHUNK = 128


def _split3(x):
    x1 = x.astype(BF16)
    r1 = x - x1.astype(F32)
    x2 = r1.astype(BF16)
    x3 = (r1 - x2.astype(F32)).astype(BF16)
    return x1, x2, x3


def _gdn_kernel(q_ref, k_ref, v_ref, z_ref, gb_ref, norm_ref, o_ref, state_ref, *, rows):
    n = GDN_CHUNK
    nchunks = rows // n

    @pl.when(pl.program_id(1) == 0)
    def _():
        state_ref[...] = jnp.zeros_like(state_ref)

    r = _iota((n, n), 0)
    c = _iota((n, n), 1)
    causal = r >= c
    strict = r > c
    lower_ones = jnp.where(causal, 1.0, 0.0).astype(BF16)

    def sub_blocks(ls):
        return ((r >> (ls + 1)) == (c >> (ls + 1))) & ((r >> ls) != (c >> ls))

    probs = []
    for ci in range(nchunks):
        rs = slice(ci * n, (ci + 1) * n)
        gbeta = gb_ref[rs, :]
        gc_all = sum(_dot(lower_ones, part) for part in _split3(gbeta))
        gc_t = gc_all.T
        for h in range(GDN_HEADS):
            hs = slice(h * GDN_HEAD_DIM, (h + 1) * GDN_HEAD_DIM)
            q, k, v = q_ref[rs, hs], k_ref[rs, hs], v_ref[rs, hs]
            gc = jnp.broadcast_to(gc_all[:, h:h + 1], (n, n))
            gc_row = jnp.broadcast_to(gc_t[h:h + 1, :], (n, n))
            g_tot = jnp.broadcast_to(gc[n - 1:n, :], (n, n))
            beta_b = jnp.broadcast_to(gbeta[:, GDN_HEADS + h:GDN_HEADS + h + 1], (n, n))
            decay = jnp.exp(gc - gc_row)
            k_beta = k * beta_b
            k16 = k.astype(BF16)
            e_gc = jnp.exp(gc)
            probs.append(dict(
                rs=rs, hs=hs, h=h,
                lower=jnp.where(strict, _dot_nt(k_beta.astype(BF16), k16) * decay, 0.0),
                a=jnp.where(causal, _dot_nt(q.astype(BF16), k16) * decay, 0.0).astype(BF16),
                qd=(q * e_gc).astype(BF16),
                kdt=(k * jnp.exp(g_tot - gc)).T.astype(BF16),
                gl=jnp.exp(g_tot),
                rhs=jnp.concatenate([v * beta_b, k_beta * e_gc], axis=1)))
    eye = jnp.where(r == c, 1.0, 0.0).astype(F32)
    zero16 = jnp.zeros((n, n), BF16)
    lower16 = [p["lower"].astype(BF16) for p in probs]
    xs = [eye - jnp.where((r >> 1) == (c >> 1), p["lower"], 0.0) for p in probs]
    for ls in range(1, n.bit_length() - 1):
        x16 = [x.astype(BF16) for x in xs]
        ys = [_dot(xb, jnp.where(sub_blocks(ls), l16, zero16)).astype(BF16) for xb, l16 in zip(x16, lower16)]
        xs = [x - _dot(y, xb) for x, y, xb in zip(xs, ys, x16)]
    for p, x in zip(probs, xs):
        uw = _dot(x.astype(BF16), p["rhs"].astype(BF16))
        p["u"] = uw[:, :n]
        p["wq"] = jnp.concatenate([uw[:, n:].astype(BF16), p["qd"]], axis=0)
        p["ak"] = jnp.concatenate([p["a"], p["kdt"]], axis=0)

    gain = norm_ref[...]
    states = [state_ref[h] for h in range(GDN_HEADS)]
    for p in probs:
        h = p["h"]
        sr = _dot(p["wq"], states[h].astype(BF16))
        vn16 = (p["u"] - sr[:n]).astype(BF16)
        vr = _dot(p["ak"], vn16)
        o = sr[n:] + vr[:n]
        states[h] = states[h] * p["gl"] + vr[n:]
        o_ref[p["rs"], p["hs"]] = (_rms(o, gain) * z_ref[p["rs"], p["hs"]]).astype(o_ref.dtype)
    for h in range(GDN_HEADS):
        state_ref[h] = states[h]


def _gdn(pfp, pgb, norm, layer, batch, seq, rows):
    steps = seq // rows
    w = GDN_W
    cq = FP_CQKV // w

    def blk(col):
        return pl.BlockSpec((rows, w), lambda b, i: (b * steps + i, col))

    return pl.pallas_call(
        functools.partial(_gdn_kernel, rows=rows),
        grid=(batch, steps),
        in_specs=[
            blk(cq), blk(cq + 1), blk(cq + 2), blk(FP_CZ // w),
            pl.BlockSpec((rows, LANES), lambda b, i: (b * steps + i, 0)),
            pl.BlockSpec((None, 1, GDN_HEAD_DIM), lambda b, i: (layer, 0, 0)),
        ],
        out_specs=pl.BlockSpec((rows, w), lambda b, i: (b * steps + i, 0)),
        out_shape=jax.ShapeDtypeStruct((batch * seq, w), BF16),
        scratch_shapes=[pltpu.VMEM((GDN_HEADS, GDN_HEAD_DIM, GDN_HEAD_DIM), F32)],
        compiler_params=pltpu.CompilerParams(
            dimension_semantics=("arbitrary", "arbitrary"), vmem_limit_bytes=VMEM_LIMIT),
        name="gdn",
    )(pfp, pfp, pfp, pfp, pgb, norm)


MERGE_CK = 256


def _merge_kernel(ya_ref, yb_ref, yc_ref, ga_ref, gb_ref, gc_ref, x_ref,
                  wa_ref, wb_ref, wc_ref, wo_ref, gain_ref, o_ref, m_ref):
    ya, yb, yc = ya_ref[...], yb_ref[...], yc_ref[...]
    for c0 in range(0, D_MODEL, MERGE_CK):
        cs = slice(c0, c0 + MERGE_CK)
        m_ref[:, cs] = (jax.nn.sigmoid(ga_ref[:, cs].astype(F32)) * _dot(ya, wa_ref[:, cs])
                        + jax.nn.sigmoid(gb_ref[:, cs].astype(F32)) * _dot(yb, wb_ref[:, cs])
                        + jax.nn.sigmoid(gc_ref[:, cs].astype(F32)) * _dot(yc, wc_ref[:, cs])).astype(BF16)
    o_ref[...] = x_ref[...] + _rms(_dot(m_ref[...], wo_ref[...]), gain_ref[...])


def _merge(ya, yb, yc, pbf, x, wa, wb, wc, wo, gain, layer, tm):
    t, d = x.shape

    def rows(width, col=0):
        return pl.BlockSpec((tm, width), lambda i: (i, col))

    def whole(a):
        return pl.BlockSpec((None,) + a.shape[1:], lambda i: (layer, 0, 0))

    return pl.pallas_call(
        _merge_kernel,
        grid=(t // tm,),
        in_specs=[rows(SB_W), rows(SW_QW), rows(GDN_W),
                  rows(d, BF_GA // d), rows(d, BF_GB // d), rows(d, BF_GC // d), rows(d),
                  whole(wa), whole(wb), whole(wc), whole(wo), whole(gain)],
        out_specs=rows(d),
        out_shape=jax.ShapeDtypeStruct((t, d), F32),
        scratch_shapes=[pltpu.VMEM((tm, d), BF16)],
        compiler_params=pltpu.CompilerParams(
            dimension_semantics=("arbitrary",), vmem_limit_bytes=VMEM_LIMIT),
        name="merge",
    )(ya, yb, yc, pbf, pbf, pbf, x, wa, wb, wc, wo, gain)


FFN_CK = 768
SQRT_2_OVER_PI = 0.7978845608028654


def _gelu_tanh(x):
    inner = x * (SQRT_2_OVER_PI + (SQRT_2_OVER_PI * 0.044715) * (x * x))
    return (0.5 * x) * (1.0 + jnp.tanh(inner))


def _ffn_kernel(x_ref, gpre_ref, wup_ref, conv_ref, wdn_ref, gpost_ref, o_ref, carry_ref, f_ref, *, tm):
    @pl.when(pl.program_id(1) == 0)
    def _():
        carry_ref[...] = jnp.zeros_like(carry_ref)

    x = x_ref[...]
    hn = _rms(x, gpre_ref[...]).astype(BF16)

    def conv(col, width):
        cs = slice(col, col + width)
        hid = _dot(hn, wup_ref[:, cs])
        prev8 = carry_ref[:, cs]
        w = conv_ref[:, cs]
        y = w[FFN_CONV - 1:FFN_CONV, :] * hid
        for s in range(1, FFN_CONV):
            y = y + w[FFN_CONV - 1 - s:FFN_CONV - s, :] * _shift_rows(hid, prev8, s)
        carry_ref[:, cs] = hid[tm - 8:tm, :]
        return y

    col = 0
    while col < D_FF:
        width = min(FFN_CK, D_FF - col)
        f_gate = conv(col, width)
        f_up = conv(D_FF + col, width)
        f_ref[:, col:col + width] = (_gelu_tanh(f_gate) * f_up).astype(BF16)
        col += width
    o_ref[...] = x + _rms(_dot(f_ref[...], wdn_ref[...]), gpost_ref[...])


def _ffn(x, gpre, wup, conv_w, wdn, gpost, layer, batch, seq, tm):
    steps = seq // tm
    d = x.shape[1]

    def whole(a):
        return pl.BlockSpec((None,) + a.shape[1:], lambda b, i: (layer, 0, 0), pipeline_mode=pl.Buffered(1))

    return pl.pallas_call(
        functools.partial(_ffn_kernel, tm=tm),
        grid=(batch, steps),
        in_specs=[pl.BlockSpec((tm, d), lambda b, i: (b * steps + i, 0)),
                  whole(gpre), whole(wup), whole(conv_w), whole(wdn), whole(gpost)],
        out_specs=pl.BlockSpec((tm, d), lambda b, i: (b * steps + i, 0)),
        out_shape=jax.ShapeDtypeStruct(x.shape, F32),
        scratch_shapes=[pltpu.VMEM((8, 2 * D_FF), F32), pltpu.VMEM((tm, D_FF), BF16)],
        compiler_params=pltpu.CompilerParams(
            dimension_semantics=("arbitrary", "arbitrary"), vmem_limit_bytes=VMEM_LIMIT),
        name="ffn",
    )(x, gpre, wup, conv_w, wdn, gpost)


def _pack_w_in(w):
    w = jnp.swapaxes(w, 1, 2).astype(BF16)
    o = 0
    parts = {}
    for name, width in (("aq", SB_W), ("ak", SB_W), ("av", SB_W), ("bq", SW_QW), ("bk", SW_KVW),
                        ("bv", SW_KVW), ("cqkv", 3 * GDN_W), ("cz", GDN_W), ("ca", GDN_HEADS),
                        ("cb", GDN_HEADS), ("ga", D_MODEL), ("gb", D_MODEL), ("gc", D_MODEL)):
        parts[name] = w[:, o:o + width, :]
        o += width

    def dup(t):
        return jnp.concatenate([t[:, h * HALF:(h + 1) * HALF, :] for h in range(SW_KV_HEADS) for _ in range(2)], axis=1)

    cols = [parts["ga"], parts["gb"], parts["gc"],
            parts["aq"] * (SB_HEAD_DIM ** -0.5), parts["ak"], parts["av"],
            parts["bq"] * (SW_HEAD_DIM ** -0.5), dup(parts["bk"]), dup(parts["bv"]),
            parts["cqkv"], parts["cz"]]
    pad = jnp.zeros((w.shape[0], LANES - 2 * GDN_HEADS, w.shape[2]), w.dtype)
    wab = jnp.concatenate([parts["ca"], parts["cb"], pad], axis=1)
    return jnp.concatenate(cols, axis=1).astype(BF16), wab


def _layer(x, batch, seq, layer, p):
    pbf, pfp, pgb = _inproj(x, p["ln_mix_pre"], p["w_in"], p["w_ab"], p["gdn_conv"], p["gdn_small"], layer, seq, tm=512)
    ya = _sb_attention(pbf, batch, seq)
    yb = _sw_attention(pbf, p["sinks"], p["slopes"], layer, batch, seq)
    yc = _gdn(pfp, pgb, p["gdn_norm"], layer, batch, seq, rows=512)
    x = _merge(ya, yb, yc, pbf, x, p["wa"], p["wb"], p["wc"], p["wo"], p["ln_mix_post"], layer, tm=1024)
    return _ffn(x, p["ln_ffn_pre"], p["w_up"], p["ffn_conv"], p["w_down"], p["ln_ffn_post"],
                layer, batch, seq, tm=1024)


def kernel(x, ln_mix_pre, w_in, sw_sinks, gdn_conv, gdn_a_log, gdn_dt_bias, gdn_norm, w_branch_a,
           w_branch_b, w_branch_c, w_out, ln_mix_post, ln_ffn_pre, w_up, ffn_conv, w_down, ln_ffn_post):
    batch, seq, d = x.shape
    depth = w_in.shape[0]
    small = jnp.zeros((depth, 8, LANES), F32)
    small = small.at[:, 0, :GDN_HEADS].set(gdn_a_log).at[:, 1, :GDN_HEADS].set(gdn_dt_bias)
    w_main, w_ab = _pack_w_in(w_in)
    p = {
        "ln_mix_pre": ln_mix_pre[:, None, :], "w_in": w_main, "w_ab": w_ab,
        "sinks": sw_sinks, "slopes": jnp.exp2(-8.0 * jnp.arange(1, SW_Q_HEADS + 1, dtype=F32) / SW_Q_HEADS),
        "gdn_conv": gdn_conv, "gdn_small": small, "gdn_norm": gdn_norm[:, None, :],
        "wa": w_branch_a.astype(BF16), "wb": w_branch_b.astype(BF16),
        "wc": w_branch_c.astype(BF16), "wo": w_out.astype(BF16),
        "ln_mix_post": ln_mix_post[:, None, :], "ln_ffn_pre": ln_ffn_pre[:, None, :],
        "w_up": w_up.astype(BF16), "ffn_conv": ffn_conv, "w_down": w_down.astype(BF16),
        "ln_ffn_post": ln_ffn_post[:, None, :],
    }
    h = x.reshape(batch * seq, d)
    for layer in range(depth):
        h = _layer(h, batch, seq, layer, p)
    return h.reshape(batch, seq, d)
```

```python
import functools

import jax
import jax.numpy as jnp
from jax import lax
from jax.experimental import pallas as pl
from jax.experimental.pallas import tpu as pltpu

F32 = jnp.float32
BF16 = jnp.bfloat16
NORM_EPS = 1e-6

D_MODEL = 1024
SB_HEADS, SB_HEAD_DIM = 8, 64
SW_Q_HEADS, SW_KV_HEADS, SW_HEAD_DIM = 8, 2, 64
GDN_HEADS, GDN_HEAD_DIM, GDN_CONV = 4, 128, 4
D_FF, FFN_CONV = 2816, 3
SB_W = SB_HEADS * SB_HEAD_DIM
SW_QW = SW_Q_HEADS * SW_HEAD_DIM
SW_KVW = SW_KV_HEADS * SW_HEAD_DIM
GDN_W = GDN_HEADS * GDN_HEAD_DIM

BLK = 128
LANES = 128
HALF = 64
V7X_VMEM_BYTES = 64 * 1024 * 1024
VMEM_LIMIT = V7X_VMEM_BYTES - 8 * 1024 * 1024

BF_GA, BF_GB, BF_GC, BF_SBQ, BF_SBK, BF_SBV, BF_SWQ, BF_SWK, BF_SWV, BF_COLS = (
    0, 1024, 2048, 3072, 3584, 4096, 4608, 5120, 5376, 5632)
FP_CQKV, FP_CZ, FP_COLS = 0, 1536, 2048

INPROJ_TM, PROJ_TN = 512, 512
SB_TQ, SB_PAIRS = 4 * BLK, 4
SW_TQ = 8 * BLK
GDN_ROWS, GDN_CHUNK = 512, 128
MERGE_TM, MERGE_CK = 1024, 256
FFN_TM, FFN_CK = 1024, 768
LOG2E = 1.4426950408889634
F32_EXP2_ZERO = -104.0 * LOG2E


def _dot(a, b):
    return jnp.dot(a, b, preferred_element_type=F32)


def _dot_nt(a, b):
    return lax.dot_general(a, b, (((1,), (1,)), ((), ())), preferred_element_type=F32)


def _iota(shape, dim):
    return lax.broadcasted_iota(jnp.int32, shape, dim)


def _rms(t, gain):
    return t * lax.rsqrt(jnp.mean(t * t, axis=-1, keepdims=True) + NORM_EPS) * gain


def _shift_rows(h, prev8, s):
    r = pltpu.roll(h, s, axis=0)
    row = _iota(h.shape, 0)
    for t in range(s):
        r = jnp.where(row == t, prev8[8 - s + t:8 - s + t + 1, :], r)
    return r


def _inproj_kernel(x_ref, g_ref, w_ref, wab_ref, conv_ref, small_ref, obf_ref, of_ref, oab_ref, carry_ref,
                   *, tm, tiles_per_seq):
    @pl.when(pl.program_id(0) % tiles_per_seq == 0)
    def _():
        carry_ref[...] = jnp.zeros_like(carry_ref)

    xn = _rms(x_ref[...], g_ref[...]).astype(BF16)
    n = GDN_CHUNK
    ab = _dot_nt(xn, wab_ref[...])
    zab = ab + small_ref[1:2, :]
    softplus = jnp.maximum(zab, 0.0) + jnp.log(1.0 + jnp.exp(-jnp.abs(zab)))
    oab_ref[...] = jnp.where(_iota(ab.shape, 1) < GDN_HEADS,
                             -jnp.exp(small_ref[0:1, :]) * softplus, jax.nn.sigmoid(ab))
    for idx in range(3):
        cols = slice(FP_CQKV + idx * GDN_W, FP_CQKV + (idx + 1) * GDN_W)
        cw = slice(idx * GDN_W, (idx + 1) * GDN_W)
        raw = _dot_nt(xn, w_ref[BF_COLS + cols.start:BF_COLS + cols.stop, :])
        w = conv_ref[:, cw]
        for ci in range(tm // n):
            x = raw[ci * n:(ci + 1) * n, :]
            prev8 = carry_ref[:, cw] if ci == 0 else raw[ci * n - 8:ci * n, :]
            y = w[GDN_CONV - 1:GDN_CONV, :] * x
            for s in range(1, GDN_CONV):
                y = y + w[GDN_CONV - 1 - s:GDN_CONV - s, :] * _shift_rows(x, prev8, s)
            y = y * jax.nn.sigmoid(y)
            for h in range(GDN_HEADS):
                yh = y[:, h * GDN_HEAD_DIM:(h + 1) * GDN_HEAD_DIM]
                if idx < 2:
                    inv = lax.rsqrt(jnp.sum(yh * yh, axis=-1, keepdims=True) + NORM_EPS)
                    yh = yh * (inv * (GDN_HEAD_DIM ** -0.5) if idx == 0 else inv)
                of_ref[ci * n:(ci + 1) * n, cols.start + h * GDN_HEAD_DIM:cols.start + (h + 1) * GDN_HEAD_DIM] = yh
        carry_ref[:, cw] = raw[tm - 8:tm, :]
    z = _dot_nt(xn, w_ref[BF_COLS + FP_CZ:BF_COLS + FP_CZ + GDN_W, :])
    of_ref[:, FP_CZ:FP_CZ + GDN_W] = z * jax.nn.sigmoid(z)
    for c0 in range(0, BF_COLS, PROJ_TN):
        obf_ref[:, c0:c0 + PROJ_TN] = _dot_nt(xn, w_ref[c0:c0 + PROJ_TN, :]).astype(BF16)


def _inproj(x, gain, w, wab, conv_w, small, layer, seq, tm):
    t, d = x.shape

    def whole(a):
        return pl.BlockSpec((None,) + a.shape[1:], lambda i: (layer, 0, 0), pipeline_mode=pl.Buffered(1))

    return pl.pallas_call(
        functools.partial(_inproj_kernel, tm=tm, tiles_per_seq=seq // tm),
        grid=(t // tm,),
        in_specs=[pl.BlockSpec((tm, d), lambda i: (i, 0)), whole(gain), whole(w), whole(wab),
                  whole(conv_w), whole(small)],
        out_specs=[
            pl.BlockSpec((tm, BF_COLS), lambda i: (i, 0)),
            pl.BlockSpec((tm, FP_COLS), lambda i: (i, 0)),
            pl.BlockSpec((tm, LANES), lambda i: (i, 0)),
        ],
        out_shape=[jax.ShapeDtypeStruct((t, BF_COLS), BF16), jax.ShapeDtypeStruct((t, FP_COLS), F32),
                   jax.ShapeDtypeStruct((t, LANES), F32)],
        scratch_shapes=[pltpu.VMEM((8, 3 * GDN_W), F32)],
        compiler_params=pltpu.CompilerParams(
            dimension_semantics=("arbitrary",), vmem_limit_bytes=VMEM_LIMIT),
        name="inproj",
    )(x, gain, w, wab, conv_w, small)


def _sb_kernel(q_ref, k_ref, v_ref, o_ref, acc_ref, lr_ref, least_ref):
    ti = pl.program_id(2)
    lane = _iota((BLK, LANES), 1)
    row = _iota((BLK, LANES), 0)
    first = lane < HALF
    tri = lane < row
    full = lane >= 0
    zero_tile = jnp.zeros((BLK, LANES), F32)
    r2 = _iota((BLK, 2 * LANES), 0)
    c2 = _iota((BLK, 2 * LANES), 1)
    suffix_ones = jnp.where((r2 > c2) | (c2 >= LANES), 1.0, 0.0).astype(BF16)
    suffix_ones = jnp.concatenate([suffix_ones, suffix_ones], axis=0)

    def split_heads(q):
        sel = jnp.concatenate([first] * (q.shape[0] // BLK), axis=0)
        zero = jnp.zeros_like(q)
        return jnp.where(sel, q, zero), jnp.where(sel, zero, q)

    def log_parts(z):
        zs = z * LOG2E
        log_stay = -(jnp.maximum(zs, 0.0) + jnp.log2(1.0 + jnp.exp2(-jnp.abs(zs))))
        return log_stay, log_stay + zs

    def suffix_sums(parts):
        hi = [x.astype(BF16) for x in parts]
        lo = [(x - h.astype(F32)).astype(BF16) for x, h in zip(parts, hi)]
        hilo = jnp.concatenate([jnp.concatenate(hi, axis=0), jnp.concatenate(lo, axis=0)], axis=1)
        sums = _dot(hilo, suffix_ones)
        out, o = [], 0
        for x in parts:
            out.append((sums[o:o + x.shape[0], :LANES], sums[o:o + x.shape[0], LANES:]))
            o += x.shape[0]
        return out

    def generic_block(pp, s, j, valid):
        ps = slice(pp * LANES, (pp + 1) * LANES)
        q0, q1 = split_heads(q_ref[s * BLK:(s + 1) * BLK, ps])
        off = pl.multiple_of(j * BLK, BLK)
        k, v = k_ref[pl.ds(off, BLK), ps], v_ref[pl.ds(off, BLK), ps]
        log_stay, log_take = log_parts(_dot_nt(jnp.concatenate([q0, q1], axis=0), k))
        if valid is not None:
            vm = jnp.concatenate([valid, valid], axis=0)
            log_stay = jnp.where(vm, log_stay, 0.0)
        (later, total), = suffix_sums([log_stay])
        w = jnp.exp2(log_take + later + lr_ref[pp, s])
        if valid is not None:
            w = jnp.where(vm, w, 0.0)
        pv = _dot(w.astype(BF16), v)
        acc_ref[pp, s] += jnp.where(first, pv[:BLK], pv[BLK:])
        lr_ref[pp, s] += total

    def tail(pp, s, j0):
        def cond(c):
            j, live = c
            return jnp.logical_and(j >= 0, live)

        def body(c):
            j, _ = c
            generic_block(pp, s, j, None)
            return j - 1, jnp.max(lr_ref[pp, s]) > F32_EXP2_ZERO

        lax.while_loop(cond, body, (j0, jnp.max(lr_ref[pp, s]) > F32_EXP2_ZERO))

    ns = SB_TQ // BLK

    def band(off):
        base = pl.multiple_of((ns * ti - off) * BLK, BLK)
        P = range(SB_PAIRS)
        order = range(ns + off - 1, -1, -1)
        ps = [slice(pp * LANES, (pp + 1) * LANES) for pp in P]
        qs = [split_heads(q_ref[:, ps[pp]]) for pp in P]
        kw = [k_ref[pl.ds(base, (ns + off) * BLK), ps[pp]] for pp in P]
        vw = [v_ref[pl.ds(base, (ns + off) * BLK), ps[pp]] for pp in P]

        def subs(t):
            return list(range(max(t - off, 0), min(t - off + 2, ns - 1) + 1))

        def lhs(q, t):
            rs = slice(subs(t)[0] * BLK, (subs(t)[-1] + 1) * BLK)
            return jnp.concatenate([q[0][rs], q[1][rs]], axis=0)

        def mask(t):
            if t - off not in subs(t):
                return None
            return jnp.concatenate([tri if s == t - off else full for s in subs(t)] * 2, axis=0)

        z = {t: [_dot_nt(lhs(qs[pp], t), kw[pp][t * BLK:(t + 1) * BLK]) for pp in P] for t in order}
        lp, sums = {}, {}
        for t in order:
            lp[t] = [log_parts(zz) for zz in z[t]]
            m = mask(t)
            sums[t] = [suffix_sums([ls if m is None else jnp.where(m, ls, 0.0)])[0] for ls, _ in lp[t]]
        least = None
        for pp in P:
            lr = [[None, None] for _ in range(ns)]
            acc = [[None, None] for _ in range(ns)]
            for t in order:
                ss = subs(t)
                later, total = sums[t][pp]
                e = lp[t][pp][1] + later
                if any(lr[s][h] is not None for s in ss for h in range(2)):
                    e = e + jnp.concatenate([zero_tile if lr[s][h] is None else lr[s][h]
                                             for h in range(2) for s in ss], axis=0)
                w = jnp.exp2(e)
                if mask(t) is not None:
                    w = jnp.where(mask(t), w, 0.0)
                pv = _dot(w.astype(BF16), vw[pp][t * BLK:(t + 1) * BLK])
                for h in range(2):
                    for j, s in enumerate(ss):
                        rs = slice((h * len(ss) + j) * BLK, (h * len(ss) + j + 1) * BLK)
                        lr[s][h] = total[rs] if lr[s][h] is None else lr[s][h] + total[rs]
                        acc[s][h] = pv[rs] if acc[s][h] is None else acc[s][h] + pv[rs]
            for s in range(ns):
                acc_ref[pp, s] = jnp.where(first, acc[s][0], acc[s][1])
                lr_ref[pp, s] = jnp.concatenate(lr[s], axis=0)
                for h in range(2):
                    least = lr[s][h] if least is None else jnp.maximum(least, lr[s][h])
        least_ref[...] = least

    @pl.when(ti == 0)
    def _():
        band(0)

    @pl.when(ti > 0)
    def _():
        band(2)

    @pl.when(jnp.max(least_ref[...]) > F32_EXP2_ZERO)
    def _():
        for pp in range(SB_PAIRS):
            for s in range(ns):
                tail(pp, s, ns * ti + s - 3)

    for pp in range(SB_PAIRS):
        for s in range(ns):
            o_ref[s * BLK:(s + 1) * BLK, pp * LANES:(pp + 1) * LANES] = acc_ref[pp, s].astype(o_ref.dtype)


def _sb_attention(pbf, batch, seq):
    nt = seq // SB_TQ
    w = SB_PAIRS * LANES
    cq, ck, cv = BF_SBQ // w, BF_SBK // w, BF_SBV // w
    return pl.pallas_call(
        _sb_kernel,
        grid=(batch, SB_W // w, nt),
        in_specs=[
            pl.BlockSpec((SB_TQ, w), lambda b, p, i: (b * nt + i, cq + p)),
            pl.BlockSpec((seq, w), lambda b, p, i: (b, ck + p), pipeline_mode=pl.Buffered(1)),
            pl.BlockSpec((seq, w), lambda b, p, i: (b, cv + p), pipeline_mode=pl.Buffered(1)),
        ],
        out_specs=pl.BlockSpec((SB_TQ, w), lambda b, p, i: (b * nt + i, p)),
        out_shape=jax.ShapeDtypeStruct((batch * seq, SB_W), BF16),
        scratch_shapes=[pltpu.VMEM((SB_PAIRS, SB_TQ // BLK, BLK, LANES), F32),
                        pltpu.VMEM((SB_PAIRS, SB_TQ // BLK, 2 * BLK, LANES), F32),
                        pltpu.VMEM((BLK, LANES), F32)],
        compiler_params=pltpu.CompilerParams(
            dimension_semantics=("arbitrary", "arbitrary", "arbitrary"), vmem_limit_bytes=VMEM_LIMIT),
        name="sb_attention",
    )(pbf, pbf, pbf)


def _sw_kernel(sinks_ref, slopes_ref, q_ref, kp_ref, kc_ref, vp_ref, vc_ref, o_ref, *, layer):
    ti = pl.program_id(1)
    lane = _iota((BLK, LANES), 1)
    row = _iota((BLK, LANES), 0)
    first = lane < HALF
    cur = lane <= row
    dist = jnp.where(cur, row - lane, row - lane + BLK).astype(F32)
    group = SW_Q_HEADS // SW_KV_HEADS
    units = []
    for s in range(SW_TQ // BLK):
        rs = slice(s * BLK, (s + 1) * BLK)
        for hk in range(SW_KV_HEADS):
            ks = slice(hk * LANES, (hk + 1) * LANES)
            kc, vc = kc_ref[rs, ks], vc_ref[rs, ks]
            if s == 0:
                kp, vp = kp_ref[:, ks], vp_ref[:, ks]
            else:
                kp, vp = kc_ref[(s - 1) * BLK:s * BLK, ks], vc_ref[(s - 1) * BLK:s * BLK, ks]
            pairs = range(hk * group // 2, (hk + 1) * group // 2)
            lhs = []
            for p in pairs:
                qp = q_ref[rs, p * LANES:(p + 1) * LANES]
                zero = jnp.zeros_like(qp)
                lhs += [jnp.where(first, qp, zero), jnp.where(first, zero, qp)]
            lhs = jnp.concatenate(lhs, axis=0)
            units.append(dict(s=s, rs=rs, hk=hk, pairs=pairs, vc=vc, vp=vp,
                              s_cur=_dot_nt(lhs, kc), s_prev=_dot_nt(lhs, kp)))
    for u in units:
        scores = []
        for g in range(group):
            head = u["hk"] * group + g
            gs = slice(g * BLK, (g + 1) * BLK)
            sc = jnp.where(cur, u["s_cur"][gs], u["s_prev"][gs]) - slopes_ref[head] * dist
            if u["s"] == 0:
                sc = jnp.where(jnp.logical_or(cur, ti > 0), sc, -jnp.inf)
            scores.append(sc)
        u["scores"] = scores
        u["m"] = [jnp.maximum(jnp.max(sc, axis=-1, keepdims=True), sinks_ref[layer, u["hk"] * group + g])
                  for g, sc in enumerate(scores)]
    for u in units:
        u["e"] = [jnp.exp(sc - m) for sc, m in zip(u["scores"], u["m"])]
        u["denom"] = [jnp.sum(e, axis=-1, keepdims=True) + jnp.exp(sinks_ref[layer, u["hk"] * group + g] - m)
                      for g, (e, m) in enumerate(zip(u["e"], u["m"]))]
    for u in units:
        prob = [(e / d).astype(BF16) for e, d in zip(u["e"], u["denom"])]
        pz = jnp.zeros_like(prob[0])
        prob_c = jnp.concatenate([jnp.where(cur, p, pz) for p in prob], axis=0)
        prob_p = jnp.concatenate([jnp.where(cur, pz, p) for p in prob], axis=0)
        pv = _dot(prob_c, u["vc"]) + _dot(prob_p, u["vp"])
        for t, p in enumerate(u["pairs"]):
            o_ref[u["rs"], p * LANES:(p + 1) * LANES] = jnp.where(
                first, pv[2 * t * BLK:(2 * t + 1) * BLK], pv[(2 * t + 1) * BLK:(2 * t + 2) * BLK]
            ).astype(o_ref.dtype)


def _sw_attention(pbf, sinks, slopes, layer, batch, seq):
    nt = seq // SW_TQ
    kvw = 2 * SW_KVW
    cq, ck, cv = BF_SWQ // SW_QW, BF_SWK // kvw, BF_SWV // kvw
    per = SW_TQ // BLK

    def prev(b, i):
        return b * nt * per + jnp.maximum(i * per - 1, 0)

    smem = pl.BlockSpec(memory_space=pltpu.SMEM)
    return pl.pallas_call(
        functools.partial(_sw_kernel, layer=layer),
        grid=(batch, nt),
        in_specs=[
            smem, smem,
            pl.BlockSpec((SW_TQ, SW_QW), lambda b, i: (b * nt + i, cq)),
            pl.BlockSpec((BLK, kvw), lambda b, i: (prev(b, i), ck)),
            pl.BlockSpec((SW_TQ, kvw), lambda b, i: (b * nt + i, ck)),
            pl.BlockSpec((BLK, kvw), lambda b, i: (prev(b, i), cv)),
            pl.BlockSpec((SW_TQ, kvw), lambda b, i: (b * nt + i, cv)),
        ],
        out_specs=pl.BlockSpec((SW_TQ, SW_QW), lambda b, i: (b * nt + i, 0)),
        out_shape=jax.ShapeDtypeStruct((batch * seq, SW_QW), BF16),
        compiler_params=pltpu.CompilerParams(
            dimension_semantics=("arbitrary", "arbitrary"), vmem_limit_bytes=VMEM_LIMIT),
        name="sw_attention",
    )(sinks, slopes, pbf, pbf, pbf, pbf, pbf)


def _split3(x):
    x1 = x.astype(BF16)
    r1 = x - x1.astype(F32)
    x2 = r1.astype(BF16)
    x3 = (r1 - x2.astype(F32)).astype(BF16)
    return x1, x2, x3


def _gdn_kernel(q_ref, k_ref, v_ref, z_ref, gb_ref, norm_ref, o_ref, state_ref, *, rows):
    n = GDN_CHUNK
    nchunks = rows // n

    @pl.when(pl.program_id(1) == 0)
    def _():
        state_ref[...] = jnp.zeros_like(state_ref)

    r = _iota((n, n), 0)
    c = _iota((n, n), 1)
    causal = r >= c
    strict = r > c
    lower_ones = jnp.where(causal, 1.0, 0.0).astype(BF16)

    def sub_blocks(ls):
        return ((r >> (ls + 1)) == (c >> (ls + 1))) & ((r >> ls) != (c >> ls))

    probs = []
    for ci in range(nchunks):
        rs = slice(ci * n, (ci + 1) * n)
        gbeta = gb_ref[rs, :]
        gc_all = sum(_dot(lower_ones, part) for part in _split3(gbeta))
        gc_t = gc_all.T
        for h in range(GDN_HEADS):
            hs = slice(h * GDN_HEAD_DIM, (h + 1) * GDN_HEAD_DIM)
            q, k, v = q_ref[rs, hs], k_ref[rs, hs], v_ref[rs, hs]
            gc = jnp.broadcast_to(gc_all[:, h:h + 1], (n, n))
            gc_row = jnp.broadcast_to(gc_t[h:h + 1, :], (n, n))
            g_tot = jnp.broadcast_to(gc[n - 1:n, :], (n, n))
            beta_b = jnp.broadcast_to(gbeta[:, GDN_HEADS + h:GDN_HEADS + h + 1], (n, n))
            decay = jnp.exp(gc - gc_row)
            k_beta = k * beta_b
            k16 = k.astype(BF16)
            e_gc = jnp.exp(gc)
            probs.append(dict(
                rs=rs, hs=hs, h=h,
                lower=jnp.where(strict, _dot_nt(k_beta.astype(BF16), k16) * decay, 0.0),
                a=jnp.where(causal, _dot_nt(q.astype(BF16), k16) * decay, 0.0).astype(BF16),
                qd=(q * e_gc).astype(BF16),
                kdt=(k * jnp.exp(g_tot - gc)).T.astype(BF16),
                gl=jnp.exp(g_tot),
                rhs=jnp.concatenate([v * beta_b, k_beta * e_gc], axis=1)))
    eye = jnp.where(r == c, 1.0, 0.0).astype(F32)
    zero16 = jnp.zeros((n, n), BF16)
    lower16 = [p["lower"].astype(BF16) for p in probs]
    xs = [eye - jnp.where((r >> 1) == (c >> 1), p["lower"], 0.0) for p in probs]
    for ls in range(1, n.bit_length() - 1):
        x16 = [x.astype(BF16) for x in xs]
        ys = [_dot(xb, jnp.where(sub_blocks(ls), l16, zero16)).astype(BF16) for xb, l16 in zip(x16, lower16)]
        xs = [x - _dot(y, xb) for x, y, xb in zip(xs, ys, x16)]
    for p, x in zip(probs, xs):
        uw = _dot(x.astype(BF16), p["rhs"].astype(BF16))
        p["u"] = uw[:, :n]
        p["wq"] = jnp.concatenate([uw[:, n:].astype(BF16), p["qd"]], axis=0)
        p["ak"] = jnp.concatenate([p["a"], p["kdt"]], axis=0)

    gain = norm_ref[...]
    states = [state_ref[h] for h in range(GDN_HEADS)]
    for p in probs:
        h = p["h"]
        sr = _dot(p["wq"], states[h].astype(BF16))
        vn16 = (p["u"] - sr[:n]).astype(BF16)
        vr = _dot(p["ak"], vn16)
        o = sr[n:] + vr[:n]
        states[h] = states[h] * p["gl"] + vr[n:]
        o_ref[p["rs"], p["hs"]] = (_rms(o, gain) * z_ref[p["rs"], p["hs"]]).astype(o_ref.dtype)
    for h in range(GDN_HEADS):
        state_ref[h] = states[h]


def _gdn(pfp, pgb, norm, layer, batch, seq, rows):
    steps = seq // rows
    w = GDN_W
    cq = FP_CQKV // w

    def blk(col):
        return pl.BlockSpec((rows, w), lambda b, i: (b * steps + i, col))

    return pl.pallas_call(
        functools.partial(_gdn_kernel, rows=rows),
        grid=(batch, steps),
        in_specs=[
            blk(cq), blk(cq + 1), blk(cq + 2), blk(FP_CZ // w),
            pl.BlockSpec((rows, LANES), lambda b, i: (b * steps + i, 0)),
            pl.BlockSpec((None, 1, GDN_HEAD_DIM), lambda b, i: (layer, 0, 0)),
        ],
        out_specs=pl.BlockSpec((rows, w), lambda b, i: (b * steps + i, 0)),
        out_shape=jax.ShapeDtypeStruct((batch * seq, w), BF16),
        scratch_shapes=[pltpu.VMEM((GDN_HEADS, GDN_HEAD_DIM, GDN_HEAD_DIM), F32)],
        compiler_params=pltpu.CompilerParams(
            dimension_semantics=("arbitrary", "arbitrary"), vmem_limit_bytes=VMEM_LIMIT),
        name="gdn",
    )(pfp, pfp, pfp, pfp, pgb, norm)


def _merge_kernel(ya_ref, yb_ref, yc_ref, ga_ref, gb_ref, gc_ref, x_ref,
                  wa_ref, wb_ref, wc_ref, wo_ref, gain_ref, o_ref, m_ref):
    ya, yb, yc = ya_ref[...], yb_ref[...], yc_ref[...]
    for c0 in range(0, D_MODEL, MERGE_CK):
        cs = slice(c0, c0 + MERGE_CK)
        m_ref[:, cs] = (jax.nn.sigmoid(ga_ref[:, cs].astype(F32)) * _dot(ya, wa_ref[:, cs])
                        + jax.nn.sigmoid(gb_ref[:, cs].astype(F32)) * _dot(yb, wb_ref[:, cs])
                        + jax.nn.sigmoid(gc_ref[:, cs].astype(F32)) * _dot(yc, wc_ref[:, cs])).astype(BF16)
    o_ref[...] = x_ref[...] + _rms(_dot(m_ref[...], wo_ref[...]), gain_ref[...])


def _merge(ya, yb, yc, pbf, x, wa, wb, wc, wo, gain, layer, tm):
    t, d = x.shape

    def rows(width, col=0):
        return pl.BlockSpec((tm, width), lambda i: (i, col))

    def whole(a):
        return pl.BlockSpec((None,) + a.shape[1:], lambda i: (layer, 0, 0))

    return pl.pallas_call(
        _merge_kernel,
        grid=(t // tm,),
        in_specs=[rows(SB_W), rows(SW_QW), rows(GDN_W),
                  rows(d, BF_GA // d), rows(d, BF_GB // d), rows(d, BF_GC // d), rows(d),
                  whole(wa), whole(wb), whole(wc), whole(wo), whole(gain)],
        out_specs=rows(d),
        out_shape=jax.ShapeDtypeStruct((t, d), F32),
        scratch_shapes=[pltpu.VMEM((tm, d), BF16)],
        compiler_params=pltpu.CompilerParams(
            dimension_semantics=("arbitrary",), vmem_limit_bytes=VMEM_LIMIT),
        name="merge",
    )(ya, yb, yc, pbf, pbf, pbf, x, wa, wb, wc, wo, gain)


SQRT_2_OVER_PI = 0.7978845608028654


def _gelu_tanh(x):
    inner = x * (SQRT_2_OVER_PI + (SQRT_2_OVER_PI * 0.044715) * (x * x))
    return (0.5 * x) * (1.0 + jnp.tanh(inner))


def _ffn_kernel(x_ref, gpre_ref, wup_ref, conv_ref, wdn_ref, gpost_ref, o_ref, carry_ref, f_ref, *, tm):
    @pl.when(pl.program_id(1) == 0)
    def _():
        carry_ref[...] = jnp.zeros_like(carry_ref)

    x = x_ref[...]
    hn = _rms(x, gpre_ref[...]).astype(BF16)

    def conv(col, width):
        cs = slice(col, col + width)
        hid = _dot(hn, wup_ref[:, cs])
        prev8 = carry_ref[:, cs]
        w = conv_ref[:, cs]
        y = w[FFN_CONV - 1:FFN_CONV, :] * hid
        for s in range(1, FFN_CONV):
            y = y + w[FFN_CONV - 1 - s:FFN_CONV - s, :] * _shift_rows(hid, prev8, s)
        carry_ref[:, cs] = hid[tm - 8:tm, :]
        return y

    col = 0
    while col < D_FF:
        width = min(FFN_CK, D_FF - col)
        f_gate = conv(col, width)
        f_up = conv(D_FF + col, width)
        f_ref[:, col:col + width] = (_gelu_tanh(f_gate) * f_up).astype(BF16)
        col += width
    o_ref[...] = x + _rms(_dot(f_ref[...], wdn_ref[...]), gpost_ref[...])


def _ffn(x, gpre, wup, conv_w, wdn, gpost, layer, batch, seq, tm):
    steps = seq // tm
    d = x.shape[1]

    def whole(a):
        return pl.BlockSpec((None,) + a.shape[1:], lambda b, i: (layer, 0, 0), pipeline_mode=pl.Buffered(1))

    return pl.pallas_call(
        functools.partial(_ffn_kernel, tm=tm),
        grid=(batch, steps),
        in_specs=[pl.BlockSpec((tm, d), lambda b, i: (b * steps + i, 0)),
                  whole(gpre), whole(wup), whole(conv_w), whole(wdn), whole(gpost)],
        out_specs=pl.BlockSpec((tm, d), lambda b, i: (b * steps + i, 0)),
        out_shape=jax.ShapeDtypeStruct(x.shape, F32),
        scratch_shapes=[pltpu.VMEM((8, 2 * D_FF), F32), pltpu.VMEM((tm, D_FF), BF16)],
        compiler_params=pltpu.CompilerParams(
            dimension_semantics=("arbitrary", "arbitrary"), vmem_limit_bytes=VMEM_LIMIT),
        name="ffn",
    )(x, gpre, wup, conv_w, wdn, gpost)


def _pack_w_in(w):
    w = jnp.swapaxes(w, 1, 2).astype(BF16)
    o = 0
    parts = {}
    for name, width in (("aq", SB_W), ("ak", SB_W), ("av", SB_W), ("bq", SW_QW), ("bk", SW_KVW),
                        ("bv", SW_KVW), ("cqkv", 3 * GDN_W), ("cz", GDN_W), ("ca", GDN_HEADS),
                        ("cb", GDN_HEADS), ("ga", D_MODEL), ("gb", D_MODEL), ("gc", D_MODEL)):
        parts[name] = w[:, o:o + width, :]
        o += width

    def dup(t):
        return jnp.concatenate([t[:, h * HALF:(h + 1) * HALF, :] for h in range(SW_KV_HEADS) for _ in range(2)], axis=1)

    cols = [parts["ga"], parts["gb"], parts["gc"],
            parts["aq"] * (SB_HEAD_DIM ** -0.5), parts["ak"], parts["av"],
            parts["bq"] * (SW_HEAD_DIM ** -0.5), dup(parts["bk"]), dup(parts["bv"]),
            parts["cqkv"], parts["cz"]]
    pad = jnp.zeros((w.shape[0], LANES - 2 * GDN_HEADS, w.shape[2]), w.dtype)
    wab = jnp.concatenate([parts["ca"], parts["cb"], pad], axis=1)
    return jnp.concatenate(cols, axis=1).astype(BF16), wab


def _layer(x, batch, seq, layer, p):
    pbf, pfp, pgb = _inproj(x, p["ln_mix_pre"], p["w_in"], p["w_ab"], p["gdn_conv"], p["gdn_small"],
                            layer, seq, tm=INPROJ_TM)
    ya = _sb_attention(pbf, batch, seq)
    yb = _sw_attention(pbf, p["sinks"], p["slopes"], layer, batch, seq)
    yc = _gdn(pfp, pgb, p["gdn_norm"], layer, batch, seq, rows=GDN_ROWS)
    x = _merge(ya, yb, yc, pbf, x, p["wa"], p["wb"], p["wc"], p["wo"], p["ln_mix_post"], layer, tm=MERGE_TM)
    return _ffn(x, p["ln_ffn_pre"], p["w_up"], p["ffn_conv"], p["w_down"], p["ln_ffn_post"],
                layer, batch, seq, tm=FFN_TM)


def kernel(x, ln_mix_pre, w_in, sw_sinks, gdn_conv, gdn_a_log, gdn_dt_bias, gdn_norm, w_branch_a,
           w_branch_b, w_branch_c, w_out, ln_mix_post, ln_ffn_pre, w_up, ffn_conv, w_down, ln_ffn_post):
    batch, seq, d = x.shape
    depth = w_in.shape[0]
    assert d == D_MODEL and w_up.shape[-1] == 2 * D_FF, (x.shape, w_up.shape)
    assert all(seq % rows == 0 for rows in (INPROJ_TM, SB_TQ, SW_TQ, GDN_ROWS, MERGE_TM, FFN_TM)), seq
    small = jnp.zeros((depth, 8, LANES), F32)
    small = small.at[:, 0, :GDN_HEADS].set(gdn_a_log).at[:, 1, :GDN_HEADS].set(gdn_dt_bias)
    w_main, w_ab = _pack_w_in(w_in)
    p = {
        "ln_mix_pre": ln_mix_pre[:, None, :], "w_in": w_main, "w_ab": w_ab,
        "sinks": sw_sinks, "slopes": jnp.exp2(-8.0 * jnp.arange(1, SW_Q_HEADS + 1, dtype=F32) / SW_Q_HEADS),
        "gdn_conv": gdn_conv, "gdn_small": small, "gdn_norm": gdn_norm[:, None, :],
        "wa": w_branch_a.astype(BF16), "wb": w_branch_b.astype(BF16),
        "wc": w_branch_c.astype(BF16), "wo": w_out.astype(BF16),
        "ln_mix_post": ln_mix_post[:, None, :], "ln_ffn_pre": ln_ffn_pre[:, None, :],
        "w_up": w_up.astype(BF16), "ffn_conv": ffn_conv, "w_down": w_down.astype(BF16),
        "ln_ffn_post": ln_ffn_post[:, None, :],
    }
    h = x.reshape(batch * seq, d)
    for layer in range(depth):
        h = _layer(h, batch, seq, layer, p)
    return h.reshape(batch, seq, d)
```

```python
import functools

import jax
import jax.numpy as jnp
from jax import lax
from jax.experimental import pallas as pl
from jax.experimental.pallas import tpu as pltpu

F32 = jnp.float32
BF16 = jnp.bfloat16
NORM_EPS = 1e-6

D_MODEL = 1024
SB_HEADS, SB_HEAD_DIM = 8, 64
SW_Q_HEADS, SW_KV_HEADS, SW_HEAD_DIM = 8, 2, 64
GDN_HEADS, GDN_HEAD_DIM, GDN_CONV = 4, 128, 4
D_FF, FFN_CONV = 2816, 3
SB_W = SB_HEADS * SB_HEAD_DIM
SW_QW = SW_Q_HEADS * SW_HEAD_DIM
SW_KVW = SW_KV_HEADS * SW_HEAD_DIM
GDN_W = GDN_HEADS * GDN_HEAD_DIM

BLK = 128
LANES = 128
HALF = 64
V7X_VMEM_BYTES = 64 * 1024 * 1024
VMEM_LIMIT = V7X_VMEM_BYTES - 8 * 1024 * 1024

BF_GA, BF_GB, BF_GC, BF_SBQ, BF_SBK, BF_SBV, BF_SWQ, BF_SWK, BF_SWV, BF_COLS = (
    0, 1024, 2048, 3072, 3584, 4096, 4608, 5120, 5376, 5632)
FP_CQKV, FP_CZ, FP_COLS = 0, 1536, 2048

INPROJ_TM, PROJ_TN = 512, 512
SB_TQ, SB_PAIRS = 4 * BLK, 4
SW_TQ = 8 * BLK
GDN_ROWS, GDN_CHUNK = 512, 128
MERGE_TM, MERGE_CK = 1024, 256
FFN_TM, FFN_CK = 1024, 768
LOG2E = 1.4426950408889634
F32_EXP2_ZERO = -104.0 * LOG2E


def _dot(a, b):
    return jnp.dot(a, b, preferred_element_type=F32)


def _dot_nt(a, b):
    return lax.dot_general(a, b, (((1,), (1,)), ((), ())), preferred_element_type=F32)


def _iota(shape, dim):
    return lax.broadcasted_iota(jnp.int32, shape, dim)


def _rms(t, gain):
    return t * lax.rsqrt(jnp.mean(t * t, axis=-1, keepdims=True) + NORM_EPS) * gain


def _shift_rows(h, prev8, s):
    r = pltpu.roll(h, s, axis=0)
    row = _iota(h.shape, 0)
    for t in range(s):
        r = jnp.where(row == t, prev8[8 - s + t:8 - s + t + 1, :], r)
    return r


def _inproj_kernel(x_ref, g_ref, w_ref, wab_ref, conv_ref, small_ref, obf_ref, of_ref, oab_ref, carry_ref,
                   *, tm, tiles_per_seq):
    @pl.when(pl.program_id(0) % tiles_per_seq == 0)
    def _():
        carry_ref[...] = jnp.zeros_like(carry_ref)

    xn = _rms(x_ref[...], g_ref[...]).astype(BF16)
    n = GDN_CHUNK
    ab = _dot_nt(xn, wab_ref[...])
    zab = ab + small_ref[1:2, :]
    softplus = jnp.maximum(zab, 0.0) + jnp.log(1.0 + jnp.exp(-jnp.abs(zab)))
    oab_ref[...] = jnp.where(_iota(ab.shape, 1) < GDN_HEADS,
                             -jnp.exp(small_ref[0:1, :]) * softplus, jax.nn.sigmoid(ab))
    for idx in range(3):
        cols = slice(FP_CQKV + idx * GDN_W, FP_CQKV + (idx + 1) * GDN_W)
        cw = slice(idx * GDN_W, (idx + 1) * GDN_W)
        raw = _dot_nt(xn, w_ref[BF_COLS + cols.start:BF_COLS + cols.stop, :])
        w = conv_ref[:, cw]
        for ci in range(tm // n):
            x = raw[ci * n:(ci + 1) * n, :]
            prev8 = carry_ref[:, cw] if ci == 0 else raw[ci * n - 8:ci * n, :]
            y = w[GDN_CONV - 1:GDN_CONV, :] * x
            for s in range(1, GDN_CONV):
                y = y + w[GDN_CONV - 1 - s:GDN_CONV - s, :] * _shift_rows(x, prev8, s)
            y = y * jax.nn.sigmoid(y)
            for h in range(GDN_HEADS):
                yh = y[:, h * GDN_HEAD_DIM:(h + 1) * GDN_HEAD_DIM]
                if idx < 2:
                    inv = lax.rsqrt(jnp.sum(yh * yh, axis=-1, keepdims=True) + NORM_EPS)
                    yh = yh * (inv * (GDN_HEAD_DIM ** -0.5) if idx == 0 else inv)
                of_ref[ci * n:(ci + 1) * n, cols.start + h * GDN_HEAD_DIM:cols.start + (h + 1) * GDN_HEAD_DIM] = yh
        carry_ref[:, cw] = raw[tm - 8:tm, :]
    z = _dot_nt(xn, w_ref[BF_COLS + FP_CZ:BF_COLS + FP_CZ + GDN_W, :])
    of_ref[:, FP_CZ:FP_CZ + GDN_W] = z * jax.nn.sigmoid(z)
    for c0 in range(0, BF_COLS, PROJ_TN):
        obf_ref[:, c0:c0 + PROJ_TN] = _dot_nt(xn, w_ref[c0:c0 + PROJ_TN, :]).astype(BF16)


def _inproj(x, gain, w, wab, conv_w, small, layer, seq, tm):
    t, d = x.shape

    def whole(a):
        return pl.BlockSpec((None,) + a.shape[1:], lambda i: (layer, 0, 0), pipeline_mode=pl.Buffered(1))

    return pl.pallas_call(
        functools.partial(_inproj_kernel, tm=tm, tiles_per_seq=seq // tm),
        grid=(t // tm,),
        in_specs=[pl.BlockSpec((tm, d), lambda i: (i, 0)), whole(gain), whole(w), whole(wab),
                  whole(conv_w), whole(small)],
        out_specs=[
            pl.BlockSpec((tm, BF_COLS), lambda i: (i, 0)),
            pl.BlockSpec((tm, FP_COLS), lambda i: (i, 0)),
            pl.BlockSpec((tm, LANES), lambda i: (i, 0)),
        ],
        out_shape=[jax.ShapeDtypeStruct((t, BF_COLS), BF16), jax.ShapeDtypeStruct((t, FP_COLS), F32),
                   jax.ShapeDtypeStruct((t, LANES), F32)],
        scratch_shapes=[pltpu.VMEM((8, 3 * GDN_W), F32)],
        compiler_params=pltpu.CompilerParams(
            dimension_semantics=("arbitrary",), vmem_limit_bytes=VMEM_LIMIT),
        name="inproj",
    )(x, gain, w, wab, conv_w, small)


def _sb_kernel(q_ref, k_ref, v_ref, o_ref, acc_ref, lr_ref, live_ref):
    ti = pl.program_id(2)
    lane = _iota((BLK, LANES), 1)
    row = _iota((BLK, LANES), 0)
    first = lane < HALF
    tri = lane < row
    full = lane >= 0
    zero_tile = jnp.zeros((BLK, LANES), F32)
    r2 = _iota((BLK, 2 * LANES), 0)
    c2 = _iota((BLK, 2 * LANES), 1)
    suffix_ones = jnp.where((r2 > c2) | (c2 >= LANES), 1.0, 0.0).astype(BF16)
    suffix_ones = jnp.concatenate([suffix_ones, suffix_ones], axis=0)

    def split_heads(q):
        sel = jnp.concatenate([first] * (q.shape[0] // BLK), axis=0)
        zero = jnp.zeros_like(q)
        return jnp.where(sel, q, zero), jnp.where(sel, zero, q)

    def log_parts(z):
        zs = z * LOG2E
        log_stay = -(jnp.maximum(zs, 0.0) + jnp.log2(1.0 + jnp.exp2(-jnp.abs(zs))))
        return log_stay, log_stay + zs

    def suffix_sums(parts):
        hi = [x.astype(BF16) for x in parts]
        lo = [(x - h.astype(F32)).astype(BF16) for x, h in zip(parts, hi)]
        hilo = jnp.concatenate([jnp.concatenate(hi, axis=0), jnp.concatenate(lo, axis=0)], axis=1)
        sums = _dot(hilo, suffix_ones)
        out, o = [], 0
        for x in parts:
            out.append((sums[o:o + x.shape[0], :LANES], sums[o:o + x.shape[0], LANES:]))
            o += x.shape[0]
        return out

    def generic_block(pp, s, j, valid):
        ps = slice(pp * LANES, (pp + 1) * LANES)
        q0, q1 = split_heads(q_ref[s * BLK:(s + 1) * BLK, ps])
        off = pl.multiple_of(j * BLK, BLK)
        k, v = k_ref[pl.ds(off, BLK), ps], v_ref[pl.ds(off, BLK), ps]
        log_stay, log_take = log_parts(_dot_nt(jnp.concatenate([q0, q1], axis=0), k))
        if valid is not None:
            vm = jnp.concatenate([valid, valid], axis=0)
            log_stay = jnp.where(vm, log_stay, 0.0)
        (later, total), = suffix_sums([log_stay])
        w = jnp.exp2(log_take + later + lr_ref[pp, s])
        if valid is not None:
            w = jnp.where(vm, w, 0.0)
        pv = _dot(w.astype(BF16), v)
        acc_ref[pp, s] += jnp.where(first, pv[:BLK], pv[BLK:])
        lr_ref[pp, s] += total

    def tail(pp, s, j0):
        def cond(c):
            j, live = c
            return jnp.logical_and(j >= 0, live)

        def body(c):
            j, _ = c
            generic_block(pp, s, j, None)
            return j - 1, jnp.max(lr_ref[pp, s]) > F32_EXP2_ZERO

        lax.while_loop(cond, body, (j0, jnp.max(lr_ref[pp, s]) > F32_EXP2_ZERO))

    ns = SB_TQ // BLK

    def band(off):
        base = pl.multiple_of((ns * ti - off) * BLK, BLK)
        P = range(SB_PAIRS)
        order = range(ns + off - 1, -1, -1)
        ps = [slice(pp * LANES, (pp + 1) * LANES) for pp in P]
        qs = [split_heads(q_ref[:, ps[pp]]) for pp in P]
        kw = [k_ref[pl.ds(base, (ns + off) * BLK), ps[pp]] for pp in P]
        vw = [v_ref[pl.ds(base, (ns + off) * BLK), ps[pp]] for pp in P]

        def subs(t):
            return list(range(max(t - off, 0), min(t - off + 2, ns - 1) + 1))

        def lhs(q, t):
            rs = slice(subs(t)[0] * BLK, (subs(t)[-1] + 1) * BLK)
            return jnp.concatenate([q[0][rs], q[1][rs]], axis=0)

        def mask(t):
            if t - off not in subs(t):
                return None
            return jnp.concatenate([tri if s == t - off else full for s in subs(t)] * 2, axis=0)

        z = {t: [_dot_nt(lhs(qs[pp], t), kw[pp][t * BLK:(t + 1) * BLK]) for pp in P] for t in order}
        lp, sums = {}, {}
        for t in order:
            lp[t] = [log_parts(zz) for zz in z[t]]
            m = mask(t)
            sums[t] = [suffix_sums([ls if m is None else jnp.where(m, ls, 0.0)])[0] for ls, _ in lp[t]]
        least = None
        for pp in P:
            fin = [[None, None] for _ in range(ns)]
            for t in order:
                ss = subs(t)
                total = sums[t][pp][1]
                for h in range(2):
                    for j, s in enumerate(ss):
                        rs = slice((h * len(ss) + j) * BLK, (h * len(ss) + j + 1) * BLK)
                        fin[s][h] = total[rs] if fin[s][h] is None else fin[s][h] + total[rs]
            for s in range(ns):
                for h in range(2):
                    least = fin[s][h] if least is None else jnp.maximum(least, fin[s][h])
        live_ref[0] = (jnp.max(least) > F32_EXP2_ZERO).astype(jnp.int32)
        for pp in P:
            lr = [[None, None] for _ in range(ns)]
            acc = [[None, None] for _ in range(ns)]
            for t in order:
                ss = subs(t)
                later, total = sums[t][pp]
                e = lp[t][pp][1] + later
                if any(lr[s][h] is not None for s in ss for h in range(2)):
                    e = e + jnp.concatenate([zero_tile if lr[s][h] is None else lr[s][h]
                                             for h in range(2) for s in ss], axis=0)
                w = jnp.exp2(e)
                if mask(t) is not None:
                    w = jnp.where(mask(t), w, 0.0)
                pv = _dot(w.astype(BF16), vw[pp][t * BLK:(t + 1) * BLK])
                for h in range(2):
                    for j, s in enumerate(ss):
                        rs = slice((h * len(ss) + j) * BLK, (h * len(ss) + j + 1) * BLK)
                        lr[s][h] = total[rs] if lr[s][h] is None else lr[s][h] + total[rs]
                        acc[s][h] = pv[rs] if acc[s][h] is None else acc[s][h] + pv[rs]
            for s in range(ns):
                acc_ref[pp, s] = jnp.where(first, acc[s][0], acc[s][1])
                lr_ref[pp, s] = jnp.concatenate(lr[s], axis=0)

    @pl.when(ti == 0)
    def _():
        band(0)

    @pl.when(ti > 0)
    def _():
        band(2)

    @pl.when(live_ref[0] != 0)
    def _():
        for pp in range(SB_PAIRS):
            for s in range(ns):
                tail(pp, s, ns * ti + s - 3)

    for pp in range(SB_PAIRS):
        for s in range(ns):
            o_ref[s * BLK:(s + 1) * BLK, pp * LANES:(pp + 1) * LANES] = acc_ref[pp, s].astype(o_ref.dtype)


def _sb_attention(pbf, batch, seq):
    nt = seq // SB_TQ
    w = SB_PAIRS * LANES
    cq, ck, cv = BF_SBQ // w, BF_SBK // w, BF_SBV // w
    return pl.pallas_call(
        _sb_kernel,
        grid=(batch, SB_W // w, nt),
        in_specs=[
            pl.BlockSpec((SB_TQ, w), lambda b, p, i: (b * nt + i, cq + p)),
            pl.BlockSpec((seq, w), lambda b, p, i: (b, ck + p), pipeline_mode=pl.Buffered(1)),
            pl.BlockSpec((seq, w), lambda b, p, i: (b, cv + p), pipeline_mode=pl.Buffered(1)),
        ],
        out_specs=pl.BlockSpec((SB_TQ, w), lambda b, p, i: (b * nt + i, p)),
        out_shape=jax.ShapeDtypeStruct((batch * seq, SB_W), BF16),
        scratch_shapes=[pltpu.VMEM((SB_PAIRS, SB_TQ // BLK, BLK, LANES), F32),
                        pltpu.VMEM((SB_PAIRS, SB_TQ // BLK, 2 * BLK, LANES), F32),
                        pltpu.SMEM((1,), jnp.int32)],
        compiler_params=pltpu.CompilerParams(
            dimension_semantics=("arbitrary", "arbitrary", "arbitrary"), vmem_limit_bytes=VMEM_LIMIT),
        name="sb_attention",
    )(pbf, pbf, pbf)


def _sw_kernel(sinks_ref, slopes_ref, q_ref, kp_ref, kc_ref, vp_ref, vc_ref, o_ref, *, layer):
    ti = pl.program_id(1)
    lane = _iota((BLK, LANES), 1)
    row = _iota((BLK, LANES), 0)
    first = lane < HALF
    cur = lane <= row
    dist = jnp.where(cur, row - lane, row - lane + BLK).astype(F32)
    group = SW_Q_HEADS // SW_KV_HEADS
    units = []
    for s in range(SW_TQ // BLK):
        rs = slice(s * BLK, (s + 1) * BLK)
        for hk in range(SW_KV_HEADS):
            ks = slice(hk * LANES, (hk + 1) * LANES)
            kc, vc = kc_ref[rs, ks], vc_ref[rs, ks]
            if s == 0:
                kp, vp = kp_ref[:, ks], vp_ref[:, ks]
            else:
                kp, vp = kc_ref[(s - 1) * BLK:s * BLK, ks], vc_ref[(s - 1) * BLK:s * BLK, ks]
            pairs = range(hk * group // 2, (hk + 1) * group // 2)
            lhs = []
            for p in pairs:
                qp = q_ref[rs, p * LANES:(p + 1) * LANES]
                zero = jnp.zeros_like(qp)
                lhs += [jnp.where(first, qp, zero), jnp.where(first, zero, qp)]
            lhs = jnp.concatenate(lhs, axis=0)
            units.append(dict(s=s, rs=rs, hk=hk, pairs=pairs, vc=vc, vp=vp,
                              s_cur=_dot_nt(lhs, kc), s_prev=_dot_nt(lhs, kp)))
    for u in units:
        scores = []
        for g in range(group):
            head = u["hk"] * group + g
            gs = slice(g * BLK, (g + 1) * BLK)
            sc = jnp.where(cur, u["s_cur"][gs], u["s_prev"][gs]) - slopes_ref[head] * dist
            if u["s"] == 0:
                sc = jnp.where(jnp.logical_or(cur, ti > 0), sc, -jnp.inf)
            scores.append(sc)
        u["scores"] = scores
        u["m"] = [jnp.maximum(jnp.max(sc, axis=-1, keepdims=True), sinks_ref[layer, u["hk"] * group + g])
                  for g, sc in enumerate(scores)]
    for u in units:
        u["e"] = [jnp.exp(sc - m) for sc, m in zip(u["scores"], u["m"])]
        u["denom"] = [jnp.sum(e, axis=-1, keepdims=True) + jnp.exp(sinks_ref[layer, u["hk"] * group + g] - m)
                      for g, (e, m) in enumerate(zip(u["e"], u["m"]))]
    for u in units:
        prob = [(e / d).astype(BF16) for e, d in zip(u["e"], u["denom"])]
        pz = jnp.zeros_like(prob[0])
        prob_c = jnp.concatenate([jnp.where(cur, p, pz) for p in prob], axis=0)
        prob_p = jnp.concatenate([jnp.where(cur, pz, p) for p in prob], axis=0)
        pv = _dot(prob_c, u["vc"]) + _dot(prob_p, u["vp"])
        for t, p in enumerate(u["pairs"]):
            o_ref[u["rs"], p * LANES:(p + 1) * LANES] = jnp.where(
                first, pv[2 * t * BLK:(2 * t + 1) * BLK], pv[(2 * t + 1) * BLK:(2 * t + 2) * BLK]
            ).astype(o_ref.dtype)


def _sw_attention(pbf, sinks, slopes, layer, batch, seq):
    nt = seq // SW_TQ
    kvw = 2 * SW_KVW
    cq, ck, cv = BF_SWQ // SW_QW, BF_SWK // kvw, BF_SWV // kvw
    per = SW_TQ // BLK

    def prev(b, i):
        return b * nt * per + jnp.maximum(i * per - 1, 0)

    smem = pl.BlockSpec(memory_space=pltpu.SMEM)
    return pl.pallas_call(
        functools.partial(_sw_kernel, layer=layer),
        grid=(batch, nt),
        in_specs=[
            smem, smem,
            pl.BlockSpec((SW_TQ, SW_QW), lambda b, i: (b * nt + i, cq)),
            pl.BlockSpec((BLK, kvw), lambda b, i: (prev(b, i), ck)),
            pl.BlockSpec((SW_TQ, kvw), lambda b, i: (b * nt + i, ck)),
            pl.BlockSpec((BLK, kvw), lambda b, i: (prev(b, i), cv)),
            pl.BlockSpec((SW_TQ, kvw), lambda b, i: (b * nt + i, cv)),
        ],
        out_specs=pl.BlockSpec((SW_TQ, SW_QW), lambda b, i: (b * nt + i, 0)),
        out_shape=jax.ShapeDtypeStruct((batch * seq, SW_QW), BF16),
        compiler_params=pltpu.CompilerParams(
            dimension_semantics=("arbitrary", "arbitrary"), vmem_limit_bytes=VMEM_LIMIT),
        name="sw_attention",
    )(sinks, slopes, pbf, pbf, pbf, pbf, pbf)


def _split3(x):
    x1 = x.astype(BF16)
    r1 = x - x1.astype(F32)
    x2 = r1.astype(BF16)
    x3 = (r1 - x2.astype(F32)).astype(BF16)
    return x1, x2, x3


def _gdn_kernel(q_ref, k_ref, v_ref, z_ref, gb_ref, norm_ref, o_ref, state_ref, *, rows):
    n = GDN_CHUNK
    nchunks = rows // n

    @pl.when(pl.program_id(1) == 0)
    def _():
        state_ref[...] = jnp.zeros_like(state_ref)

    r = _iota((n, n), 0)
    c = _iota((n, n), 1)
    causal = r >= c
    strict = r > c
    lower_ones = jnp.where(causal, 1.0, 0.0).astype(BF16)

    def sub_blocks(ls):
        return ((r >> (ls + 1)) == (c >> (ls + 1))) & ((r >> ls) != (c >> ls))

    probs = []
    for ci in range(nchunks):
        rs = slice(ci * n, (ci + 1) * n)
        gbeta = gb_ref[rs, :]
        gc_all = sum(_dot(lower_ones, part) for part in _split3(gbeta))
        gc_t = gc_all.T
        for h in range(GDN_HEADS):
            hs = slice(h * GDN_HEAD_DIM, (h + 1) * GDN_HEAD_DIM)
            q, k, v = q_ref[rs, hs], k_ref[rs, hs], v_ref[rs, hs]
            gc = jnp.broadcast_to(gc_all[:, h:h + 1], (n, n))
            gc_row = jnp.broadcast_to(gc_t[h:h + 1, :], (n, n))
            g_tot = jnp.broadcast_to(gc[n - 1:n, :], (n, n))
            beta_b = jnp.broadcast_to(gbeta[:, GDN_HEADS + h:GDN_HEADS + h + 1], (n, n))
            decay = jnp.exp(gc - gc_row)
            k_beta = k * beta_b
            k16 = k.astype(BF16)
            e_gc = jnp.exp(gc)
            probs.append(dict(
                rs=rs, hs=hs, h=h,
                lower=jnp.where(strict, _dot_nt(k_beta.astype(BF16), k16) * decay, 0.0),
                a=jnp.where(causal, _dot_nt(q.astype(BF16), k16) * decay, 0.0).astype(BF16),
                qd=(q * e_gc).astype(BF16),
                kdt=(k * jnp.exp(g_tot - gc)).T.astype(BF16),
                gl=jnp.exp(g_tot),
                rhs=jnp.concatenate([v * beta_b, k_beta * e_gc], axis=1)))
    eye = jnp.where(r == c, 1.0, 0.0).astype(F32)
    zero16 = jnp.zeros((n, n), BF16)
    lower16 = [p["lower"].astype(BF16) for p in probs]
    xs = [eye - jnp.where((r >> 1) == (c >> 1), p["lower"], 0.0) for p in probs]
    for ls in range(1, n.bit_length() - 1):
        x16 = [x.astype(BF16) for x in xs]
        ys = [_dot(xb, jnp.where(sub_blocks(ls), l16, zero16)).astype(BF16) for xb, l16 in zip(x16, lower16)]
        xs = [x - _dot(y, xb) for x, y, xb in zip(xs, ys, x16)]
    for p, x in zip(probs, xs):
        uw = _dot(x.astype(BF16), p["rhs"].astype(BF16))
        p["u"] = uw[:, :n]
        p["wq"] = jnp.concatenate([uw[:, n:].astype(BF16), p["qd"]], axis=0)
        p["ak"] = jnp.concatenate([p["a"], p["kdt"]], axis=0)

    gain = norm_ref[...]
    states = [state_ref[h] for h in range(GDN_HEADS)]
    for p in probs:
        h = p["h"]
        sr = _dot(p["wq"], states[h].astype(BF16))
        vn16 = (p["u"] - sr[:n]).astype(BF16)
        vr = _dot(p["ak"], vn16)
        o = sr[n:] + vr[:n]
        states[h] = states[h] * p["gl"] + vr[n:]
        o_ref[p["rs"], p["hs"]] = (_rms(o, gain) * z_ref[p["rs"], p["hs"]]).astype(o_ref.dtype)
    for h in range(GDN_HEADS):
        state_ref[h] = states[h]


def _gdn(pfp, pgb, norm, layer, batch, seq, rows):
    steps = seq // rows
    w = GDN_W
    cq = FP_CQKV // w

    def blk(col):
        return pl.BlockSpec((rows, w), lambda b, i: (b * steps + i, col))

    return pl.pallas_call(
        functools.partial(_gdn_kernel, rows=rows),
        grid=(batch, steps),
        in_specs=[
            blk(cq), blk(cq + 1), blk(cq + 2), blk(FP_CZ // w),
            pl.BlockSpec((rows, LANES), lambda b, i: (b * steps + i, 0)),
            pl.BlockSpec((None, 1, GDN_HEAD_DIM), lambda b, i: (layer, 0, 0)),
        ],
        out_specs=pl.BlockSpec((rows, w), lambda b, i: (b * steps + i, 0)),
        out_shape=jax.ShapeDtypeStruct((batch * seq, w), BF16),
        scratch_shapes=[pltpu.VMEM((GDN_HEADS, GDN_HEAD_DIM, GDN_HEAD_DIM), F32)],
        compiler_params=pltpu.CompilerParams(
            dimension_semantics=("arbitrary", "arbitrary"), vmem_limit_bytes=VMEM_LIMIT),
        name="gdn",
    )(pfp, pfp, pfp, pfp, pgb, norm)


def _merge_kernel(ya_ref, yb_ref, yc_ref, ga_ref, gb_ref, gc_ref, x_ref,
                  wa_ref, wb_ref, wc_ref, wo_ref, gain_ref, o_ref, m_ref):
    ya, yb, yc = ya_ref[...], yb_ref[...], yc_ref[...]
    for c0 in range(0, D_MODEL, MERGE_CK):
        cs = slice(c0, c0 + MERGE_CK)
        m_ref[:, cs] = (jax.nn.sigmoid(ga_ref[:, cs].astype(F32)) * _dot(ya, wa_ref[:, cs])
                        + jax.nn.sigmoid(gb_ref[:, cs].astype(F32)) * _dot(yb, wb_ref[:, cs])
                        + jax.nn.sigmoid(gc_ref[:, cs].astype(F32)) * _dot(yc, wc_ref[:, cs])).astype(BF16)
    o_ref[...] = x_ref[...] + _rms(_dot(m_ref[...], wo_ref[...]), gain_ref[...])


def _merge(ya, yb, yc, pbf, x, wa, wb, wc, wo, gain, layer, tm):
    t, d = x.shape

    def rows(width, col=0):
        return pl.BlockSpec((tm, width), lambda i: (i, col))

    def whole(a):
        return pl.BlockSpec((None,) + a.shape[1:], lambda i: (layer, 0, 0))

    return pl.pallas_call(
        _merge_kernel,
        grid=(t // tm,),
        in_specs=[rows(SB_W), rows(SW_QW), rows(GDN_W),
                  rows(d, BF_GA // d), rows(d, BF_GB // d), rows(d, BF_GC // d), rows(d),
                  whole(wa), whole(wb), whole(wc), whole(wo), whole(gain)],
        out_specs=rows(d),
        out_shape=jax.ShapeDtypeStruct((t, d), F32),
        scratch_shapes=[pltpu.VMEM((tm, d), BF16)],
        compiler_params=pltpu.CompilerParams(
            dimension_semantics=("arbitrary",), vmem_limit_bytes=VMEM_LIMIT),
        name="merge",
    )(ya, yb, yc, pbf, pbf, pbf, x, wa, wb, wc, wo, gain)


SQRT_2_OVER_PI = 0.7978845608028654


def _gelu_tanh(x):
    inner = x * (SQRT_2_OVER_PI + (SQRT_2_OVER_PI * 0.044715) * (x * x))
    return (0.5 * x) * (1.0 + jnp.tanh(inner))


def _ffn_kernel(x_ref, gpre_ref, wup_ref, conv_ref, wdn_ref, gpost_ref, o_ref, carry_ref, f_ref, *, tm):
    @pl.when(pl.program_id(1) == 0)
    def _():
        carry_ref[...] = jnp.zeros_like(carry_ref)

    x = x_ref[...]
    hn = _rms(x, gpre_ref[...]).astype(BF16)

    def conv(col, width):
        cs = slice(col, col + width)
        hid = _dot(hn, wup_ref[:, cs])
        prev8 = carry_ref[:, cs]
        w = conv_ref[:, cs]
        y = w[FFN_CONV - 1:FFN_CONV, :] * hid
        for s in range(1, FFN_CONV):
            y = y + w[FFN_CONV - 1 - s:FFN_CONV - s, :] * _shift_rows(hid, prev8, s)
        carry_ref[:, cs] = hid[tm - 8:tm, :]
        return y

    col = 0
    while col < D_FF:
        width = min(FFN_CK, D_FF - col)
        f_gate = conv(col, width)
        f_up = conv(D_FF + col, width)
        f_ref[:, col:col + width] = (_gelu_tanh(f_gate) * f_up).astype(BF16)
        col += width
    o_ref[...] = x + _rms(_dot(f_ref[...], wdn_ref[...]), gpost_ref[...])


def _ffn(x, gpre, wup, conv_w, wdn, gpost, layer, batch, seq, tm):
    steps = seq // tm
    d = x.shape[1]

    def whole(a):
        return pl.BlockSpec((None,) + a.shape[1:], lambda b, i: (layer, 0, 0), pipeline_mode=pl.Buffered(1))

    return pl.pallas_call(
        functools.partial(_ffn_kernel, tm=tm),
        grid=(batch, steps),
        in_specs=[pl.BlockSpec((tm, d), lambda b, i: (b * steps + i, 0)),
                  whole(gpre), whole(wup), whole(conv_w), whole(wdn), whole(gpost)],
        out_specs=pl.BlockSpec((tm, d), lambda b, i: (b * steps + i, 0)),
        out_shape=jax.ShapeDtypeStruct(x.shape, F32),
        scratch_shapes=[pltpu.VMEM((8, 2 * D_FF), F32), pltpu.VMEM((tm, D_FF), BF16)],
        compiler_params=pltpu.CompilerParams(
            dimension_semantics=("arbitrary", "arbitrary"), vmem_limit_bytes=VMEM_LIMIT),
        name="ffn",
    )(x, gpre, wup, conv_w, wdn, gpost)


def _pack_w_in(w):
    w = jnp.swapaxes(w, 1, 2).astype(BF16)
    o = 0
    parts = {}
    for name, width in (("aq", SB_W), ("ak", SB_W), ("av", SB_W), ("bq", SW_QW), ("bk", SW_KVW),
                        ("bv", SW_KVW), ("cqkv", 3 * GDN_W), ("cz", GDN_W), ("ca", GDN_HEADS),
                        ("cb", GDN_HEADS), ("ga", D_MODEL), ("gb", D_MODEL), ("gc", D_MODEL)):
        parts[name] = w[:, o:o + width, :]
        o += width

    def dup(t):
        return jnp.concatenate([t[:, h * HALF:(h + 1) * HALF, :] for h in range(SW_KV_HEADS) for _ in range(2)], axis=1)

    cols = [parts["ga"], parts["gb"], parts["gc"],
            parts["aq"] * (SB_HEAD_DIM ** -0.5), parts["ak"], parts["av"],
            parts["bq"] * (SW_HEAD_DIM ** -0.5), dup(parts["bk"]), dup(parts["bv"]),
            parts["cqkv"], parts["cz"]]
    pad = jnp.zeros((w.shape[0], LANES - 2 * GDN_HEADS, w.shape[2]), w.dtype)
    wab = jnp.concatenate([parts["ca"], parts["cb"], pad], axis=1)
    return jnp.concatenate(cols, axis=1).astype(BF16), wab


def _layer(x, batch, seq, layer, p):
    pbf, pfp, pgb = _inproj(x, p["ln_mix_pre"], p["w_in"], p["w_ab"], p["gdn_conv"], p["gdn_small"],
                            layer, seq, tm=INPROJ_TM)
    ya = _sb_attention(pbf, batch, seq)
    yb = _sw_attention(pbf, p["sinks"], p["slopes"], layer, batch, seq)
    yc = _gdn(pfp, pgb, p["gdn_norm"], layer, batch, seq, rows=GDN_ROWS)
    x = _merge(ya, yb, yc, pbf, x, p["wa"], p["wb"], p["wc"], p["wo"], p["ln_mix_post"], layer, tm=MERGE_TM)
    return _ffn(x, p["ln_ffn_pre"], p["w_up"], p["ffn_conv"], p["w_down"], p["ln_ffn_post"],
                layer, batch, seq, tm=FFN_TM)


def kernel(x, ln_mix_pre, w_in, sw_sinks, gdn_conv, gdn_a_log, gdn_dt_bias, gdn_norm, w_branch_a,
           w_branch_b, w_branch_c, w_out, ln_mix_post, ln_ffn_pre, w_up, ffn_conv, w_down, ln_ffn_post):
    batch, seq, d = x.shape
    depth = w_in.shape[0]
    assert d == D_MODEL and w_up.shape[-1] == 2 * D_FF, (x.shape, w_up.shape)
    assert all(seq % rows == 0 for rows in (INPROJ_TM, SB_TQ, SW_TQ, GDN_ROWS, MERGE_TM, FFN_TM)), seq
    small = jnp.zeros((depth, 8, LANES), F32)
    small = small.at[:, 0, :GDN_HEADS].set(gdn_a_log).at[:, 1, :GDN_HEADS].set(gdn_dt_bias)
    w_main, w_ab = _pack_w_in(w_in)
    p = {
        "ln_mix_pre": ln_mix_pre[:, None, :], "w_in": w_main, "w_ab": w_ab,
        "sinks": sw_sinks, "slopes": jnp.exp2(-8.0 * jnp.arange(1, SW_Q_HEADS + 1, dtype=F32) / SW_Q_HEADS),
        "gdn_conv": gdn_conv, "gdn_small": small, "gdn_norm": gdn_norm[:, None, :],
        "wa": w_branch_a.astype(BF16), "wb": w_branch_b.astype(BF16),
        "wc": w_branch_c.astype(BF16), "wo": w_out.astype(BF16),
        "ln_mix_post": ln_mix_post[:, None, :], "ln_ffn_pre": ln_ffn_pre[:, None, :],
        "w_up": w_up.astype(BF16), "ffn_conv": ffn_conv, "w_down": w_down.astype(BF16),
        "ln_ffn_post": ln_ffn_post[:, None, :],
    }
    h = x.reshape(batch * seq, d)
    for layer in range(depth):
        h = _layer(h, batch, seq, layer, p)
    return h.reshape(batch, seq, d)
```

```python
import functools

import jax
import jax.numpy as jnp
from jax import lax
from jax.experimental import pallas as pl
from jax.experimental.pallas import tpu as pltpu

F32 = jnp.float32
BF16 = jnp.bfloat16
NORM_EPS = 1e-6

D_MODEL = 1024
SB_HEADS, SB_HEAD_DIM = 8, 64
SW_Q_HEADS, SW_KV_HEADS, SW_HEAD_DIM = 8, 2, 64
GDN_HEADS, GDN_HEAD_DIM, GDN_CONV = 4, 128, 4
D_FF, FFN_CONV = 2816, 3
SB_W = SB_HEADS * SB_HEAD_DIM
SW_QW = SW_Q_HEADS * SW_HEAD_DIM
SW_KVW = SW_KV_HEADS * SW_HEAD_DIM
GDN_W = GDN_HEADS * GDN_HEAD_DIM

BLK = 128
LANES = 128
HALF = 64
V7X_VMEM_BYTES = 64 * 1024 * 1024
VMEM_LIMIT = V7X_VMEM_BYTES - 8 * 1024 * 1024

BF_GA, BF_GB, BF_GC, BF_SBQ, BF_SBK, BF_SBV, BF_SWQ, BF_SWK, BF_SWV, BF_COLS = (
    0, 1024, 2048, 3072, 3584, 4096, 4608, 5120, 5376, 5632)
FP_CQKV, FP_CZ, FP_COLS = 0, 1536, 2048

INPROJ_TM, PROJ_TN = 512, 512
SB_TQ, SB_PAIRS = 4 * BLK, 4
SW_TQ = 8 * BLK
GDN_ROWS, GDN_CHUNK = 512, 128
MERGE_TM, MERGE_CK = 1024, 256
FFN_TM, FFN_CK = 1024, 768
MERGE_FFN_TM = 512
LOG2E = 1.4426950408889634
F32_EXP2_ZERO = -104.0 * LOG2E


def _dot(a, b):
    return jnp.dot(a, b, preferred_element_type=F32)


def _dot_nt(a, b):
    return lax.dot_general(a, b, (((1,), (1,)), ((), ())), preferred_element_type=F32)


def _iota(shape, dim):
    return lax.broadcasted_iota(jnp.int32, shape, dim)


def _rms(t, gain):
    return t * lax.rsqrt(jnp.mean(t * t, axis=-1, keepdims=True) + NORM_EPS) * gain


def _shift_rows(h, prev8, s):
    r = pltpu.roll(h, s, axis=0)
    row = _iota(h.shape, 0)
    for t in range(s):
        r = jnp.where(row == t, prev8[8 - s + t:8 - s + t + 1, :], r)
    return r


def _inproj_kernel(x_ref, g_ref, w_ref, wab_ref, conv_ref, small_ref, obf_ref, of_ref, oab_ref, carry_ref,
                   *, tm, tiles_per_seq):
    @pl.when(pl.program_id(0) % tiles_per_seq == 0)
    def _():
        carry_ref[...] = jnp.zeros_like(carry_ref)

    xn = _rms(x_ref[...], g_ref[...]).astype(BF16)
    n = GDN_CHUNK
    ab = _dot_nt(xn, wab_ref[...])
    zab = ab + small_ref[1:2, :]
    softplus = jnp.maximum(zab, 0.0) + jnp.log(1.0 + jnp.exp(-jnp.abs(zab)))
    oab_ref[...] = jnp.where(_iota(ab.shape, 1) < GDN_HEADS,
                             -jnp.exp(small_ref[0:1, :]) * softplus, jax.nn.sigmoid(ab))
    for idx in range(3):
        cols = slice(FP_CQKV + idx * GDN_W, FP_CQKV + (idx + 1) * GDN_W)
        cw = slice(idx * GDN_W, (idx + 1) * GDN_W)
        raw = _dot_nt(xn, w_ref[BF_COLS + cols.start:BF_COLS + cols.stop, :])
        w = conv_ref[:, cw]
        for ci in range(tm // n):
            x = raw[ci * n:(ci + 1) * n, :]
            prev8 = carry_ref[:, cw] if ci == 0 else raw[ci * n - 8:ci * n, :]
            y = w[GDN_CONV - 1:GDN_CONV, :] * x
            for s in range(1, GDN_CONV):
                y = y + w[GDN_CONV - 1 - s:GDN_CONV - s, :] * _shift_rows(x, prev8, s)
            y = y * jax.nn.sigmoid(y)
            for h in range(GDN_HEADS):
                yh = y[:, h * GDN_HEAD_DIM:(h + 1) * GDN_HEAD_DIM]
                if idx < 2:
                    inv = lax.rsqrt(jnp.sum(yh * yh, axis=-1, keepdims=True) + NORM_EPS)
                    yh = yh * (inv * (GDN_HEAD_DIM ** -0.5) if idx == 0 else inv)
                of_ref[ci * n:(ci + 1) * n, cols.start + h * GDN_HEAD_DIM:cols.start + (h + 1) * GDN_HEAD_DIM] = yh
        carry_ref[:, cw] = raw[tm - 8:tm, :]
    z = _dot_nt(xn, w_ref[BF_COLS + FP_CZ:BF_COLS + FP_CZ + GDN_W, :])
    of_ref[:, FP_CZ:FP_CZ + GDN_W] = z * jax.nn.sigmoid(z)
    for c0 in range(0, BF_COLS, PROJ_TN):
        obf_ref[:, c0:c0 + PROJ_TN] = _dot_nt(xn, w_ref[c0:c0 + PROJ_TN, :]).astype(BF16)


def _inproj(x, gain, w, wab, conv_w, small, layer, seq, tm):
    t, d = x.shape

    def whole(a):
        return pl.BlockSpec((None,) + a.shape[1:], lambda i: (layer, 0, 0), pipeline_mode=pl.Buffered(1))

    return pl.pallas_call(
        functools.partial(_inproj_kernel, tm=tm, tiles_per_seq=seq // tm),
        grid=(t // tm,),
        in_specs=[pl.BlockSpec((tm, d), lambda i: (i, 0)), whole(gain), whole(w), whole(wab),
                  whole(conv_w), whole(small)],
        out_specs=[
            pl.BlockSpec((tm, BF_COLS), lambda i: (i, 0)),
            pl.BlockSpec((tm, FP_COLS), lambda i: (i, 0)),
            pl.BlockSpec((tm, LANES), lambda i: (i, 0)),
        ],
        out_shape=[jax.ShapeDtypeStruct((t, BF_COLS), BF16), jax.ShapeDtypeStruct((t, FP_COLS), F32),
                   jax.ShapeDtypeStruct((t, LANES), F32)],
        scratch_shapes=[pltpu.VMEM((8, 3 * GDN_W), F32)],
        compiler_params=pltpu.CompilerParams(
            dimension_semantics=("arbitrary",), vmem_limit_bytes=VMEM_LIMIT),
        name="inproj",
    )(x, gain, w, wab, conv_w, small)


def _sb_kernel(q_ref, k_ref, v_ref, o_ref, acc_ref, lr_ref, least_ref):
    ti = pl.program_id(2)
    lane = _iota((BLK, LANES), 1)
    row = _iota((BLK, LANES), 0)
    first = lane < HALF
    tri = lane < row
    full = lane >= 0
    zero_tile = jnp.zeros((BLK, LANES), F32)
    r2 = _iota((BLK, 2 * LANES), 0)
    c2 = _iota((BLK, 2 * LANES), 1)
    suffix_ones = jnp.where((r2 > c2) | (c2 >= LANES), 1.0, 0.0).astype(BF16)
    suffix_ones = jnp.concatenate([suffix_ones, suffix_ones], axis=0)

    def split_heads(q):
        sel = jnp.concatenate([first] * (q.shape[0] // BLK), axis=0)
        zero = jnp.zeros_like(q)
        return jnp.where(sel, q, zero), jnp.where(sel, zero, q)

    def log_parts(z):
        zs = z * LOG2E
        log_stay = -(jnp.maximum(zs, 0.0) + jnp.log2(1.0 + jnp.exp2(-jnp.abs(zs))))
        return log_stay, log_stay + zs

    def suffix_sums(parts):
        hi = [x.astype(BF16) for x in parts]
        lo = [(x - h.astype(F32)).astype(BF16) for x, h in zip(parts, hi)]
        hilo = jnp.concatenate([jnp.concatenate(hi, axis=0), jnp.concatenate(lo, axis=0)], axis=1)
        sums = _dot(hilo, suffix_ones)
        out, o = [], 0
        for x in parts:
            out.append((sums[o:o + x.shape[0], :LANES], sums[o:o + x.shape[0], LANES:]))
            o += x.shape[0]
        return out

    def generic_block(pp, s, j, valid):
        ps = slice(pp * LANES, (pp + 1) * LANES)
        q0, q1 = split_heads(q_ref[s * BLK:(s + 1) * BLK, ps])
        off = pl.multiple_of(j * BLK, BLK)
        k, v = k_ref[pl.ds(off, BLK), ps], v_ref[pl.ds(off, BLK), ps]
        log_stay, log_take = log_parts(_dot_nt(jnp.concatenate([q0, q1], axis=0), k))
        if valid is not None:
            vm = jnp.concatenate([valid, valid], axis=0)
            log_stay = jnp.where(vm, log_stay, 0.0)
        (later, total), = suffix_sums([log_stay])
        w = jnp.exp2(log_take + later + lr_ref[pp, s])
        if valid is not None:
            w = jnp.where(vm, w, 0.0)
        pv = _dot(w.astype(BF16), v)
        acc_ref[pp, s] += jnp.where(first, pv[:BLK], pv[BLK:])
        lr_ref[pp, s] += total

    def tail(pp, s, j0):
        def cond(c):
            j, live = c
            return jnp.logical_and(j >= 0, live)

        def body(c):
            j, _ = c
            generic_block(pp, s, j, None)
            return j - 1, jnp.max(lr_ref[pp, s]) > F32_EXP2_ZERO

        lax.while_loop(cond, body, (j0, jnp.max(lr_ref[pp, s]) > F32_EXP2_ZERO))

    ns = SB_TQ // BLK

    def band(off):
        base = pl.multiple_of((ns * ti - off) * BLK, BLK)
        P = range(SB_PAIRS)
        order = range(ns + off - 1, -1, -1)
        ps = [slice(pp * LANES, (pp + 1) * LANES) for pp in P]
        qs = [split_heads(q_ref[:, ps[pp]]) for pp in P]
        kw = [k_ref[pl.ds(base, (ns + off) * BLK), ps[pp]] for pp in P]
        vw = [v_ref[pl.ds(base, (ns + off) * BLK), ps[pp]] for pp in P]

        def subs(t):
            return list(range(max(t - off, 0), min(t - off + 2, ns - 1) + 1))

        def lhs(q, t):
            rs = slice(subs(t)[0] * BLK, (subs(t)[-1] + 1) * BLK)
            return jnp.concatenate([q[0][rs], q[1][rs]], axis=0)

        def mask(t):
            if t - off not in subs(t):
                return None
            return jnp.concatenate([tri if s == t - off else full for s in subs(t)] * 2, axis=0)

        z = {t: [_dot_nt(lhs(qs[pp], t), kw[pp][t * BLK:(t + 1) * BLK]) for pp in P] for t in order}
        lp, sums = {}, {}
        for t in order:
            lp[t] = [log_parts(zz) for zz in z[t]]
            m = mask(t)
            sums[t] = [suffix_sums([ls if m is None else jnp.where(m, ls, 0.0)])[0] for ls, _ in lp[t]]
        least = None
        for pp in P:
            lr = [[None, None] for _ in range(ns)]
            acc = [[None, None] for _ in range(ns)]
            for t in order:
                ss = subs(t)
                later, total = sums[t][pp]
                e = lp[t][pp][1] + later
                if any(lr[s][h] is not None for s in ss for h in range(2)):
                    e = e + jnp.concatenate([zero_tile if lr[s][h] is None else lr[s][h]
                                             for h in range(2) for s in ss], axis=0)
                w = jnp.exp2(e)
                if mask(t) is not None:
                    w = jnp.where(mask(t), w, 0.0)
                pv = _dot(w.astype(BF16), vw[pp][t * BLK:(t + 1) * BLK])
                for h in range(2):
                    for j, s in enumerate(ss):
                        rs = slice((h * len(ss) + j) * BLK, (h * len(ss) + j + 1) * BLK)
                        lr[s][h] = total[rs] if lr[s][h] is None else lr[s][h] + total[rs]
                        acc[s][h] = pv[rs] if acc[s][h] is None else acc[s][h] + pv[rs]
            for s in range(ns):
                acc_ref[pp, s] = jnp.where(first, acc[s][0], acc[s][1])
                lr_ref[pp, s] = jnp.concatenate(lr[s], axis=0)
                for h in range(2):
                    least = lr[s][h] if least is None else jnp.maximum(least, lr[s][h])
        least_ref[...] = least

    @pl.when(ti == 0)
    def _():
        band(0)

    @pl.when(ti > 0)
    def _():
        band(2)

    @pl.when(jnp.max(least_ref[...]) > F32_EXP2_ZERO)
    def _():
        for pp in range(SB_PAIRS):
            for s in range(ns):
                tail(pp, s, ns * ti + s - 3)

    for pp in range(SB_PAIRS):
        for s in range(ns):
            o_ref[s * BLK:(s + 1) * BLK, pp * LANES:(pp + 1) * LANES] = acc_ref[pp, s].astype(o_ref.dtype)


def _sb_attention(pbf, batch, seq):
    nt = seq // SB_TQ
    w = SB_PAIRS * LANES
    cq, ck, cv = BF_SBQ // w, BF_SBK // w, BF_SBV // w
    return pl.pallas_call(
        _sb_kernel,
        grid=(batch, SB_W // w, nt),
        in_specs=[
            pl.BlockSpec((SB_TQ, w), lambda b, p, i: (b * nt + i, cq + p)),
            pl.BlockSpec((seq, w), lambda b, p, i: (b, ck + p), pipeline_mode=pl.Buffered(1)),
            pl.BlockSpec((seq, w), lambda b, p, i: (b, cv + p), pipeline_mode=pl.Buffered(1)),
        ],
        out_specs=pl.BlockSpec((SB_TQ, w), lambda b, p, i: (b * nt + i, p)),
        out_shape=jax.ShapeDtypeStruct((batch * seq, SB_W), BF16),
        scratch_shapes=[pltpu.VMEM((SB_PAIRS, SB_TQ // BLK, BLK, LANES), F32),
                        pltpu.VMEM((SB_PAIRS, SB_TQ // BLK, 2 * BLK, LANES), F32),
                        pltpu.VMEM((BLK, LANES), F32)],
        compiler_params=pltpu.CompilerParams(
            dimension_semantics=("arbitrary", "arbitrary", "arbitrary"), vmem_limit_bytes=VMEM_LIMIT),
        name="sb_attention",
    )(pbf, pbf, pbf)


def _sw_kernel(sinks_ref, slopes_ref, q_ref, kp_ref, kc_ref, vp_ref, vc_ref, o_ref, *, layer):
    ti = pl.program_id(1)
    lane = _iota((BLK, LANES), 1)
    row = _iota((BLK, LANES), 0)
    first = lane < HALF
    cur = lane <= row
    dist = jnp.where(cur, row - lane, row - lane + BLK).astype(F32)
    group = SW_Q_HEADS // SW_KV_HEADS
    units = []
    for s in range(SW_TQ // BLK):
        rs = slice(s * BLK, (s + 1) * BLK)
        for hk in range(SW_KV_HEADS):
            ks = slice(hk * LANES, (hk + 1) * LANES)
            kc, vc = kc_ref[rs, ks], vc_ref[rs, ks]
            if s == 0:
                kp, vp = kp_ref[:, ks], vp_ref[:, ks]
            else:
                kp, vp = kc_ref[(s - 1) * BLK:s * BLK, ks], vc_ref[(s - 1) * BLK:s * BLK, ks]
            pairs = range(hk * group // 2, (hk + 1) * group // 2)
            lhs = []
            for p in pairs:
                qp = q_ref[rs, p * LANES:(p + 1) * LANES]
                zero = jnp.zeros_like(qp)
                lhs += [jnp.where(first, qp, zero), jnp.where(first, zero, qp)]
            lhs = jnp.concatenate(lhs, axis=0)
            units.append(dict(s=s, rs=rs, hk=hk, pairs=pairs, vc=vc, vp=vp,
                              s_cur=_dot_nt(lhs, kc), s_prev=_dot_nt(lhs, kp)))
    for u in units:
        scores = []
        for g in range(group):
            head = u["hk"] * group + g
            gs = slice(g * BLK, (g + 1) * BLK)
            sc = jnp.where(cur, u["s_cur"][gs], u["s_prev"][gs]) - slopes_ref[head] * dist
            if u["s"] == 0:
                sc = jnp.where(jnp.logical_or(cur, ti > 0), sc, -jnp.inf)
            scores.append(sc)
        u["scores"] = scores
        u["m"] = [jnp.maximum(jnp.max(sc, axis=-1, keepdims=True), sinks_ref[layer, u["hk"] * group + g])
                  for g, sc in enumerate(scores)]
    for u in units:
        u["e"] = [jnp.exp(sc - m) for sc, m in zip(u["scores"], u["m"])]
        u["denom"] = [jnp.sum(e, axis=-1, keepdims=True) + jnp.exp(sinks_ref[layer, u["hk"] * group + g] - m)
                      for g, (e, m) in enumerate(zip(u["e"], u["m"]))]
    for u in units:
        prob = [(e / d).astype(BF16) for e, d in zip(u["e"], u["denom"])]
        pz = jnp.zeros_like(prob[0])
        prob_c = jnp.concatenate([jnp.where(cur, p, pz) for p in prob], axis=0)
        prob_p = jnp.concatenate([jnp.where(cur, pz, p) for p in prob], axis=0)
        pv = _dot(prob_c, u["vc"]) + _dot(prob_p, u["vp"])
        for t, p in enumerate(u["pairs"]):
            o_ref[u["rs"], p * LANES:(p + 1) * LANES] = jnp.where(
                first, pv[2 * t * BLK:(2 * t + 1) * BLK], pv[(2 * t + 1) * BLK:(2 * t + 2) * BLK]
            ).astype(o_ref.dtype)


def _sw_attention(pbf, sinks, slopes, layer, batch, seq):
    nt = seq // SW_TQ
    kvw = 2 * SW_KVW
    cq, ck, cv = BF_SWQ // SW_QW, BF_SWK // kvw, BF_SWV // kvw
    per = SW_TQ // BLK

    def prev(b, i):
        return b * nt * per + jnp.maximum(i * per - 1, 0)

    smem = pl.BlockSpec(memory_space=pltpu.SMEM)
    return pl.pallas_call(
        functools.partial(_sw_kernel, layer=layer),
        grid=(batch, nt),
        in_specs=[
            smem, smem,
            pl.BlockSpec((SW_TQ, SW_QW), lambda b, i: (b * nt + i, cq)),
            pl.BlockSpec((BLK, kvw), lambda b, i: (prev(b, i), ck)),
            pl.BlockSpec((SW_TQ, kvw), lambda b, i: (b * nt + i, ck)),
            pl.BlockSpec((BLK, kvw), lambda b, i: (prev(b, i), cv)),
            pl.BlockSpec((SW_TQ, kvw), lambda b, i: (b * nt + i, cv)),
        ],
        out_specs=pl.BlockSpec((SW_TQ, SW_QW), lambda b, i: (b * nt + i, 0)),
        out_shape=jax.ShapeDtypeStruct((batch * seq, SW_QW), BF16),
        compiler_params=pltpu.CompilerParams(
            dimension_semantics=("arbitrary", "arbitrary"), vmem_limit_bytes=VMEM_LIMIT),
        name="sw_attention",
    )(sinks, slopes, pbf, pbf, pbf, pbf, pbf)


def _split3(x):
    x1 = x.astype(BF16)
    r1 = x - x1.astype(F32)
    x2 = r1.astype(BF16)
    x3 = (r1 - x2.astype(F32)).astype(BF16)
    return x1, x2, x3


def _gdn_kernel(q_ref, k_ref, v_ref, z_ref, gb_ref, norm_ref, o_ref, state_ref, *, rows):
    n = GDN_CHUNK
    nchunks = rows // n

    @pl.when(pl.program_id(1) == 0)
    def _():
        state_ref[...] = jnp.zeros_like(state_ref)

    r = _iota((n, n), 0)
    c = _iota((n, n), 1)
    causal = r >= c
    strict = r > c
    lower_ones = jnp.where(causal, 1.0, 0.0).astype(BF16)

    def sub_blocks(ls):
        return ((r >> (ls + 1)) == (c >> (ls + 1))) & ((r >> ls) != (c >> ls))

    probs = []
    for ci in range(nchunks):
        rs = slice(ci * n, (ci + 1) * n)
        gbeta = gb_ref[rs, :]
        gc_all = sum(_dot(lower_ones, part) for part in _split3(gbeta))
        gc_t = gc_all.T
        for h in range(GDN_HEADS):
            hs = slice(h * GDN_HEAD_DIM, (h + 1) * GDN_HEAD_DIM)
            q, k, v = q_ref[rs, hs], k_ref[rs, hs], v_ref[rs, hs]
            gc = jnp.broadcast_to(gc_all[:, h:h + 1], (n, n))
            gc_row = jnp.broadcast_to(gc_t[h:h + 1, :], (n, n))
            g_tot = jnp.broadcast_to(gc[n - 1:n, :], (n, n))
            beta_b = jnp.broadcast_to(gbeta[:, GDN_HEADS + h:GDN_HEADS + h + 1], (n, n))
            decay = jnp.exp(gc - gc_row)
            k_beta = k * beta_b
            k16 = k.astype(BF16)
            e_gc = jnp.exp(gc)
            probs.append(dict(
                rs=rs, hs=hs, h=h,
                lower=jnp.where(strict, _dot_nt(k_beta.astype(BF16), k16) * decay, 0.0),
                a=jnp.where(causal, _dot_nt(q.astype(BF16), k16) * decay, 0.0).astype(BF16),
                qd=(q * e_gc).astype(BF16),
                kdt=(k * jnp.exp(g_tot - gc)).T.astype(BF16),
                gl=jnp.exp(g_tot),
                rhs=jnp.concatenate([v * beta_b, k_beta * e_gc], axis=1)))
    eye = jnp.where(r == c, 1.0, 0.0).astype(F32)
    zero16 = jnp.zeros((n, n), BF16)
    lower16 = [p["lower"].astype(BF16) for p in probs]
    xs = [eye - jnp.where((r >> 1) == (c >> 1), p["lower"], 0.0) for p in probs]
    for ls in range(1, n.bit_length() - 1):
        x16 = [x.astype(BF16) for x in xs]
        ys = [_dot(xb, jnp.where(sub_blocks(ls), l16, zero16)).astype(BF16) for xb, l16 in zip(x16, lower16)]
        xs = [x - _dot(y, xb) for x, y, xb in zip(xs, ys, x16)]
    for p, x in zip(probs, xs):
        uw = _dot(x.astype(BF16), p["rhs"].astype(BF16))
        p["u"] = uw[:, :n]
        p["wq"] = jnp.concatenate([uw[:, n:].astype(BF16), p["qd"]], axis=0)
        p["ak"] = jnp.concatenate([p["a"], p["kdt"]], axis=0)

    gain = norm_ref[...]
    states = [state_ref[h] for h in range(GDN_HEADS)]
    for p in probs:
        h = p["h"]
        sr = _dot(p["wq"], states[h].astype(BF16))
        vn16 = (p["u"] - sr[:n]).astype(BF16)
        vr = _dot(p["ak"], vn16)
        o = sr[n:] + vr[:n]
        states[h] = states[h] * p["gl"] + vr[n:]
        o_ref[p["rs"], p["hs"]] = (_rms(o, gain) * z_ref[p["rs"], p["hs"]]).astype(o_ref.dtype)
    for h in range(GDN_HEADS):
        state_ref[h] = states[h]


def _gdn(pfp, pgb, norm, layer, batch, seq, rows):
    steps = seq // rows
    w = GDN_W
    cq = FP_CQKV // w

    def blk(col):
        return pl.BlockSpec((rows, w), lambda b, i: (b * steps + i, col))

    return pl.pallas_call(
        functools.partial(_gdn_kernel, rows=rows),
        grid=(batch, steps),
        in_specs=[
            blk(cq), blk(cq + 1), blk(cq + 2), blk(FP_CZ // w),
            pl.BlockSpec((rows, LANES), lambda b, i: (b * steps + i, 0)),
            pl.BlockSpec((None, 1, GDN_HEAD_DIM), lambda b, i: (layer, 0, 0)),
        ],
        out_specs=pl.BlockSpec((rows, w), lambda b, i: (b * steps + i, 0)),
        out_shape=jax.ShapeDtypeStruct((batch * seq, w), BF16),
        scratch_shapes=[pltpu.VMEM((GDN_HEADS, GDN_HEAD_DIM, GDN_HEAD_DIM), F32)],
        compiler_params=pltpu.CompilerParams(
            dimension_semantics=("arbitrary", "arbitrary"), vmem_limit_bytes=VMEM_LIMIT),
        name="gdn",
    )(pfp, pfp, pfp, pfp, pgb, norm)


def _merge_kernel(ya_ref, yb_ref, yc_ref, ga_ref, gb_ref, gc_ref, x_ref,
                  wa_ref, wb_ref, wc_ref, wo_ref, gain_ref, o_ref, m_ref):
    ya, yb, yc = ya_ref[...], yb_ref[...], yc_ref[...]
    for c0 in range(0, D_MODEL, MERGE_CK):
        cs = slice(c0, c0 + MERGE_CK)
        m_ref[:, cs] = (jax.nn.sigmoid(ga_ref[:, cs].astype(F32)) * _dot(ya, wa_ref[:, cs])
                        + jax.nn.sigmoid(gb_ref[:, cs].astype(F32)) * _dot(yb, wb_ref[:, cs])
                        + jax.nn.sigmoid(gc_ref[:, cs].astype(F32)) * _dot(yc, wc_ref[:, cs])).astype(BF16)
    o_ref[...] = x_ref[...] + _rms(_dot(m_ref[...], wo_ref[...]), gain_ref[...])


def _merge(ya, yb, yc, pbf, x, wa, wb, wc, wo, gain, layer, tm):
    t, d = x.shape

    def rows(width, col=0):
        return pl.BlockSpec((tm, width), lambda i: (i, col))

    def whole(a):
        return pl.BlockSpec((None,) + a.shape[1:], lambda i: (layer, 0, 0))

    return pl.pallas_call(
        _merge_kernel,
        grid=(t // tm,),
        in_specs=[rows(SB_W), rows(SW_QW), rows(GDN_W),
                  rows(d, BF_GA // d), rows(d, BF_GB // d), rows(d, BF_GC // d), rows(d),
                  whole(wa), whole(wb), whole(wc), whole(wo), whole(gain)],
        out_specs=rows(d),
        out_shape=jax.ShapeDtypeStruct((t, d), F32),
        scratch_shapes=[pltpu.VMEM((tm, d), BF16)],
        compiler_params=pltpu.CompilerParams(
            dimension_semantics=("arbitrary",), vmem_limit_bytes=VMEM_LIMIT),
        name="merge",
    )(ya, yb, yc, pbf, pbf, pbf, x, wa, wb, wc, wo, gain)


SQRT_2_OVER_PI = 0.7978845608028654


def _gelu_tanh(x):
    inner = x * (SQRT_2_OVER_PI + (SQRT_2_OVER_PI * 0.044715) * (x * x))
    return (0.5 * x) * (1.0 + jnp.tanh(inner))


def _ffn_kernel(x_ref, gpre_ref, wup_ref, conv_ref, wdn_ref, gpost_ref, o_ref, carry_ref, f_ref, *, tm):
    @pl.when(pl.program_id(1) == 0)
    def _():
        carry_ref[...] = jnp.zeros_like(carry_ref)

    x = x_ref[...]
    hn = _rms(x, gpre_ref[...]).astype(BF16)

    def conv(col, width):
        cs = slice(col, col + width)
        hid = _dot(hn, wup_ref[:, cs])
        prev8 = carry_ref[:, cs]
        w = conv_ref[:, cs]
        y = w[FFN_CONV - 1:FFN_CONV, :] * hid
        for s in range(1, FFN_CONV):
            y = y + w[FFN_CONV - 1 - s:FFN_CONV - s, :] * _shift_rows(hid, prev8, s)
        carry_ref[:, cs] = hid[tm - 8:tm, :]
        return y

    col = 0
    while col < D_FF:
        width = min(FFN_CK, D_FF - col)
        f_gate = conv(col, width)
        f_up = conv(D_FF + col, width)
        f_ref[:, col:col + width] = (_gelu_tanh(f_gate) * f_up).astype(BF16)
        col += width
    o_ref[...] = x + _rms(_dot(f_ref[...], wdn_ref[...]), gpost_ref[...])


def _ffn(x, gpre, wup, conv_w, wdn, gpost, layer, batch, seq, tm):
    steps = seq // tm
    d = x.shape[1]

    def whole(a):
        return pl.BlockSpec((None,) + a.shape[1:], lambda b, i: (layer, 0, 0), pipeline_mode=pl.Buffered(1))

    return pl.pallas_call(
        functools.partial(_ffn_kernel, tm=tm),
        grid=(batch, steps),
        in_specs=[pl.BlockSpec((tm, d), lambda b, i: (b * steps + i, 0)),
                  whole(gpre), whole(wup), whole(conv_w), whole(wdn), whole(gpost)],
        out_specs=pl.BlockSpec((tm, d), lambda b, i: (b * steps + i, 0)),
        out_shape=jax.ShapeDtypeStruct(x.shape, F32),
        scratch_shapes=[pltpu.VMEM((8, 2 * D_FF), F32), pltpu.VMEM((tm, D_FF), BF16)],
        compiler_params=pltpu.CompilerParams(
            dimension_semantics=("arbitrary", "arbitrary"), vmem_limit_bytes=VMEM_LIMIT),
        name="ffn",
    )(x, gpre, wup, conv_w, wdn, gpost)


def _merge_ffn_kernel(ya_ref, yb_ref, yc_ref, ga_ref, gb_ref, gc_ref, x_ref, wa_ref, wb_ref, wc_ref, wo_ref,
                      gmix_ref, gpre_ref, wup_ref, conv_ref, wdn_ref, gpost_ref, o_ref,
                      m_ref, xmid_ref, carry_ref, f_ref, *, tm):
    _merge_kernel(ya_ref, yb_ref, yc_ref, ga_ref, gb_ref, gc_ref, x_ref, wa_ref, wb_ref, wc_ref, wo_ref,
                  gmix_ref, xmid_ref, m_ref)
    _ffn_kernel(xmid_ref, gpre_ref, wup_ref, conv_ref, wdn_ref, gpost_ref, o_ref, carry_ref, f_ref, tm=tm)


def _merge_ffn(ya, yb, yc, pbf, x, wa, wb, wc, wo, gmix, gpre, wup, conv_w, wdn, gpost, layer, batch, seq, tm):
    steps = seq // tm
    d = x.shape[1]

    def rows(width, col=0):
        return pl.BlockSpec((tm, width), lambda b, i: (b * steps + i, col))

    def whole(a):
        return pl.BlockSpec((None,) + a.shape[1:], lambda b, i: (layer, 0, 0), pipeline_mode=pl.Buffered(1))

    return pl.pallas_call(
        functools.partial(_merge_ffn_kernel, tm=tm),
        grid=(batch, steps),
        in_specs=[rows(SB_W), rows(SW_QW), rows(GDN_W),
                  rows(d, BF_GA // d), rows(d, BF_GB // d), rows(d, BF_GC // d), rows(d),
                  whole(wa), whole(wb), whole(wc), whole(wo), whole(gmix),
                  whole(gpre), whole(wup), whole(conv_w), whole(wdn), whole(gpost)],
        out_specs=rows(d),
        out_shape=jax.ShapeDtypeStruct(x.shape, F32),
        scratch_shapes=[pltpu.VMEM((tm, d), BF16), pltpu.VMEM((tm, d), F32),
                        pltpu.VMEM((8, 2 * D_FF), F32), pltpu.VMEM((tm, D_FF), BF16)],
        compiler_params=pltpu.CompilerParams(
            dimension_semantics=("arbitrary", "arbitrary"), vmem_limit_bytes=VMEM_LIMIT),
        name="merge_ffn",
    )(ya, yb, yc, pbf, pbf, pbf, x, wa, wb, wc, wo, gmix, gpre, wup, conv_w, wdn, gpost)


def _pack_w_in(w):
    w = jnp.swapaxes(w, 1, 2).astype(BF16)
    o = 0
    parts = {}
    for name, width in (("aq", SB_W), ("ak", SB_W), ("av", SB_W), ("bq", SW_QW), ("bk", SW_KVW),
                        ("bv", SW_KVW), ("cqkv", 3 * GDN_W), ("cz", GDN_W), ("ca", GDN_HEADS),
                        ("cb", GDN_HEADS), ("ga", D_MODEL), ("gb", D_MODEL), ("gc", D_MODEL)):
        parts[name] = w[:, o:o + width, :]
        o += width

    def dup(t):
        return jnp.concatenate([t[:, h * HALF:(h + 1) * HALF, :] for h in range(SW_KV_HEADS) for _ in range(2)], axis=1)

    cols = [parts["ga"], parts["gb"], parts["gc"],
            parts["aq"] * (SB_HEAD_DIM ** -0.5), parts["ak"], parts["av"],
            parts["bq"] * (SW_HEAD_DIM ** -0.5), dup(parts["bk"]), dup(parts["bv"]),
            parts["cqkv"], parts["cz"]]
    pad = jnp.zeros((w.shape[0], LANES - 2 * GDN_HEADS, w.shape[2]), w.dtype)
    wab = jnp.concatenate([parts["ca"], parts["cb"], pad], axis=1)
    return jnp.concatenate(cols, axis=1).astype(BF16), wab


def _layer(x, batch, seq, layer, p):
    pbf, pfp, pgb = _inproj(x, p["ln_mix_pre"], p["w_in"], p["w_ab"], p["gdn_conv"], p["gdn_small"],
                            layer, seq, tm=INPROJ_TM)
    ya = _sb_attention(pbf, batch, seq)
    yb = _sw_attention(pbf, p["sinks"], p["slopes"], layer, batch, seq)
    yc = _gdn(pfp, pgb, p["gdn_norm"], layer, batch, seq, rows=GDN_ROWS)
    return _merge_ffn(ya, yb, yc, pbf, x, p["wa"], p["wb"], p["wc"], p["wo"], p["ln_mix_post"],
                      p["ln_ffn_pre"], p["w_up"], p["ffn_conv"], p["w_down"], p["ln_ffn_post"],
                      layer, batch, seq, tm=MERGE_FFN_TM)


def kernel(x, ln_mix_pre, w_in, sw_sinks, gdn_conv, gdn_a_log, gdn_dt_bias, gdn_norm, w_branch_a,
           w_branch_b, w_branch_c, w_out, ln_mix_post, ln_ffn_pre, w_up, ffn_conv, w_down, ln_ffn_post):
    batch, seq, d = x.shape
    depth = w_in.shape[0]
    assert d == D_MODEL and w_up.shape[-1] == 2 * D_FF, (x.shape, w_up.shape)
    assert all(seq % rows == 0 for rows in (INPROJ_TM, SB_TQ, SW_TQ, GDN_ROWS, MERGE_FFN_TM)), seq
    small = jnp.zeros((depth, 8, LANES), F32)
    small = small.at[:, 0, :GDN_HEADS].set(gdn_a_log).at[:, 1, :GDN_HEADS].set(gdn_dt_bias)
    w_main, w_ab = _pack_w_in(w_in)
    p = {
        "ln_mix_pre": ln_mix_pre[:, None, :], "w_in": w_main, "w_ab": w_ab,
        "sinks": sw_sinks, "slopes": jnp.exp2(-8.0 * jnp.arange(1, SW_Q_HEADS + 1, dtype=F32) / SW_Q_HEADS),
        "gdn_conv": gdn_conv, "gdn_small": small, "gdn_norm": gdn_norm[:, None, :],
        "wa": w_branch_a.astype(BF16), "wb": w_branch_b.astype(BF16),
        "wc": w_branch_c.astype(BF16), "wo": w_out.astype(BF16),
        "ln_mix_post": ln_mix_post[:, None, :], "ln_ffn_pre": ln_ffn_pre[:, None, :],
        "w_up": w_up.astype(BF16), "ffn_conv": ffn_conv, "w_down": w_down.astype(BF16),
        "ln_ffn_post": ln_ffn_post[:, None, :],
    }
    h = x.reshape(batch * seq, d)
    for layer in range(depth):
        h = _layer(h, batch, seq, layer, p)
    return h.reshape(batch, seq, d)
```
